```python
import math
import jax, jax.numpy as jnp
from jax import lax
import numpy as np

D_MODEL = 1024
BATCH = 4
SEQ = 8192
DEPTH = 1

D_LRU = D_MODEL
LRU_BLOCKS = 16
LRU_BW = D_LRU // LRU_BLOCKS
CONV_W = 4
LRU_C = 8.0
N_HEADS = 8
HEAD_DIM = 128
D_ATTN = N_HEADS * HEAD_DIM
BLOCK_Q = 128
D_FF = 4 * D_MODEL
N_BRANCH = 2
RMS_EPS = 1e-6
IN_SPLITS = (D_LRU, D_LRU, D_ATTN, D_ATTN, D_ATTN, N_BRANCH * D_MODEL, N_HEADS)
D_IN = sum(IN_SPLITS)

kernel_name = "hybrid_rglru_fox_gated_block"


def rms_norm(x, g):
    xf = x.astype(jnp.float32)
    y = xf * lax.rsqrt(jnp.mean(xf * xf, axis=-1, keepdims=True) + RMS_EPS)
    return (y * g.astype(jnp.float32)).astype(x.dtype)


def causal_depthwise_conv(x, w, b):
    S = x.shape[1]
    xp = jnp.pad(x, ((0, 0), (CONV_W - 1, 0), (0, 0)))
    out = b
    for k in range(CONV_W):
        out = out + xp[:, k:k + S, :] * w[k]
    return out


def block_diag_linear(x, w, b):
    B, S, _ = x.shape
    xb = x.reshape(B, S, LRU_BLOCKS, LRU_BW)
    y = jnp.einsum('bsnc,ncd->bsnd', xb, w).reshape(B, S, D_LRU)
    return y + b


def rg_lru(x, wa, ba, wx, bx, lam):
    r = jax.nn.sigmoid(block_diag_linear(x, wa, ba).astype(jnp.float32))
    i = jax.nn.sigmoid(block_diag_linear(x, wx, bx).astype(jnp.float32))
    log_a = -LRU_C * r * jax.nn.softplus(-lam.astype(jnp.float32))
    a = jnp.exp(log_a)
    mult = jnp.sqrt(-jnp.expm1(2.0 * log_a))
    bterm = mult * (i * x.astype(jnp.float32))

    def combine(left, right):
        a1, b1 = left
        a2, b2 = right
        return a1 * a2, a2 * b1 + b2

    _, h = lax.associative_scan(combine, (a, bterm), axis=1)
    return h.astype(x.dtype)


def fox_attention(q, k, v, log_f):
    S = q.shape[1]
    scale = 1.0 / math.sqrt(HEAD_DIM)
    F = jnp.cumsum(log_f.astype(jnp.float32), axis=1)
    F = jnp.transpose(F, (0, 2, 1))
    outs = []
    for blk in range(S // BLOCK_Q):
        q0, q1 = blk * BLOCK_Q, (blk + 1) * BLOCK_Q
        qb = q[:, q0:q1]
        kb = k[:, :q1]
        vb = v[:, :q1]
        s = jnp.einsum('bqhd,bkhd->bhqk', qb, kb).astype(jnp.float32) * scale
        s = s + F[:, :, q0:q1, None] - F[:, :, None, :q1]
        q_pos = jnp.arange(q0, q1)
        k_pos = jnp.arange(q1)
        mask = q_pos[:, None] >= k_pos[None, :]
        s = jnp.where(mask[None, None], s, -jnp.inf)
        p = jax.nn.softmax(s, axis=-1).astype(v.dtype)
        outs.append(jnp.einsum('bhqk,bkhd->bqhd', p, vb))
    return jnp.concatenate(outs, axis=1)


def setup_inputs(seed: int = 0) -> dict:
    key = jax.random.key(seed)
    ks = jax.random.split(key, 20)
    f32 = jnp.float32
    n = lambda k, shape, s: jax.random.normal(k, shape, f32) * s
    x = jax.random.normal(ks[0], (BATCH, SEQ, D_MODEL), f32)
    norm_mix_g = 1.0 + n(ks[1], (D_MODEL,), 0.02)
    w_in = n(ks[2], (D_MODEL, D_IN), D_MODEL ** -0.5)
    conv_w = n(ks[3], (CONV_W, D_LRU), CONV_W ** -0.5)
    conv_b = n(ks[4], (D_LRU,), 0.02)
    lru_wa = n(ks[5], (LRU_BLOCKS, LRU_BW, LRU_BW), LRU_BW ** -0.5)
    lru_ba = n(ks[6], (D_LRU,), 0.02)
    lru_wx = n(ks[7], (LRU_BLOCKS, LRU_BW, LRU_BW), LRU_BW ** -0.5)
    lru_bx = n(ks[8], (D_LRU,), 0.02)
    a0 = jax.random.uniform(ks[9], (D_LRU,), f32, 0.9, 0.999)
    s0 = a0 ** (1.0 / LRU_C)
    lru_lambda = jnp.log(s0) - jnp.log1p(-s0)
    forget_b = 2.0 + n(ks[10], (N_HEADS,), 0.5)
    w_branch_a = n(ks[11], (D_LRU, D_MODEL), D_LRU ** -0.5)
    w_branch_b = n(ks[12], (D_ATTN, D_MODEL), D_ATTN ** -0.5)
    w_out = n(ks[13], (D_MODEL, D_MODEL), D_MODEL ** -0.5)
    norm_mlp_g = 1.0 + n(ks[14], (D_MODEL,), 0.02)
    w_up = n(ks[15], (D_MODEL, D_FF), D_MODEL ** -0.5)
    w_down = n(ks[16], (D_FF, D_MODEL), D_FF ** -0.5)
    norm_final_g = 1.0 + n(ks[17], (D_MODEL,), 0.02)
    return {"x": x, "norm_mix_g": norm_mix_g, "w_in": w_in, "conv_w": conv_w,
            "conv_b": conv_b, "lru_wa": lru_wa, "lru_ba": lru_ba, "lru_wx": lru_wx,
            "lru_bx": lru_bx, "lru_lambda": lru_lambda, "forget_b": forget_b,
            "w_branch_a": w_branch_a, "w_branch_b": w_branch_b, "w_out": w_out,
            "norm_mlp_g": norm_mlp_g, "w_up": w_up, "w_down": w_down,
            "norm_final_g": norm_final_g}


def reference(x, norm_mix_g, w_in, conv_w, conv_b, lru_wa, lru_ba, lru_wx, lru_bx,
              lru_lambda, forget_b, w_branch_a, w_branch_b, w_out, norm_mlp_g,
              w_up, w_down, norm_final_g):
    B, S, _ = x.shape
    for _layer in range(DEPTH):
        u = rms_norm(x, norm_mix_g)
        proj = u @ w_in
        cuts = list(np.cumsum(IN_SPLITS)[:-1])
        x_lru, g_lru, q, k, v, gates, f_logit = jnp.split(proj, cuts, axis=-1)

        xa = causal_depthwise_conv(x_lru, conv_w, conv_b)
        ha = rg_lru(xa, lru_wa, lru_ba, lru_wx, lru_bx, lru_lambda)
        ya = (jax.nn.gelu(g_lru) * ha) @ w_branch_a

        log_f = jax.nn.log_sigmoid((f_logit + forget_b).astype(jnp.float32))
        qh = q.reshape(B, S, N_HEADS, HEAD_DIM)
        kh = k.reshape(B, S, N_HEADS, HEAD_DIM)
        vh = v.reshape(B, S, N_HEADS, HEAD_DIM)
        ob = fox_attention(qh, kh, vh, log_f).reshape(B, S, D_ATTN)
        yb = ob @ w_branch_b

        g_a, g_b = jnp.split(jax.nn.sigmoid(gates), N_BRANCH, axis=-1)
        x = x + (g_a * ya + g_b * yb) @ w_out

        m = rms_norm(x, norm_mlp_g)
        h = jnp.square(jax.nn.relu(m @ w_up))
        x = x + h @ w_down
    return rms_norm(x, norm_final_g)
```

```python
import functools
import math

import jax
import jax.numpy as jnp
from jax import lax
from jax.experimental import pallas as pl
from jax.experimental.pallas import tpu as pltpu

D_MODEL = 1024
N_HEADS = 8
HEAD_DIM = 128
LRU_BLOCKS = 16
LRU_BW = 64
CONV_W = 4
LRU_C = 8.0
D_FF = 4 * D_MODEL
RMS_EPS = 1e-6
LANES = 128
SUBLANES = 8
LOG2E = 1.4426950408889634
Q_SCALE = LOG2E / math.sqrt(HEAD_DIM)
VMEM_LIMIT = 56 * 1024 * 1024

BF16 = jnp.bfloat16
F32 = jnp.float32


def _dot(a, b):
    return jnp.dot(a, b, preferred_element_type=F32)


def _rms(x, g):
    return x * lax.rsqrt(jnp.mean(x * x, axis=-1, keepdims=True) + RMS_EPS) * g


def _softplus(x):
    return jnp.maximum(x, 0.0) + jnp.log1p(jnp.exp(-jnp.abs(x)))


def _sigmoid(x):
    return 0.5 * jnp.tanh(0.5 * x) + 0.5


def _gelu_tanh(x):
    c = math.sqrt(2.0 / math.pi)
    return 0.5 * x * (1.0 + jnp.tanh(c * (x + 0.044715 * (x * x * x))))


def _const_spec(shape):
    nd = len(shape)
    return pl.BlockSpec(shape, lambda *_: (0,) * nd, pipeline_mode=pl.Buffered(1))


def _in_proj_kernel(x_ref, g_ref, w_ref, wf_ref, pf_ref, qkv_ref, fl_ref):
    u = _rms(x_ref[...], g_ref[...]).astype(BF16)
    fl_ref[...] = _dot(u, wf_ref[...])
    for j in range(4):
        cols = slice(j * D_MODEL, (j + 1) * D_MODEL)
        pf_ref[:, cols] = _dot(u, w_ref[:, cols])
    for j in range(3):
        cols = slice((4 + j) * D_MODEL, (5 + j) * D_MODEL)
        res = _dot(u, w_ref[:, cols])
        if j == 0:
            res = res * Q_SCALE
        for h in range(N_HEADS):
            qkv_ref[j, h] = res[:, h * HEAD_DIM:(h + 1) * HEAD_DIM].astype(BF16)


def _in_proj(x2, g, w_main, w_f, tm):
    T = x2.shape[0]
    return pl.pallas_call(
        _in_proj_kernel,
        grid=(T // tm,),
        in_specs=[
            pl.BlockSpec((tm, D_MODEL), lambda i: (i, 0)),
            _const_spec((1, D_MODEL)),
            _const_spec((D_MODEL, 7 * D_MODEL)),
            _const_spec((D_MODEL, LANES)),
        ],
        out_specs=[
            pl.BlockSpec((tm, 4 * D_MODEL), lambda i: (i, 0)),
            pl.BlockSpec((3, N_HEADS, tm, HEAD_DIM), lambda i: (0, 0, i, 0)),
            pl.BlockSpec((tm, LANES), lambda i: (i, 0)),
        ],
        out_shape=[
            jax.ShapeDtypeStruct((T, 4 * D_MODEL), F32),
            jax.ShapeDtypeStruct((3, N_HEADS, T, HEAD_DIM), BF16),
            jax.ShapeDtypeStruct((T, LANES), F32),
        ],
        compiler_params=pltpu.CompilerParams(
            dimension_semantics=("arbitrary",), vmem_limit_bytes=VMEM_LIMIT),
        name="in_proj",
    )(x2, g, w_main, w_f)


def _lru_kernel(xl_ref, gl_ref, ga_ref, fl_ref, cw_ref, cb_ref, wd_ref, ba_ref, bx_ref,
                lam_ref, fb_ref, wa_ref,
                yag_ref, f_ref, ft_ref,
                xp_sc, a_sc, b_sc, hc_sc, fc_sc, *, tm):
    s = pl.program_id(1)

    @pl.when(s == 0)
    def _():
        xp_sc[0:SUBLANES, :] = jnp.zeros((SUBLANES, D_MODEL), F32)
        hc_sc[...] = jnp.zeros_like(hc_sc)
        fc_sc[...] = jnp.zeros_like(fc_sc)

    xp_sc[SUBLANES:SUBLANES + tm, :] = xl_ref[...]
    xa = cb_ref[...] + cw_ref[0:1, :] * xp_sc[pl.ds(SUBLANES - 3, tm), :]
    for k in range(1, CONV_W):
        xa = xa + cw_ref[k:k + 1, :] * xp_sc[pl.ds(SUBLANES - 3 + k, tm), :]
    xp_sc[0:SUBLANES, :] = xp_sc[tm:tm + SUBLANES, :]

    xab = xa.astype(BF16)
    c_all = -LRU_C * _softplus(-lam_ref[...])
    for j in range(D_MODEL // LANES):
        cols = slice(j * LANES, (j + 1) * LANES)
        ri = _dot(xab[:, cols], wd_ref[j])
        r = _sigmoid(ri[:, :LANES] + ba_ref[:, cols])
        i = _sigmoid(ri[:, LANES:] + bx_ref[:, cols])
        log_a = c_all[:, cols] * r
        a = jnp.exp(log_a)
        a_sc[:, cols] = a
        b_sc[:, cols] = jnp.sqrt(1.0 - a * a) * (i * xa[:, cols])

    row = lax.broadcasted_iota(jnp.int32, (SUBLANES, D_MODEL), 0)

    def group(g, carry):
        r0 = pl.multiple_of(g * SUBLANES, SUBLANES)
        a = a_sc[pl.ds(r0, SUBLANES), :]
        b = b_sc[pl.ds(r0, SUBLANES), :]
        for d in (1, 2, 4):
            keep = row >= d
            a_sh = jnp.where(keep, pltpu.roll(a, d, axis=0), 1.0)
            b_sh = jnp.where(keep, pltpu.roll(b, d, axis=0), 0.0)
            b = a * b_sh + b
            a = a * a_sh
        h = b + a * carry
        b_sc[pl.ds(r0, SUBLANES), :] = h
        return jnp.broadcast_to(h[SUBLANES - 1:SUBLANES, :], (SUBLANES, D_MODEL))

    hc_sc[...] = lax.fori_loop(0, tm // SUBLANES, group, hc_sc[...])

    y = (_gelu_tanh(gl_ref[...]) * b_sc[...]).astype(BF16)
    yag_ref[...] = _sigmoid(ga_ref[...]) * _dot(y, wa_ref[...])

    lf = -_softplus(-(fl_ref[...] + fb_ref[...])) * LOG2E
    rowf = lax.broadcasted_iota(jnp.int32, (tm, LANES), 0)
    d = 1
    while d < tm:
        lf = lf + jnp.where(rowf >= d, pltpu.roll(lf, d, axis=0), 0.0)
        d *= 2
    fblk = lf + fc_sc[...]
    fc_sc[...] = fblk[tm - 1:tm, :]
    f_ref[...] = fblk
    ft_ref[...] = fblk.T[0:N_HEADS, :]


def _lru(pf, fl, cw, cb, wd, ba, bx, lam, fb, wa, B, S, tm):
    T = B * S
    ns = S // tm
    row_blk = lambda c: pl.BlockSpec((tm, D_MODEL), lambda b, s, c=c: (b * ns + s, c))
    return pl.pallas_call(
        functools.partial(_lru_kernel, tm=tm),
        grid=(B, ns),
        in_specs=[
            row_blk(0), row_blk(1), row_blk(2),
            pl.BlockSpec((tm, LANES), lambda b, s: (b * ns + s, 0)),
            _const_spec((CONV_W, D_MODEL)),
            _const_spec((1, D_MODEL)),
            _const_spec((D_MODEL // LANES, LANES, 2 * LANES)),
            _const_spec((1, D_MODEL)),
            _const_spec((1, D_MODEL)),
            _const_spec((1, D_MODEL)),
            _const_spec((1, LANES)),
            _const_spec((D_MODEL, D_MODEL)),
        ],
        out_specs=[
            pl.BlockSpec((tm, D_MODEL), lambda b, s: (b * ns + s, 0)),
            pl.BlockSpec((tm, LANES), lambda b, s: (b * ns + s, 0)),
            pl.BlockSpec((None, N_HEADS, tm), lambda b, s: (b, 0, s)),
        ],
        out_shape=[
            jax.ShapeDtypeStruct((T, D_MODEL), F32),
            jax.ShapeDtypeStruct((T, LANES), F32),
            jax.ShapeDtypeStruct((B, N_HEADS, S), F32),
        ],
        scratch_shapes=[
            pltpu.VMEM((tm + SUBLANES, D_MODEL), F32),
            pltpu.VMEM((tm, D_MODEL), F32),
            pltpu.VMEM((tm, D_MODEL), F32),
            pltpu.VMEM((SUBLANES, D_MODEL), F32),
            pltpu.VMEM((1, LANES), F32),
        ],
        compiler_params=pltpu.CompilerParams(
            dimension_semantics=("arbitrary", "arbitrary"), vmem_limit_bytes=VMEM_LIMIT),
        name="lru",
    )(pf, pf, pf, fl, cw, cb, wd, ba, bx, lam, fb, wa)


def _attn_kernel(q_ref, k_ref, v_ref, ft_ref, fq_ref, o_ref, *, tq):
    h = pl.program_id(1)
    qi = pl.program_id(2)
    q = q_ref[...]
    lane = lax.broadcasted_iota(jnp.int32, (tq, LANES), 1)
    f_t = jnp.sum(jnp.where(lane == h, fq_ref[...], 0.0), axis=1, keepdims=True)

    def step(j, carry, masked):
        m_row, l, acc = carry
        c0 = pl.multiple_of(j * tq, tq)
        kb = k_ref[pl.ds(c0, tq), :]
        vb = v_ref[pl.ds(c0, tq), :]
        z = lax.dot_general(q, kb, (((1,), (1,)), ((), ())), preferred_element_type=F32)
        z = z - ft_ref[:, pl.ds(c0, tq)]
        if masked:
            r = lax.broadcasted_iota(jnp.int32, (tq, tq), 0)
            c = lax.broadcasted_iota(jnp.int32, (tq, tq), 1)
            z = jnp.where(r >= c, z, -jnp.inf)
        m_new = jnp.maximum(m_row, jnp.max(z, axis=1, keepdims=True) + f_t)
        p = jnp.exp2(z - (m_new - f_t))
        alpha = jnp.exp2(m_row - m_new)
        l = alpha * l + jnp.sum(p, axis=1, keepdims=True)
        acc = alpha * acc + _dot(p.astype(BF16), vb)
        return m_new, l, acc

    init = (jnp.full((tq, 1), -jnp.inf, F32), jnp.zeros((tq, 1), F32),
            jnp.zeros((tq, HEAD_DIM), F32))
    carry = lax.fori_loop(0, qi, functools.partial(step, masked=False), init)
    _, l, acc = step(qi, carry, True)
    o_ref[...] = (acc / l).astype(BF16)


def _attn(qkv, ft, f, B, S, tq):
    T = B * S
    nq = S // tq
    return pl.pallas_call(
        functools.partial(_attn_kernel, tq=tq),
        grid=(B, N_HEADS, nq),
        in_specs=[
            pl.BlockSpec((None, None, tq, HEAD_DIM), lambda b, h, i: (0, h, b * nq + i, 0)),
            pl.BlockSpec((None, None, S, HEAD_DIM), lambda b, h, i: (1, h, b, 0)),
            pl.BlockSpec((None, None, S, HEAD_DIM), lambda b, h, i: (2, h, b, 0)),
            pl.BlockSpec((None, 1, S), lambda b, h, i: (b * N_HEADS + h, 0, 0)),
            pl.BlockSpec((tq, LANES), lambda b, h, i: (b * nq + i, 0)),
        ],
        out_specs=pl.BlockSpec((None, tq, HEAD_DIM), lambda b, h, i: (h, b * nq + i, 0)),
        out_shape=jax.ShapeDtypeStruct((N_HEADS, T, HEAD_DIM), BF16),
        compiler_params=pltpu.CompilerParams(
            dimension_semantics=("arbitrary", "arbitrary", "arbitrary"),
            vmem_limit_bytes=VMEM_LIMIT),
        name="fox_attn",
    )(qkv, qkv, qkv, ft, f)


def _merge_mlp_kernel(ob_ref, yag_ref, gb_ref, x_ref, wb_ref, wo_ref, gm_ref, wu_ref,
                      wdn_ref, gf_ref, o_ref):
    ob = jnp.concatenate([ob_ref[h] for h in range(N_HEADS)], axis=1)
    yb = _dot(ob, wb_ref[...])
    mix = yag_ref[...] + _sigmoid(gb_ref[...]) * yb
    x1 = x_ref[...] + _dot(mix.astype(BF16), wo_ref[...])
    m = _rms(x1, gm_ref[...]).astype(BF16)
    acc = x1
    for c in range(D_FF // D_MODEL):
        cols = slice(c * D_MODEL, (c + 1) * D_MODEL)
        hc = jnp.maximum(_dot(m, wu_ref[:, cols]), 0.0)
        acc = acc + _dot((hc * hc).astype(BF16), wdn_ref[cols, :])
    o_ref[...] = _rms(acc, gf_ref[...])


def _merge_mlp(ob, yag, pf, x2, wb, wo, gm, wu, wdn, gf, tm):
    T = x2.shape[0]
    return pl.pallas_call(
        _merge_mlp_kernel,
        grid=(T // tm,),
        in_specs=[
            pl.BlockSpec((N_HEADS, tm, HEAD_DIM), lambda i: (0, i, 0)),
            pl.BlockSpec((tm, D_MODEL), lambda i: (i, 0)),
            pl.BlockSpec((tm, D_MODEL), lambda i: (i, 3)),
            pl.BlockSpec((tm, D_MODEL), lambda i: (i, 0)),
            _const_spec((D_MODEL, D_MODEL)),
            _const_spec((D_MODEL, D_MODEL)),
            _const_spec((1, D_MODEL)),
            _const_spec((D_MODEL, D_FF)),
            _const_spec((D_FF, D_MODEL)),
            _const_spec((1, D_MODEL)),
        ],
        out_specs=pl.BlockSpec((tm, D_MODEL), lambda i: (i, 0)),
        out_shape=jax.ShapeDtypeStruct((T, D_MODEL), F32),
        compiler_params=pltpu.CompilerParams(
            dimension_semantics=("arbitrary",), vmem_limit_bytes=VMEM_LIMIT),
        name="merge_mlp",
    )(ob, yag, pf, x2, wb, wo, gm, wu, wdn, gf)


def _block_diag_pairs(wa, wx):
    def pair(w):
        w = w.reshape(LRU_BLOCKS // 2, 2, LRU_BW, LRU_BW)
        z = jnp.zeros_like(w[:, 0])
        top = jnp.concatenate([w[:, 0], z], axis=2)
        bot = jnp.concatenate([z, w[:, 1]], axis=2)
        return jnp.concatenate([top, bot], axis=1)
    return jnp.concatenate([pair(wa), pair(wx)], axis=2).astype(BF16)


def kernel(x, norm_mix_g, w_in, conv_w, conv_b, lru_wa, lru_ba, lru_wx, lru_bx, lru_lambda,
           forget_b, w_branch_a, w_branch_b, w_out, norm_mlp_g, w_up, w_down, norm_final_g):
    B, S, D = x.shape
    assert D == D_MODEL
    T = B * S
    tm = min(512, S)
    tq = min(512, S)
    assert S % tm == 0 and S % tq == 0 and tm % LANES == 0

    cuts = (0, D, 2 * D, 3 * D, 4 * D, 5 * D, 7 * D, 7 * D + N_HEADS)
    seg = [w_in[:, cuts[i]:cuts[i + 1]] for i in range(7)]
    w_main = jnp.concatenate([seg[0], seg[1], seg[5], seg[2], seg[3], seg[4]], axis=1).astype(BF16)
    w_f = jnp.pad(seg[6], ((0, 0), (0, LANES - N_HEADS))).astype(BF16)
    wd = _block_diag_pairs(lru_wa, lru_wx)
    row = lambda v: v.reshape(1, -1).astype(F32)
    fb = jnp.pad(forget_b, (0, LANES - N_HEADS)).reshape(1, LANES)

    x2 = x.reshape(T, D)
    pf, qkv, fl = _in_proj(x2, row(norm_mix_g), w_main, w_f, tm)
    yag, f, ft = _lru(pf, fl, conv_w, row(conv_b), wd, row(lru_ba), row(lru_bx),
                      row(lru_lambda), fb, w_branch_a.astype(BF16), B, S, tm)
    ob = _attn(qkv, ft.reshape(B * N_HEADS, 1, S), f, B, S, tq)
    out = _merge_mlp(ob, yag, pf, x2, w_branch_b.astype(BF16), w_out.astype(BF16),
                     row(norm_mlp_g), w_up.astype(BF16), w_down.astype(BF16),
                     row(norm_final_g), tm)
    return out.reshape(B, S, D)
```

```python
import functools
import math

import jax
import jax.numpy as jnp
from jax import lax
from jax.experimental import pallas as pl
from jax.experimental.pallas import tpu as pltpu

D_MODEL = 1024
N_HEADS = 8
HEAD_DIM = 128
LRU_BLOCKS = 16
LRU_BW = 64
CONV_W = 4
LRU_C = 8.0
D_FF = 4 * D_MODEL
RMS_EPS = 1e-6
LANES = 128
SUBLANES = 8
LOG2E = 1.4426950408889634
Q_SCALE = LOG2E / math.sqrt(HEAD_DIM)
VMEM_LIMIT = 56 * 1024 * 1024

BF16 = jnp.bfloat16
F32 = jnp.float32
_NT_DIMS = (((1,), (1,)), ((), ()))
F_PARTS = 3


def _dot(a, b):
    return jnp.dot(a, b, preferred_element_type=F32)


def _rms(x, g):
    return x * lax.rsqrt(jnp.mean(x * x, axis=-1, keepdims=True) + RMS_EPS) * g


def _softplus(x):
    return jnp.maximum(x, 0.0) + jnp.log1p(jnp.exp(-jnp.abs(x)))


def _sigmoid(x):
    return 0.5 * jnp.tanh(0.5 * x) + 0.5


def _gelu_tanh(x):
    c = math.sqrt(2.0 / math.pi)
    return 0.5 * x * (1.0 + jnp.tanh(c * (x + 0.044715 * (x * x * x))))


def _const_spec(shape):
    nd = len(shape)
    return pl.BlockSpec(shape, lambda *_: (0,) * nd, pipeline_mode=pl.Buffered(1))


def _in_proj_kernel(x_ref, g_ref, w_ref, wvt_ref, wf_ref, pf_ref, qk_ref, vt_ref, fl_ref):
    u = _rms(x_ref[...], g_ref[...]).astype(BF16)
    fl_ref[...] = _dot(u, wf_ref[...])
    for j in range(4):
        cols = slice(j * D_MODEL, (j + 1) * D_MODEL)
        pf_ref[:, cols] = _dot(u, w_ref[:, cols])
    for j in range(2):
        cols = slice((4 + j) * D_MODEL, (5 + j) * D_MODEL)
        res = _dot(u, w_ref[:, cols])
        if j == 0:
            res = res * Q_SCALE
        for h in range(N_HEADS):
            qk_ref[j, h] = res[:, h * HEAD_DIM:(h + 1) * HEAD_DIM].astype(BF16)
    vt = lax.dot_general(wvt_ref[...], u, _NT_DIMS, preferred_element_type=F32)
    for h in range(N_HEADS):
        vt_ref[h] = vt[h * HEAD_DIM:(h + 1) * HEAD_DIM, :].astype(BF16)


def _in_proj(x2, g, w_main, w_vt, w_f, tm):
    T = x2.shape[0]
    return pl.pallas_call(
        _in_proj_kernel,
        grid=(T // tm,),
        in_specs=[
            pl.BlockSpec((tm, D_MODEL), lambda i: (i, 0)),
            _const_spec((1, D_MODEL)),
            _const_spec((D_MODEL, 6 * D_MODEL)),
            _const_spec((D_MODEL, D_MODEL)),
            _const_spec((D_MODEL, LANES)),
        ],
        out_specs=[
            pl.BlockSpec((tm, 4 * D_MODEL), lambda i: (i, 0)),
            pl.BlockSpec((2, N_HEADS, tm, HEAD_DIM), lambda i: (0, 0, i, 0)),
            pl.BlockSpec((N_HEADS, HEAD_DIM, tm), lambda i: (0, 0, i)),
            pl.BlockSpec((tm, LANES), lambda i: (i, 0)),
        ],
        out_shape=[
            jax.ShapeDtypeStruct((T, 4 * D_MODEL), F32),
            jax.ShapeDtypeStruct((2, N_HEADS, T, HEAD_DIM), BF16),
            jax.ShapeDtypeStruct((N_HEADS, HEAD_DIM, T), BF16),
            jax.ShapeDtypeStruct((T, LANES), F32),
        ],
        compiler_params=pltpu.CompilerParams(
            dimension_semantics=("arbitrary",), vmem_limit_bytes=VMEM_LIMIT),
        name="in_proj",
    )(x2, g, w_main, w_vt, w_f)


def _lru_kernel(xl_ref, gl_ref, ga_ref, fl_ref, cw_ref, cb_ref, wd_ref, ba_ref, bx_ref,
                lam_ref, fb_ref, wa_ref, e_ref,
                yag_ref, kf_ref, ft_ref,
                xp_sc, a_sc, b_sc, hc_sc, fc_sc, *, tm):
    s = pl.program_id(1)

    @pl.when(s == 0)
    def _():
        xp_sc[0:SUBLANES, :] = jnp.zeros((SUBLANES, D_MODEL), F32)
        hc_sc[...] = jnp.zeros_like(hc_sc)
        fc_sc[...] = jnp.zeros_like(fc_sc)

    xp_sc[SUBLANES:SUBLANES + tm, :] = xl_ref[...]
    xa = cb_ref[...] + cw_ref[0:1, :] * xp_sc[pl.ds(SUBLANES - 3, tm), :]
    for k in range(1, CONV_W):
        xa = xa + cw_ref[k:k + 1, :] * xp_sc[pl.ds(SUBLANES - 3 + k, tm), :]
    xp_sc[0:SUBLANES, :] = xp_sc[tm:tm + SUBLANES, :]

    xab = xa.astype(BF16)
    c_all = -LRU_C * _softplus(-lam_ref[...])
    for j in range(D_MODEL // LANES):
        cols = slice(j * LANES, (j + 1) * LANES)
        ri = _dot(xab[:, cols], wd_ref[j])
        r = _sigmoid(ri[:, :LANES] + ba_ref[:, cols])
        i = _sigmoid(ri[:, LANES:] + bx_ref[:, cols])
        log_a = c_all[:, cols] * r
        a = jnp.exp(log_a)
        a_sc[:, cols] = a
        b_sc[:, cols] = jnp.sqrt(1.0 - a * a) * (i * xa[:, cols])

    row = lax.broadcasted_iota(jnp.int32, (SUBLANES, D_MODEL), 0)

    def group(g, carry):
        r0 = pl.multiple_of(g * SUBLANES, SUBLANES)
        a = a_sc[pl.ds(r0, SUBLANES), :]
        b = b_sc[pl.ds(r0, SUBLANES), :]
        for d in (1, 2, 4):
            keep = row >= d
            a_sh = jnp.where(keep, pltpu.roll(a, d, axis=0), 1.0)
            b_sh = jnp.where(keep, pltpu.roll(b, d, axis=0), 0.0)
            b = a * b_sh + b
            a = a * a_sh
        h = b + a * carry
        b_sc[pl.ds(r0, SUBLANES), :] = h
        return jnp.broadcast_to(h[SUBLANES - 1:SUBLANES, :], (SUBLANES, D_MODEL))

    hc_sc[...] = lax.fori_loop(0, tm // SUBLANES, group, hc_sc[...])

    y = (_gelu_tanh(gl_ref[...]) * b_sc[...]).astype(BF16)
    yag_ref[...] = _sigmoid(ga_ref[...]) * _dot(y, wa_ref[...])

    lf = -_softplus(-(fl_ref[...] + fb_ref[...])) * LOG2E
    rowf = lax.broadcasted_iota(jnp.int32, (tm, LANES), 0)
    d = 1
    while d < tm:
        lf = lf + jnp.where(rowf >= d, pltpu.roll(lf, d, axis=0), 0.0)
        d *= 2
    fblk = lf + fc_sc[...]
    fc_sc[...] = fblk[tm - 1:tm, :]
    ft_ref[...] = fblk.T[0:N_HEADS, :]
    hi = (-fblk).astype(BF16)
    r1 = -fblk - hi.astype(F32)
    mid = r1.astype(BF16)
    lo = (r1 - mid.astype(F32)).astype(BF16)
    kf_ref[...] = _dot(jnp.concatenate([hi, mid, lo], axis=1), e_ref[...]).astype(BF16)


def _lru(pf, fl, cw, cb, wd, ba, bx, lam, fb, wa, e, B, S, tm):
    T = B * S
    ns = S // tm
    row_blk = lambda c: pl.BlockSpec((tm, D_MODEL), lambda b, s, c=c: (b * ns + s, c))
    return pl.pallas_call(
        functools.partial(_lru_kernel, tm=tm),
        grid=(B, ns),
        in_specs=[
            row_blk(0), row_blk(1), row_blk(2),
            pl.BlockSpec((tm, LANES), lambda b, s: (b * ns + s, 0)),
            _const_spec((CONV_W, D_MODEL)),
            _const_spec((1, D_MODEL)),
            _const_spec((D_MODEL // LANES, LANES, 2 * LANES)),
            _const_spec((1, D_MODEL)),
            _const_spec((1, D_MODEL)),
            _const_spec((1, D_MODEL)),
            _const_spec((1, LANES)),
            _const_spec((D_MODEL, D_MODEL)),
            _const_spec((F_PARTS * LANES, LANES)),
        ],
        out_specs=[
            pl.BlockSpec((tm, D_MODEL), lambda b, s: (b * ns + s, 0)),
            pl.BlockSpec((tm, LANES), lambda b, s: (b * ns + s, 0)),
            pl.BlockSpec((None, N_HEADS, tm), lambda b, s: (b, 0, s)),
        ],
        out_shape=[
            jax.ShapeDtypeStruct((T, D_MODEL), F32),
            jax.ShapeDtypeStruct((T, LANES), BF16),
            jax.ShapeDtypeStruct((B, N_HEADS, S), F32),
        ],
        scratch_shapes=[
            pltpu.VMEM((tm + SUBLANES, D_MODEL), F32),
            pltpu.VMEM((tm, D_MODEL), F32),
            pltpu.VMEM((tm, D_MODEL), F32),
            pltpu.VMEM((SUBLANES, D_MODEL), F32),
            pltpu.VMEM((1, LANES), F32),
        ],
        compiler_params=pltpu.CompilerParams(
            dimension_semantics=("arbitrary", "arbitrary"), vmem_limit_bytes=VMEM_LIMIT),
        name="lru",
    )(pf, pf, pf, fl, cw, cb, wd, ba, bx, lam, fb, wa, e)


def _col_reduce(op, z):
    rows, cols = z.shape
    return op(op(z.reshape(SUBLANES, rows // SUBLANES, cols), axis=0), axis=0, keepdims=True)


def _attn_kernel(q_ref, k_ref, kf_ref, vt_ref, ft_ref, o_ref,
                 z_sc, p_sc, acc_sc, m_sc, l_sc, zmax_sc, *, tq, hp):
    hg = pl.program_id(1)
    qi = pl.program_id(2)
    lane = lax.broadcasted_iota(jnp.int32, (tq, LANES), 1)
    qa, f_t = [], []
    for a in range(hp):
        h = hg * hp + a
        ones = jnp.where((lane >= F_PARTS * h) & (lane < F_PARTS * (h + 1)), 1.0, 0.0).astype(BF16)
        qa.append(jnp.concatenate([q_ref[a], ones], axis=1))
        f_t.append(ft_ref[pl.ds(h, 1), :])

    def qk(a, j):
        c0 = pl.multiple_of(j * tq, tq)
        ka = jnp.concatenate([k_ref[a, pl.ds(c0, tq), :], kf_ref[pl.ds(c0, tq), :]], axis=1)
        return lax.dot_general(ka, qa[a], _NT_DIMS, preferred_element_type=F32)

    def pv(a, j):
        c0 = pl.multiple_of(j * tq, tq)
        return _dot(vt_ref[a, :, pl.ds(c0, tq)], p_sc[a])

    def softmax(a, slot, m_row, zmax, masked):
        z = z_sc[slot, a]
        if masked:
            kr = lax.broadcasted_iota(jnp.int32, (tq, tq), 0)
            qc = lax.broadcasted_iota(jnp.int32, (tq, tq), 1)
            z = jnp.where(kr <= qc, z, -jnp.inf)
            zmax = _col_reduce(jnp.max, z)
        m_new = jnp.maximum(m_row, zmax + f_t[a])
        p = jnp.exp2(z - (m_new - f_t[a]))
        p_sc[a] = p.astype(BF16)
        return m_new, jnp.exp2(m_row - m_new), _col_reduce(jnp.sum, p)

    def trip(j, last, slot):
        zmax_next = []
        if not last:
            for a in range(hp):
                zn = qk(a, j + 1)
                z_sc[1 - slot, a] = zn
                zmax_next.append(_col_reduce(jnp.max, zn))
        pvs = [pv(a, jnp.maximum(j - 1, 0)) for a in range(hp)]
        for a in range(hp):
            m_new, alpha, psum = softmax(a, slot, m_sc[a], zmax_sc[a], last)
            m_sc[a] = m_new
            l_sc[a] = alpha * l_sc[a] + psum
            acc_sc[a] = alpha * (acc_sc[a] + pvs[a])
            if not last:
                zmax_sc[a] = zmax_next[a]

    def by_parity(j, last):
        for slot in range(2):
            pl.when(j % 2 == slot)(functools.partial(trip, j, last, slot))

    for a in range(hp):
        z0 = qk(a, 0)
        z_sc[0, a] = z0
        zmax_sc[a] = _col_reduce(jnp.max, z0)
        p_sc[a] = jnp.zeros((tq, tq), BF16)
        m_sc[a] = jnp.full((1, tq), -jnp.inf, F32)
        l_sc[a] = jnp.zeros((1, tq), F32)
        acc_sc[a] = jnp.zeros((HEAD_DIM, tq), F32)

    @pl.loop(0, qi)
    def _(j):
        by_parity(j, False)

    by_parity(qi, True)
    for a in range(hp):
        o_ref[a] = ((acc_sc[a] + pv(a, qi)) / l_sc[a]).T.astype(BF16)


def _attn(qk, kf, vt, ft, B, S, tq, hp):
    T = B * S
    nq = S // tq
    return pl.pallas_call(
        functools.partial(_attn_kernel, tq=tq, hp=hp),
        grid=(B, N_HEADS // hp, nq),
        in_specs=[
            pl.BlockSpec((None, hp, tq, HEAD_DIM), lambda b, g, i: (0, g, b * nq + i, 0)),
            pl.BlockSpec((None, hp, S, HEAD_DIM), lambda b, g, i: (1, g, b, 0)),
            pl.BlockSpec((S, LANES), lambda b, g, i: (b, 0)),
            pl.BlockSpec((hp, HEAD_DIM, S), lambda b, g, i: (g, 0, b)),
            pl.BlockSpec((None, N_HEADS, tq), lambda b, g, i: (b, 0, i)),
        ],
        out_specs=pl.BlockSpec((hp, tq, HEAD_DIM), lambda b, g, i: (g, b * nq + i, 0)),
        out_shape=jax.ShapeDtypeStruct((N_HEADS, T, HEAD_DIM), BF16),
        scratch_shapes=[
            pltpu.VMEM((2, hp, tq, tq), F32),
            pltpu.VMEM((hp, tq, tq), BF16),
            pltpu.VMEM((hp, HEAD_DIM, tq), F32),
            pltpu.VMEM((hp, 1, tq), F32),
            pltpu.VMEM((hp, 1, tq), F32),
            pltpu.VMEM((hp, 1, tq), F32),
        ],
        compiler_params=pltpu.CompilerParams(
            dimension_semantics=("arbitrary", "arbitrary", "arbitrary"),
            vmem_limit_bytes=VMEM_LIMIT),
        name="fox_attn",
    )(qk, qk, kf, vt, ft)


def _merge_mlp_kernel(ob_ref, yag_ref, gb_ref, x_ref, wb_ref, wo_ref, gm_ref, wu_ref,
                      wdn_ref, gf_ref, o_ref):
    ob = jnp.concatenate([ob_ref[h] for h in range(N_HEADS)], axis=1)
    yb = _dot(ob, wb_ref[...])
    mix = yag_ref[...] + _sigmoid(gb_ref[...]) * yb
    x1 = x_ref[...] + _dot(mix.astype(BF16), wo_ref[...])
    m = _rms(x1, gm_ref[...]).astype(BF16)
    acc = x1
    for c in range(D_FF // D_MODEL):
        cols = slice(c * D_MODEL, (c + 1) * D_MODEL)
        hc = jnp.maximum(_dot(m, wu_ref[:, cols]), 0.0)
        acc = acc + _dot((hc * hc).astype(BF16), wdn_ref[cols, :])
    o_ref[...] = _rms(acc, gf_ref[...])


def _merge_mlp(ob, yag, pf, x2, wb, wo, gm, wu, wdn, gf, tm):
    T = x2.shape[0]
    return pl.pallas_call(
        _merge_mlp_kernel,
        grid=(T // tm,),
        in_specs=[
            pl.BlockSpec((N_HEADS, tm, HEAD_DIM), lambda i: (0, i, 0)),
            pl.BlockSpec((tm, D_MODEL), lambda i: (i, 0)),
            pl.BlockSpec((tm, D_MODEL), lambda i: (i, 3)),
            pl.BlockSpec((tm, D_MODEL), lambda i: (i, 0)),
            _const_spec((D_MODEL, D_MODEL)),
            _const_spec((D_MODEL, D_MODEL)),
            _const_spec((1, D_MODEL)),
            _const_spec((D_MODEL, D_FF)),
            _const_spec((D_FF, D_MODEL)),
            _const_spec((1, D_MODEL)),
        ],
        out_specs=pl.BlockSpec((tm, D_MODEL), lambda i: (i, 0)),
        out_shape=jax.ShapeDtypeStruct((T, D_MODEL), F32),
        compiler_params=pltpu.CompilerParams(
            dimension_semantics=("arbitrary",), vmem_limit_bytes=VMEM_LIMIT),
        name="merge_mlp",
    )(ob, yag, pf, x2, wb, wo, gm, wu, wdn, gf)


def _block_diag_pairs(wa, wx):
    def pair(w):
        w = w.reshape(LRU_BLOCKS // 2, 2, LRU_BW, LRU_BW)
        z = jnp.zeros_like(w[:, 0])
        top = jnp.concatenate([w[:, 0], z], axis=2)
        bot = jnp.concatenate([z, w[:, 1]], axis=2)
        return jnp.concatenate([top, bot], axis=1)
    return jnp.concatenate([pair(wa), pair(wx)], axis=2).astype(BF16)


def kernel(x, norm_mix_g, w_in, conv_w, conv_b, lru_wa, lru_ba, lru_wx, lru_bx, lru_lambda,
           forget_b, w_branch_a, w_branch_b, w_out, norm_mlp_g, w_up, w_down, norm_final_g):
    B, S, D = x.shape
    assert D == D_MODEL
    T = B * S
    tm = min(512, S)
    tq = min(512, S)
    assert S % tm == 0 and S % tq == 0 and tm % LANES == 0

    cuts = (0, D, 2 * D, 3 * D, 4 * D, 5 * D, 7 * D, 7 * D + N_HEADS)
    seg = [w_in[:, cuts[i]:cuts[i + 1]] for i in range(7)]
    w_main = jnp.concatenate([seg[0], seg[1], seg[5], seg[2], seg[3]], axis=1).astype(BF16)
    w_vt = seg[4].T.astype(BF16)
    w_f = jnp.pad(seg[6], ((0, 0), (0, LANES - N_HEADS))).astype(BF16)
    wd = _block_diag_pairs(lru_wa, lru_wx)
    row = lambda v: v.reshape(1, -1).astype(F32)
    fb = jnp.pad(forget_b, (0, LANES - N_HEADS)).reshape(1, LANES)
    e_rows = jnp.arange(F_PARTS * LANES)
    e_tgt = jnp.where(e_rows % LANES < N_HEADS, (e_rows % LANES) * F_PARTS + e_rows // LANES, -1)
    e = (e_tgt[:, None] == jnp.arange(LANES)[None, :]).astype(BF16)

    x2 = x.reshape(T, D)
    pf, qk, vt, fl = _in_proj(x2, row(norm_mix_g), w_main, w_vt, w_f, tm)
    yag, kf, ft = _lru(pf, fl, conv_w, row(conv_b), wd, row(lru_ba), row(lru_bx),
                       row(lru_lambda), fb, w_branch_a.astype(BF16), e, B, S, tm)
    ob = _attn(qk, kf, vt, ft, B, S, tq, hp=2)
    out = _merge_mlp(ob, yag, pf, x2, w_branch_b.astype(BF16), w_out.astype(BF16),
                     row(norm_mlp_g), w_up.astype(BF16), w_down.astype(BF16),
                     row(norm_final_g), tm)
    return out.reshape(B, S, D)
```

```python
import functools
import math

import jax
import jax.numpy as jnp
from jax import lax
from jax.experimental import pallas as pl
from jax.experimental.pallas import tpu as pltpu

D_MODEL = 1024
N_HEADS = 8
HEAD_DIM = 128
LRU_BLOCKS = 16
LRU_BW = 64
CONV_W = 4
LRU_C = 8.0
D_FF = 4 * D_MODEL
RMS_EPS = 1e-6
LANES = 128
SUBLANES = 8
LOG2E = 1.4426950408889634
Q_SCALE = LOG2E / math.sqrt(HEAD_DIM)
VMEM_LIMIT = 56 * 1024 * 1024

BF16 = jnp.bfloat16
F32 = jnp.float32
_NT_DIMS = (((1,), (1,)), ((), ()))
F_PARTS = 3


def _dot(a, b):
    return jnp.dot(a, b, preferred_element_type=F32)


def _rms(x, g):
    return x * lax.rsqrt(jnp.mean(x * x, axis=-1, keepdims=True) + RMS_EPS) * g


def _softplus(x):
    return jnp.maximum(x, 0.0) + jnp.log1p(jnp.exp(-jnp.abs(x)))


def _sigmoid(x):
    return 0.5 * jnp.tanh(0.5 * x) + 0.5


def _gelu_tanh(x):
    c = math.sqrt(2.0 / math.pi)
    half_x = 0.5 * x
    return half_x + half_x * jnp.tanh(x * (c + (c * 0.044715) * (x * x)))


def _const_spec(shape):
    nd = len(shape)
    return pl.BlockSpec(shape, lambda *_: (0,) * nd, pipeline_mode=pl.Buffered(1))


def _in_proj_kernel(x_ref, g_ref, w_ref, wvt_ref, wf_ref, pf_ref, qk_ref, vt_ref, fl_ref):
    u = _rms(x_ref[...], g_ref[...]).astype(BF16)
    fl_ref[...] = _dot(u, wf_ref[...])
    for j in range(4):
        cols = slice(j * D_MODEL, (j + 1) * D_MODEL)
        pf_ref[:, cols] = _dot(u, w_ref[:, cols])
    for j in range(2):
        cols = slice((4 + j) * D_MODEL, (5 + j) * D_MODEL)
        res = _dot(u, w_ref[:, cols])
        if j == 0:
            res = res * Q_SCALE
        for h in range(N_HEADS):
            qk_ref[j, h] = res[:, h * HEAD_DIM:(h + 1) * HEAD_DIM].astype(BF16)
    vt = lax.dot_general(wvt_ref[...], u, _NT_DIMS, preferred_element_type=F32)
    for h in range(N_HEADS):
        vt_ref[h] = vt[h * HEAD_DIM:(h + 1) * HEAD_DIM, :].astype(BF16)


def _in_proj(x2, g, w_main, w_vt, w_f, tm):
    T = x2.shape[0]
    return pl.pallas_call(
        _in_proj_kernel,
        grid=(T // tm,),
        in_specs=[
            pl.BlockSpec((tm, D_MODEL), lambda i: (i, 0)),
            _const_spec((1, D_MODEL)),
            _const_spec((D_MODEL, 6 * D_MODEL)),
            _const_spec((D_MODEL, D_MODEL)),
            _const_spec((D_MODEL, LANES)),
        ],
        out_specs=[
            pl.BlockSpec((tm, 4 * D_MODEL), lambda i: (i, 0)),
            pl.BlockSpec((2, N_HEADS, tm, HEAD_DIM), lambda i: (0, 0, i, 0)),
            pl.BlockSpec((N_HEADS, HEAD_DIM, tm), lambda i: (0, 0, i)),
            pl.BlockSpec((tm, LANES), lambda i: (i, 0)),
        ],
        out_shape=[
            jax.ShapeDtypeStruct((T, 4 * D_MODEL), F32),
            jax.ShapeDtypeStruct((2, N_HEADS, T, HEAD_DIM), BF16),
            jax.ShapeDtypeStruct((N_HEADS, HEAD_DIM, T), BF16),
            jax.ShapeDtypeStruct((T, LANES), F32),
        ],
        compiler_params=pltpu.CompilerParams(
            dimension_semantics=("arbitrary",), vmem_limit_bytes=VMEM_LIMIT),
        name="in_proj",
    )(x2, g, w_main, w_vt, w_f)


def _lru_kernel(xl_ref, gl_ref, ga_ref, fl_ref, cw_ref, cb_ref, wd_ref, ba_ref, bx_ref,
                lam_ref, fb_ref, wa_ref, e_ref,
                yag_ref, kf_ref, ft_ref,
                xq_sc, a_sc, b_sc, h_sc, halo_sc, hc_sc, fc_sc, *, tm):
    s = pl.program_id(1)
    seg = tm // SUBLANES
    pitch = seg + SUBLANES
    gps = seg // SUBLANES
    nc = D_MODEL // LANES
    n_halo = CONV_W - 1

    @pl.when(s == 0)
    def _():
        halo_sc[...] = jnp.zeros_like(halo_sc)
        hc_sc[...] = jnp.zeros_like(hc_sc)
        fc_sc[...] = jnp.zeros_like(fc_sc)

    for g in range(tm // SUBLANES):
        u, j0 = g // gps, (g % gps) * SUBLANES
        for c in range(nc):
            xq_sc[c, pl.ds(j0 * SUBLANES + u, SUBLANES, stride=SUBLANES), :] = (
                xl_ref[g * SUBLANES:(g + 1) * SUBLANES, c * LANES:(c + 1) * LANES])

    sub = lax.broadcasted_iota(jnp.int32, (SUBLANES, LANES), 0)
    c_all = -LRU_C * _softplus(-lam_ref[...])
    for c in range(nc):
        cols = slice(c * LANES, (c + 1) * LANES)
        x = xq_sc[c]
        wrap = []
        for q in range(n_halo):
            rows = slice(q * SUBLANES, (q + 1) * SUBLANES)
            rolled = pltpu.roll(x[tm - n_halo * SUBLANES:, :][rows], 1, axis=0)
            wrap.append(jnp.where(sub == 0, halo_sc[c, rows, :], rolled))
            halo_sc[c, rows, :] = rolled
        xa = cb_ref[:, cols] + cw_ref[CONV_W - 1:CONV_W, cols] * x
        for d in range(1, CONV_W):
            xd = jnp.concatenate(wrap[n_halo - d:] + [x[:tm - d * SUBLANES, :]], axis=0)
            xa = xa + cw_ref[CONV_W - 1 - d:CONV_W - d, cols] * xd
        ri = _dot(xa.astype(BF16), wd_ref[c])
        t_r = jnp.tanh(0.5 * ri[:, :LANES] + 0.5 * ba_ref[:, cols])
        t_i = jnp.tanh(0.5 * ri[:, LANES:] + 0.5 * bx_ref[:, cols])
        half_c = 0.5 * c_all[:, cols]
        a = jnp.exp(half_c * t_r + half_c)
        half_xa = 0.5 * xa
        one_m_a2 = 1.0 - a * a
        mult = one_m_a2 * lax.rsqrt(jnp.maximum(one_m_a2, 1e-30))
        a_sc[c] = a
        b_sc[c] = mult * (half_xa * t_i + half_xa)

    def scan_step(j, carry):
        r0 = pl.multiple_of(j * SUBLANES, SUBLANES)
        out = []
        for c in range(nc):
            h, p = carry[c]
            a = a_sc[c, pl.ds(r0, SUBLANES), :]
            h = a * h + b_sc[c, pl.ds(r0, SUBLANES), :]
            p = a * p
            b_sc[c, pl.ds(r0, SUBLANES), :] = h
            a_sc[c, pl.ds(r0, SUBLANES), :] = p
            out.append((h, p))
        return tuple(out)

    ends = lax.fori_loop(
        0, seg, scan_step,
        tuple((jnp.zeros((SUBLANES, LANES), F32), jnp.ones((SUBLANES, LANES), F32))
              for _ in range(nc)), unroll=SUBLANES)

    carry_in = []
    for c in range(nc):
        cols = slice(c * LANES, (c + 1) * LANES)
        h_end, p_end = ends[c]
        for d in (1, 2, 4):
            keep = sub >= d
            p_sh = jnp.where(keep, pltpu.roll(p_end, d, axis=0), 1.0)
            h_sh = jnp.where(keep, pltpu.roll(h_end, d, axis=0), 0.0)
            h_end = p_end * h_sh + h_end
            p_end = p_end * p_sh
        blk_in = hc_sc[:, cols]
        seg_out = h_end + p_end * blk_in
        carry_in.append(jnp.where(sub == 0, blk_in, pltpu.roll(seg_out, 1, axis=0)))
        hc_sc[:, cols] = jnp.broadcast_to(seg_out[SUBLANES - 1:SUBLANES, :], (SUBLANES, LANES))

    def fix_step(j, _):
        r0 = pl.multiple_of(j * SUBLANES, SUBLANES)
        for c in range(nc):
            h = b_sc[c, pl.ds(r0, SUBLANES), :] + a_sc[c, pl.ds(r0, SUBLANES), :] * carry_in[c]
            h_sc[c, pl.ds(j, SUBLANES, stride=pitch), :] = h
        return 0

    lax.fori_loop(0, seg, fix_step, 0, unroll=SUBLANES)

    h = jnp.concatenate(
        [jnp.concatenate([h_sc[c, u * pitch:u * pitch + seg, :] for u in range(SUBLANES)], axis=0)
         for c in range(nc)], axis=1)
    y = (_gelu_tanh(gl_ref[...]) * h).astype(BF16)
    yag_ref[...] = _sigmoid(ga_ref[...]) * _dot(y, wa_ref[...])

    lf = -_softplus(-(fl_ref[...] + fb_ref[...])) * LOG2E
    rowf = lax.broadcasted_iota(jnp.int32, (tm, LANES), 0)
    d = 1
    while d < tm:
        lf = lf + jnp.where(rowf >= d, pltpu.roll(lf, d, axis=0), 0.0)
        d *= 2
    fblk = lf + fc_sc[...]
    fc_sc[...] = fblk[tm - 1:tm, :]
    ft_ref[...] = fblk.T[0:N_HEADS, :]
    hi = (-fblk).astype(BF16)
    r1 = -fblk - hi.astype(F32)
    mid = r1.astype(BF16)
    lo = (r1 - mid.astype(F32)).astype(BF16)
    kf_ref[...] = _dot(jnp.concatenate([hi, mid, lo], axis=1), e_ref[...]).astype(BF16)


def _lru(pf, fl, cw, cb, wd, ba, bx, lam, fb, wa, e, B, S, tm):
    T = B * S
    ns = S // tm
    row_blk = lambda c: pl.BlockSpec((tm, D_MODEL), lambda b, s, c=c: (b * ns + s, c))
    return pl.pallas_call(
        functools.partial(_lru_kernel, tm=tm),
        grid=(B, ns),
        in_specs=[
            row_blk(0), row_blk(1), row_blk(2),
            pl.BlockSpec((tm, LANES), lambda b, s: (b * ns + s, 0)),
            _const_spec((CONV_W, D_MODEL)),
            _const_spec((1, D_MODEL)),
            _const_spec((D_MODEL // LANES, LANES, 2 * LANES)),
            _const_spec((1, D_MODEL)),
            _const_spec((1, D_MODEL)),
            _const_spec((1, D_MODEL)),
            _const_spec((1, LANES)),
            _const_spec((D_MODEL, D_MODEL)),
            _const_spec((F_PARTS * LANES, LANES)),
        ],
        out_specs=[
            pl.BlockSpec((tm, D_MODEL), lambda b, s: (b * ns + s, 0)),
            pl.BlockSpec((tm, LANES), lambda b, s: (b * ns + s, 0)),
            pl.BlockSpec((None, N_HEADS, tm), lambda b, s: (b, 0, s)),
        ],
        out_shape=[
            jax.ShapeDtypeStruct((T, D_MODEL), F32),
            jax.ShapeDtypeStruct((T, LANES), BF16),
            jax.ShapeDtypeStruct((B, N_HEADS, S), F32),
        ],
        scratch_shapes=[
            pltpu.VMEM((D_MODEL // LANES, tm, LANES), F32),
            pltpu.VMEM((D_MODEL // LANES, tm, LANES), F32),
            pltpu.VMEM((D_MODEL // LANES, tm, LANES), F32),
            pltpu.VMEM((D_MODEL // LANES, tm + SUBLANES * SUBLANES, LANES), F32),
            pltpu.VMEM((D_MODEL // LANES, (CONV_W - 1) * SUBLANES, LANES), F32),
            pltpu.VMEM((SUBLANES, D_MODEL), F32),
            pltpu.VMEM((1, LANES), F32),
        ],
        compiler_params=pltpu.CompilerParams(
            dimension_semantics=("arbitrary", "arbitrary"), vmem_limit_bytes=VMEM_LIMIT),
        name="lru",
    )(pf, pf, pf, fl, cw, cb, wd, ba, bx, lam, fb, wa, e)


def _col_reduce(op, z):
    rows, cols = z.shape
    return op(op(z.reshape(SUBLANES, rows // SUBLANES, cols), axis=0), axis=0, keepdims=True)


def _attn_kernel(q_ref, k_ref, kf_ref, vt_ref, ft_ref, o_ref,
                 z_sc, p_sc, acc_sc, m_sc, l_sc, zmax_sc, *, tq, tk, hp):
    hg = pl.program_id(1)
    qi = pl.program_id(2)
    n_part = tq // tk
    lane = lax.broadcasted_iota(jnp.int32, (tq, LANES), 1)
    qa, f_t = [], []
    for a in range(hp):
        h = hg * hp + a
        ones = jnp.where((lane >= F_PARTS * h) & (lane < F_PARTS * (h + 1)), 1.0, 0.0).astype(BF16)
        qa.append(jnp.concatenate([q_ref[a], ones], axis=1))
        f_t.append(ft_ref[pl.ds(h, 1), :])

    def qk(a, j):
        c0 = pl.multiple_of(j * tk, tk)
        ka = jnp.concatenate([k_ref[a, pl.ds(c0, tk), :], kf_ref[pl.ds(c0, tk), :]], axis=1)
        return lax.dot_general(ka, qa[a], _NT_DIMS, preferred_element_type=F32)

    def pv(a, j):
        c0 = pl.multiple_of(j * tk, tk)
        return _dot(vt_ref[a, :, pl.ds(c0, tk)], p_sc[a])

    def softmax(a, slot, m_row, zmax, part):
        z = z_sc[slot, a]
        if part is not None:
            kr = lax.broadcasted_iota(jnp.int32, (tk, tq), 0)
            qc = lax.broadcasted_iota(jnp.int32, (tk, tq), 1)
            z = jnp.where(kr + part * tk <= qc, z, -jnp.inf)
            zmax = _col_reduce(jnp.max, z)
        m_new = jnp.maximum(m_row, zmax + f_t[a])
        p = jnp.exp2(z - (m_new - f_t[a]))
        p_sc[a] = p.astype(BF16)
        return m_new, jnp.exp2(m_row - m_new), _col_reduce(jnp.sum, p)

    def trip(j, slot, part=None, prefetch=True):
        zmax_next = []
        if prefetch:
            for a in range(hp):
                zn = qk(a, j + 1)
                z_sc[1 - slot, a] = zn
                zmax_next.append(_col_reduce(jnp.max, zn))
        pvs = [pv(a, jnp.maximum(j - 1, 0)) for a in range(hp)]
        for a in range(hp):
            m_new, alpha, psum = softmax(a, slot, m_sc[a], zmax_sc[a], part)
            m_sc[a] = m_new
            l_sc[a] = alpha * l_sc[a] + psum
            acc_sc[a] = alpha * (acc_sc[a] + pvs[a])
            if prefetch:
                zmax_sc[a] = zmax_next[a]

    for a in range(hp):
        z0 = qk(a, 0)
        z_sc[0, a] = z0
        zmax_sc[a] = _col_reduce(jnp.max, z0)
        p_sc[a] = jnp.zeros((tk, tq), BF16)
        m_sc[a] = jnp.full((1, tq), -jnp.inf, F32)
        l_sc[a] = jnp.zeros((1, tq), F32)
        acc_sc[a] = jnp.zeros((HEAD_DIM, tq), F32)

    @pl.loop(0, qi)
    def _(i):
        trip(n_part * i, 0)
        trip(n_part * i + 1, 1)

    first = n_part * qi
    for part in range(n_part):
        trip(first + part, part % 2, part=part, prefetch=part + 1 < n_part)
    for a in range(hp):
        o_ref[a] = ((acc_sc[a] + pv(a, first + n_part - 1)) / l_sc[a]).T.astype(BF16)


def _attn(qk, kf, vt, ft, B, S, tq, tk, hp):
    T = B * S
    nq = S // tq
    assert tq == 2 * tk
    return pl.pallas_call(
        functools.partial(_attn_kernel, tq=tq, tk=tk, hp=hp),
        grid=(B, N_HEADS // hp, nq),
        in_specs=[
            pl.BlockSpec((None, hp, tq, HEAD_DIM), lambda b, g, i: (0, g, b * nq + i, 0)),
            pl.BlockSpec((None, hp, S, HEAD_DIM), lambda b, g, i: (1, g, b, 0)),
            pl.BlockSpec((S, LANES), lambda b, g, i: (b, 0)),
            pl.BlockSpec((hp, HEAD_DIM, S), lambda b, g, i: (g, 0, b)),
            pl.BlockSpec((None, N_HEADS, tq), lambda b, g, i: (b, 0, i)),
        ],
        out_specs=pl.BlockSpec((hp, tq, HEAD_DIM), lambda b, g, i: (g, b * nq + i, 0)),
        out_shape=jax.ShapeDtypeStruct((N_HEADS, T, HEAD_DIM), BF16),
        scratch_shapes=[
            pltpu.VMEM((2, hp, tk, tq), F32),
            pltpu.VMEM((hp, tk, tq), BF16),
            pltpu.VMEM((hp, HEAD_DIM, tq), F32),
            pltpu.VMEM((hp, 1, tq), F32),
            pltpu.VMEM((hp, 1, tq), F32),
            pltpu.VMEM((hp, 1, tq), F32),
        ],
        compiler_params=pltpu.CompilerParams(
            dimension_semantics=("arbitrary", "arbitrary", "arbitrary"),
            vmem_limit_bytes=VMEM_LIMIT),
        name="fox_attn",
    )(qk, qk, kf, vt, ft)


def _merge_mlp_kernel(ob_ref, yag_ref, gb_ref, x_ref, wb_ref, wo_ref, gm_ref, wu_ref,
                      wdn_ref, gf_ref, o_ref):
    ob = jnp.concatenate([ob_ref[h] for h in range(N_HEADS)], axis=1)
    yb = _dot(ob, wb_ref[...])
    mix = yag_ref[...] + _sigmoid(gb_ref[...]) * yb
    x1 = x_ref[...] + _dot(mix.astype(BF16), wo_ref[...])
    m = _rms(x1, gm_ref[...]).astype(BF16)
    acc = x1
    for c in range(D_FF // D_MODEL):
        cols = slice(c * D_MODEL, (c + 1) * D_MODEL)
        hc = jnp.maximum(_dot(m, wu_ref[:, cols]), 0.0)
        acc = acc + _dot((hc * hc).astype(BF16), wdn_ref[cols, :])
    o_ref[...] = _rms(acc, gf_ref[...])


def _merge_mlp(ob, yag, pf, x2, wb, wo, gm, wu, wdn, gf, tm):
    T = x2.shape[0]
    return pl.pallas_call(
        _merge_mlp_kernel,
        grid=(T // tm,),
        in_specs=[
            pl.BlockSpec((N_HEADS, tm, HEAD_DIM), lambda i: (0, i, 0)),
            pl.BlockSpec((tm, D_MODEL), lambda i: (i, 0)),
            pl.BlockSpec((tm, D_MODEL), lambda i: (i, 3)),
            pl.BlockSpec((tm, D_MODEL), lambda i: (i, 0)),
            _const_spec((D_MODEL, D_MODEL)),
            _const_spec((D_MODEL, D_MODEL)),
            _const_spec((1, D_MODEL)),
            _const_spec((D_MODEL, D_FF)),
            _const_spec((D_FF, D_MODEL)),
            _const_spec((1, D_MODEL)),
        ],
        out_specs=pl.BlockSpec((tm, D_MODEL), lambda i: (i, 0)),
        out_shape=jax.ShapeDtypeStruct((T, D_MODEL), F32),
        compiler_params=pltpu.CompilerParams(
            dimension_semantics=("arbitrary",), vmem_limit_bytes=VMEM_LIMIT),
        name="merge_mlp",
    )(ob, yag, pf, x2, wb, wo, gm, wu, wdn, gf)


def _block_diag_pairs(wa, wx):
    def pair(w):
        w = w.reshape(LRU_BLOCKS // 2, 2, LRU_BW, LRU_BW)
        z = jnp.zeros_like(w[:, 0])
        top = jnp.concatenate([w[:, 0], z], axis=2)
        bot = jnp.concatenate([z, w[:, 1]], axis=2)
        return jnp.concatenate([top, bot], axis=1)
    return jnp.concatenate([pair(wa), pair(wx)], axis=2).astype(BF16)


def kernel(x, norm_mix_g, w_in, conv_w, conv_b, lru_wa, lru_ba, lru_wx, lru_bx, lru_lambda,
           forget_b, w_branch_a, w_branch_b, w_out, norm_mlp_g, w_up, w_down, norm_final_g):
    B, S, D = x.shape
    assert D == D_MODEL
    T = B * S
    tm = min(512, S)
    tq = min(1024, S)
    tk = tq // 2
    assert S % tm == 0 and S % tq == 0 and tm % LANES == 0

    cuts = (0, D, 2 * D, 3 * D, 4 * D, 5 * D, 7 * D, 7 * D + N_HEADS)
    seg = [w_in[:, cuts[i]:cuts[i + 1]] for i in range(7)]
    w_main = jnp.concatenate([seg[0], seg[1], seg[5], seg[2], seg[3]], axis=1).astype(BF16)
    w_vt = seg[4].T.astype(BF16)
    w_f = jnp.pad(seg[6], ((0, 0), (0, LANES - N_HEADS))).astype(BF16)
    wd = _block_diag_pairs(lru_wa, lru_wx)
    row = lambda v: v.reshape(1, -1).astype(F32)
    fb = jnp.pad(forget_b, (0, LANES - N_HEADS)).reshape(1, LANES)
    e_rows = jnp.arange(F_PARTS * LANES)
    e_tgt = jnp.where(e_rows % LANES < N_HEADS, (e_rows % LANES) * F_PARTS + e_rows // LANES, -1)
    e = (e_tgt[:, None] == jnp.arange(LANES)[None, :]).astype(BF16)

    x2 = x.reshape(T, D)
    pf, qk, vt, fl = _in_proj(x2, row(norm_mix_g), w_main, w_vt, w_f, tm)
    yag, kf, ft = _lru(pf, fl, conv_w, row(conv_b), wd, row(lru_ba), row(lru_bx),
                       row(lru_lambda), fb, w_branch_a.astype(BF16), e, B, S, tm)
    ob = _attn(qk, kf, vt, ft, B, S, tq, tk, hp=2)
    out = _merge_mlp(ob, yag, pf, x2, w_branch_b.astype(BF16), w_out.astype(BF16),
                     row(norm_mlp_g), w_up.astype(BF16), w_down.astype(BF16),
                     row(norm_final_g), tm)
    return out.reshape(B, S, D)
```

```python
import functools
import math

import jax
import jax.numpy as jnp
from jax import lax
from jax.experimental import pallas as pl
from jax.experimental.pallas import tpu as pltpu

D_MODEL = 1024
N_HEADS = 8
HEAD_DIM = 128
LRU_BLOCKS = 16
LRU_BW = 64
CONV_W = 4
LRU_C = 8.0
D_FF = 4 * D_MODEL
RMS_EPS = 1e-6
LANES = 128
SUBLANES = 8
LOG2E = 1.4426950408889634
Q_SCALE = LOG2E / math.sqrt(HEAD_DIM)
VMEM_LIMIT = 56 * 1024 * 1024

BF16 = jnp.bfloat16
F32 = jnp.float32
_NT_DIMS = (((1,), (1,)), ((), ()))
F_PARTS = 3


def _dot(a, b):
    return jnp.dot(a, b, preferred_element_type=F32)


def _rms(x, g):
    return x * lax.rsqrt(jnp.mean(x * x, axis=-1, keepdims=True) + RMS_EPS) * g


def _softplus(x):
    return jnp.maximum(x, 0.0) + jnp.log1p(jnp.exp(-jnp.abs(x)))


def _sigmoid(x):
    return 0.5 * jnp.tanh(0.5 * x) + 0.5


def _gelu_tanh(x):
    c = math.sqrt(2.0 / math.pi)
    half_x = 0.5 * x
    return half_x + half_x * jnp.tanh(x * (c + (c * 0.044715) * (x * x)))


def _const_spec(shape):
    nd = len(shape)
    return pl.BlockSpec(shape, lambda *_: (0,) * nd, pipeline_mode=pl.Buffered(1))


def _in_proj_kernel(x_ref, g_ref, w_ref, wvt_ref, wf_ref, pf_ref, qk_ref, vt_ref, fl_ref):
    u = _rms(x_ref[...], g_ref[...]).astype(BF16)
    fl_ref[...] = _dot(u, wf_ref[...])
    post = (None, _gelu_tanh, _sigmoid, _sigmoid)
    for j in range(4):
        cols = slice(j * D_MODEL, (j + 1) * D_MODEL)
        res = _dot(u, w_ref[:, cols])
        pf_ref[:, cols] = res if post[j] is None else post[j](res)
    for j in range(2):
        cols = slice((4 + j) * D_MODEL, (5 + j) * D_MODEL)
        res = _dot(u, w_ref[:, cols])
        if j == 0:
            res = res * Q_SCALE
        for h in range(N_HEADS):
            qk_ref[j, h] = res[:, h * HEAD_DIM:(h + 1) * HEAD_DIM].astype(BF16)
    vt = lax.dot_general(wvt_ref[...], u, _NT_DIMS, preferred_element_type=F32)
    for h in range(N_HEADS):
        vt_ref[h] = vt[h * HEAD_DIM:(h + 1) * HEAD_DIM, :].astype(BF16)


def _in_proj(x2, g, w_main, w_vt, w_f, tm):
    T = x2.shape[0]
    return pl.pallas_call(
        _in_proj_kernel,
        grid=(T // tm,),
        in_specs=[
            pl.BlockSpec((tm, D_MODEL), lambda i: (i, 0)),
            _const_spec((1, D_MODEL)),
            _const_spec((D_MODEL, 6 * D_MODEL)),
            _const_spec((D_MODEL, D_MODEL)),
            _const_spec((D_MODEL, LANES)),
        ],
        out_specs=[
            pl.BlockSpec((tm, 4 * D_MODEL), lambda i: (i, 0)),
            pl.BlockSpec((2, N_HEADS, tm, HEAD_DIM), lambda i: (0, 0, i, 0)),
            pl.BlockSpec((N_HEADS, HEAD_DIM, tm), lambda i: (0, 0, i)),
            pl.BlockSpec((tm, LANES), lambda i: (i, 0)),
        ],
        out_shape=[
            jax.ShapeDtypeStruct((T, 4 * D_MODEL), F32),
            jax.ShapeDtypeStruct((2, N_HEADS, T, HEAD_DIM), BF16),
            jax.ShapeDtypeStruct((N_HEADS, HEAD_DIM, T), BF16),
            jax.ShapeDtypeStruct((T, LANES), F32),
        ],
        compiler_params=pltpu.CompilerParams(
            dimension_semantics=("arbitrary",), vmem_limit_bytes=VMEM_LIMIT),
        name="in_proj",
    )(x2, g, w_main, w_vt, w_f)


def _lru_kernel(xl_ref, gl_ref, ga_ref, fl_ref, cw_ref, cb_ref, wd_ref, ba_ref, bx_ref,
                lam_ref, fb_ref, wa_ref, e_ref,
                yag_ref, kf_ref, ft_ref,
                xq_sc, a_sc, b_sc, h_sc, halo_sc, hc_sc, fc_sc, *, tm):
    s = pl.program_id(1)
    seg = tm // SUBLANES
    pitch = seg + SUBLANES
    gps = seg // SUBLANES
    nc = D_MODEL // LANES
    n_halo = CONV_W - 1

    @pl.when(s == 0)
    def _():
        halo_sc[...] = jnp.zeros_like(halo_sc)
        hc_sc[...] = jnp.zeros_like(hc_sc)
        fc_sc[...] = jnp.zeros_like(fc_sc)

    for g in range(tm // SUBLANES):
        u, j0 = g // gps, (g % gps) * SUBLANES
        for c in range(nc):
            xq_sc[c, pl.ds(j0 * SUBLANES + u, SUBLANES, stride=SUBLANES), :] = (
                xl_ref[g * SUBLANES:(g + 1) * SUBLANES, c * LANES:(c + 1) * LANES])

    sub = lax.broadcasted_iota(jnp.int32, (SUBLANES, LANES), 0)
    c_all = -LRU_C * _softplus(-lam_ref[...])
    for c in range(nc):
        cols = slice(c * LANES, (c + 1) * LANES)
        x = xq_sc[c]
        wrap = []
        for q in range(n_halo):
            rows = slice(q * SUBLANES, (q + 1) * SUBLANES)
            rolled = pltpu.roll(x[tm - n_halo * SUBLANES:, :][rows], 1, axis=0)
            wrap.append(jnp.where(sub == 0, halo_sc[c, rows, :], rolled))
            halo_sc[c, rows, :] = rolled
        xa = cb_ref[:, cols] + cw_ref[CONV_W - 1:CONV_W, cols] * x
        for d in range(1, CONV_W):
            xd = jnp.concatenate(wrap[n_halo - d:] + [x[:tm - d * SUBLANES, :]], axis=0)
            xa = xa + cw_ref[CONV_W - 1 - d:CONV_W - d, cols] * xd
        ri = _dot(xa.astype(BF16), wd_ref[c])
        t_r = jnp.tanh(0.5 * ri[:, :LANES] + 0.5 * ba_ref[:, cols])
        t_i = jnp.tanh(0.5 * ri[:, LANES:] + 0.5 * bx_ref[:, cols])
        half_c = 0.5 * c_all[:, cols]
        a = jnp.exp(half_c * t_r + half_c)
        half_xa = 0.5 * xa
        one_m_a2 = 1.0 - a * a
        mult = one_m_a2 * lax.rsqrt(jnp.maximum(one_m_a2, 1e-30))
        a_sc[c] = a
        b_sc[c] = mult * (half_xa * t_i + half_xa)

    def scan_step(j, carry):
        r0 = pl.multiple_of(j * SUBLANES, SUBLANES)
        out = []
        for c in range(nc):
            h, p = carry[c]
            a = a_sc[c, pl.ds(r0, SUBLANES), :]
            h = a * h + b_sc[c, pl.ds(r0, SUBLANES), :]
            p = a * p
            b_sc[c, pl.ds(r0, SUBLANES), :] = h
            a_sc[c, pl.ds(r0, SUBLANES), :] = p
            out.append((h, p))
        return tuple(out)

    ends = lax.fori_loop(
        0, seg, scan_step,
        tuple((jnp.zeros((SUBLANES, LANES), F32), jnp.ones((SUBLANES, LANES), F32))
              for _ in range(nc)), unroll=SUBLANES)

    carry_in = []
    for c in range(nc):
        cols = slice(c * LANES, (c + 1) * LANES)
        h_end, p_end = ends[c]
        for d in (1, 2, 4):
            keep = sub >= d
            p_sh = jnp.where(keep, pltpu.roll(p_end, d, axis=0), 1.0)
            h_sh = jnp.where(keep, pltpu.roll(h_end, d, axis=0), 0.0)
            h_end = p_end * h_sh + h_end
            p_end = p_end * p_sh
        blk_in = hc_sc[:, cols]
        seg_out = h_end + p_end * blk_in
        carry_in.append(jnp.where(sub == 0, blk_in, pltpu.roll(seg_out, 1, axis=0)))
        hc_sc[:, cols] = jnp.broadcast_to(seg_out[SUBLANES - 1:SUBLANES, :], (SUBLANES, LANES))

    def fix_step(j, _):
        r0 = pl.multiple_of(j * SUBLANES, SUBLANES)
        for c in range(nc):
            h = b_sc[c, pl.ds(r0, SUBLANES), :] + a_sc[c, pl.ds(r0, SUBLANES), :] * carry_in[c]
            h_sc[c, pl.ds(j, SUBLANES, stride=pitch), :] = h
        return 0

    lax.fori_loop(0, seg, fix_step, 0, unroll=SUBLANES)

    h = jnp.concatenate(
        [jnp.concatenate([h_sc[c, u * pitch:u * pitch + seg, :] for u in range(SUBLANES)], axis=0)
         for c in range(nc)], axis=1)
    y = (gl_ref[...] * h).astype(BF16)
    yag_ref[...] = ga_ref[...] * _dot(y, wa_ref[...])

    lf = -_softplus(-(fl_ref[...] + fb_ref[...])) * LOG2E
    rowf = lax.broadcasted_iota(jnp.int32, (tm, LANES), 0)
    d = 1
    while d < tm:
        lf = lf + jnp.where(rowf >= d, pltpu.roll(lf, d, axis=0), 0.0)
        d *= 2
    fblk = lf + fc_sc[...]
    fc_sc[...] = fblk[tm - 1:tm, :]
    ft_ref[...] = fblk.T[0:N_HEADS, :]
    hi = (-fblk).astype(BF16)
    r1 = -fblk - hi.astype(F32)
    mid = r1.astype(BF16)
    lo = (r1 - mid.astype(F32)).astype(BF16)
    kf_ref[...] = _dot(jnp.concatenate([hi, mid, lo], axis=1), e_ref[...]).astype(BF16)


def _lru(pf, fl, cw, cb, wd, ba, bx, lam, fb, wa, e, B, S, tm):
    T = B * S
    ns = S // tm
    row_blk = lambda c: pl.BlockSpec((tm, D_MODEL), lambda b, s, c=c: (b * ns + s, c))
    return pl.pallas_call(
        functools.partial(_lru_kernel, tm=tm),
        grid=(B, ns),
        in_specs=[
            row_blk(0), row_blk(1), row_blk(2),
            pl.BlockSpec((tm, LANES), lambda b, s: (b * ns + s, 0)),
            _const_spec((CONV_W, D_MODEL)),
            _const_spec((1, D_MODEL)),
            _const_spec((D_MODEL // LANES, LANES, 2 * LANES)),
            _const_spec((1, D_MODEL)),
            _const_spec((1, D_MODEL)),
            _const_spec((1, D_MODEL)),
            _const_spec((1, LANES)),
            _const_spec((D_MODEL, D_MODEL)),
            _const_spec((F_PARTS * LANES, LANES)),
        ],
        out_specs=[
            pl.BlockSpec((tm, D_MODEL), lambda b, s: (b * ns + s, 0)),
            pl.BlockSpec((tm, LANES), lambda b, s: (b * ns + s, 0)),
            pl.BlockSpec((None, N_HEADS, tm), lambda b, s: (b, 0, s)),
        ],
        out_shape=[
            jax.ShapeDtypeStruct((T, D_MODEL), F32),
            jax.ShapeDtypeStruct((T, LANES), BF16),
            jax.ShapeDtypeStruct((B, N_HEADS, S), F32),
        ],
        scratch_shapes=[
            pltpu.VMEM((D_MODEL // LANES, tm, LANES), F32),
            pltpu.VMEM((D_MODEL // LANES, tm, LANES), F32),
            pltpu.VMEM((D_MODEL // LANES, tm, LANES), F32),
            pltpu.VMEM((D_MODEL // LANES, tm + SUBLANES * SUBLANES, LANES), F32),
            pltpu.VMEM((D_MODEL // LANES, (CONV_W - 1) * SUBLANES, LANES), F32),
            pltpu.VMEM((SUBLANES, D_MODEL), F32),
            pltpu.VMEM((1, LANES), F32),
        ],
        compiler_params=pltpu.CompilerParams(
            dimension_semantics=("arbitrary", "arbitrary"), vmem_limit_bytes=VMEM_LIMIT),
        name="lru",
    )(pf, pf, pf, fl, cw, cb, wd, ba, bx, lam, fb, wa, e)


def _col_reduce(op, z):
    rows, cols = z.shape
    return op(op(z.reshape(SUBLANES, rows // SUBLANES, cols), axis=0), axis=0, keepdims=True)


def _attn_kernel(q_ref, qn_ref, k_ref, kf_ref, vt_ref, ft_ref, o_ref,
                 z_sc, p_sc, acc_sc, m_sc, l_sc, zmax_sc, *, tq, tk, hp):
    hg = pl.program_id(1)
    qi = pl.program_id(2)
    lane = lax.broadcasted_iota(jnp.int32, (tq, LANES), 1)
    upper = slice(tk, tq)
    qa, qa_next, f_t = [], [], []
    for a in range(hp):
        h = hg * hp + a
        ones = jnp.where((lane >= F_PARTS * h) & (lane < F_PARTS * (h + 1)), 1.0, 0.0).astype(BF16)
        qa.append(jnp.concatenate([q_ref[a], ones], axis=1))
        qa_next.append(jnp.concatenate([qn_ref[a], ones], axis=1))
        f_t.append(ft_ref[pl.ds(h, 1), :])

    def qk(a, j, queries):
        c0 = pl.multiple_of(j * tk, tk)
        ka = jnp.concatenate([k_ref[a, pl.ds(c0, tk), :], kf_ref[pl.ds(c0, tk), :]], axis=1)
        return lax.dot_general(ka, queries, _NT_DIMS, preferred_element_type=F32)

    def pv(a, j, p):
        c0 = pl.multiple_of(j * tk, tk)
        return _dot(vt_ref[a, :, pl.ds(c0, tk)], p)

    def softmax(z, m_row, zmax, f_row, masked):
        if masked:
            kr = lax.broadcasted_iota(jnp.int32, z.shape, 0)
            qc = lax.broadcasted_iota(jnp.int32, z.shape, 1)
            z = jnp.where(kr <= qc, z, -jnp.inf)
            zmax = _col_reduce(jnp.max, z)
        m_new = jnp.maximum(m_row, zmax + f_row)
        p = jnp.exp2(z - (m_new - f_row))
        return m_new, jnp.exp2(m_row - m_new), _col_reduce(jnp.sum, p), p.astype(BF16)

    def trip(j, slot, masked=False):
        zmax_next = []
        for a in range(hp):
            if masked:
                z_sc[1 - slot, a, :, upper] = qk(a, j + 1, qa[a][upper])
            else:
                zn = qk(a, j + 1, qa[a])
                z_sc[1 - slot, a] = zn
                zmax_next.append(_col_reduce(jnp.max, zn))
        pvs = [pv(a, jnp.maximum(j - 1, 0), p_sc[a]) for a in range(hp)]
        for a in range(hp):
            m_new, alpha, psum, p = softmax(z_sc[slot, a], m_sc[a], zmax_sc[a], f_t[a], masked)
            p_sc[a] = p
            m_sc[a] = m_new
            l_sc[a] = alpha * l_sc[a] + psum
            acc_sc[a] = alpha * (acc_sc[a] + pvs[a])
            if not masked:
                zmax_sc[a] = zmax_next[a]

    @pl.when(qi == 0)
    def _():
        for a in range(hp):
            z0 = qk(a, 0, qa[a])
            z_sc[0, a] = z0
            zmax_sc[a] = _col_reduce(jnp.max, z0)

    for a in range(hp):
        p_sc[a] = jnp.zeros((tk, tq), BF16)
        m_sc[a] = jnp.full((1, tq), -jnp.inf, F32)
        l_sc[a] = jnp.zeros((1, tq), F32)
        acc_sc[a] = jnp.zeros((HEAD_DIM, tq), F32)

    @pl.loop(0, qi)
    def _(i):
        trip(2 * i, 0)
        trip(2 * i + 1, 1)

    first = 2 * qi
    trip(first, 0, masked=True)
    for a in range(hp):
        acc = acc_sc[a] + pv(a, first, p_sc[a])
        m_new, alpha, psum, p = softmax(z_sc[1, a, :, upper], m_sc[a][:, upper], None,
                                        f_t[a][:, upper], True)
        acc_up = alpha * acc[:, upper] + pv(a, first + 1, p)
        l_up = alpha * l_sc[a][:, upper] + psum
        out = jnp.concatenate([acc[:, :tk] / l_sc[a][:, :tk], acc_up / l_up], axis=1)
        o_ref[a] = out.T.astype(BF16)

    for a in range(hp):
        zn = qk(a, 0, qa_next[a])
        z_sc[0, a] = zn
        zmax_sc[a] = _col_reduce(jnp.max, zn)


def _attn(qk, kf, vt, ft, B, S, tq, tk, hp):
    T = B * S
    nq = S // tq
    assert tq == 2 * tk
    return pl.pallas_call(
        functools.partial(_attn_kernel, tq=tq, tk=tk, hp=hp),
        grid=(B, N_HEADS // hp, nq),
        in_specs=[
            pl.BlockSpec((None, hp, tq, HEAD_DIM), lambda b, g, i: (0, g, b * nq + i, 0)),
            pl.BlockSpec((None, hp, tq, HEAD_DIM),
                         lambda b, g, i: (0, g, b * nq + jnp.minimum(i + 1, nq - 1), 0)),
            pl.BlockSpec((None, hp, S, HEAD_DIM), lambda b, g, i: (1, g, b, 0)),
            pl.BlockSpec((S, LANES), lambda b, g, i: (b, 0)),
            pl.BlockSpec((hp, HEAD_DIM, S), lambda b, g, i: (g, 0, b)),
            pl.BlockSpec((None, N_HEADS, tq), lambda b, g, i: (b, 0, i)),
        ],
        out_specs=pl.BlockSpec((hp, tq, HEAD_DIM), lambda b, g, i: (g, b * nq + i, 0)),
        out_shape=jax.ShapeDtypeStruct((N_HEADS, T, HEAD_DIM), BF16),
        scratch_shapes=[
            pltpu.VMEM((2, hp, tk, tq), F32),
            pltpu.VMEM((hp, tk, tq), BF16),
            pltpu.VMEM((hp, HEAD_DIM, tq), F32),
            pltpu.VMEM((hp, 1, tq), F32),
            pltpu.VMEM((hp, 1, tq), F32),
            pltpu.VMEM((hp, 1, tq), F32),
        ],
        compiler_params=pltpu.CompilerParams(
            dimension_semantics=("arbitrary", "arbitrary", "arbitrary"),
            vmem_limit_bytes=VMEM_LIMIT),
        name="fox_attn",
    )(qk, qk, qk, kf, vt, ft)


def _merge_mlp_kernel(ob_ref, yag_ref, gb_ref, x_ref, wb_ref, wo_ref, gm_ref, wu_ref,
                      wdn_ref, gf_ref, o_ref):
    ob = jnp.concatenate([ob_ref[h] for h in range(N_HEADS)], axis=1)
    yb = _dot(ob, wb_ref[...])
    mix = yag_ref[...] + gb_ref[...] * yb
    x1 = x_ref[...] + _dot(mix.astype(BF16), wo_ref[...])
    m = _rms(x1, gm_ref[...]).astype(BF16)
    acc = x1
    for c in range(D_FF // D_MODEL):
        cols = slice(c * D_MODEL, (c + 1) * D_MODEL)
        hc = jnp.maximum(_dot(m, wu_ref[:, cols]), 0.0)
        acc = acc + _dot((hc * hc).astype(BF16), wdn_ref[cols, :])
    o_ref[...] = _rms(acc, gf_ref[...])


def _merge_mlp(ob, yag, pf, x2, wb, wo, gm, wu, wdn, gf, tm):
    T = x2.shape[0]
    return pl.pallas_call(
        _merge_mlp_kernel,
        grid=(T // tm,),
        in_specs=[
            pl.BlockSpec((N_HEADS, tm, HEAD_DIM), lambda i: (0, i, 0)),
            pl.BlockSpec((tm, D_MODEL), lambda i: (i, 0)),
            pl.BlockSpec((tm, D_MODEL), lambda i: (i, 3)),
            pl.BlockSpec((tm, D_MODEL), lambda i: (i, 0)),
            _const_spec((D_MODEL, D_MODEL)),
            _const_spec((D_MODEL, D_MODEL)),
            _const_spec((1, D_MODEL)),
            _const_spec((D_MODEL, D_FF)),
            _const_spec((D_FF, D_MODEL)),
            _const_spec((1, D_MODEL)),
        ],
        out_specs=pl.BlockSpec((tm, D_MODEL), lambda i: (i, 0)),
        out_shape=jax.ShapeDtypeStruct((T, D_MODEL), F32),
        compiler_params=pltpu.CompilerParams(
            dimension_semantics=("arbitrary",), vmem_limit_bytes=VMEM_LIMIT),
        name="merge_mlp",
    )(ob, yag, pf, x2, wb, wo, gm, wu, wdn, gf)


def _block_diag_pairs(wa, wx):
    def pair(w):
        w = w.reshape(LRU_BLOCKS // 2, 2, LRU_BW, LRU_BW)
        z = jnp.zeros_like(w[:, 0])
        top = jnp.concatenate([w[:, 0], z], axis=2)
        bot = jnp.concatenate([z, w[:, 1]], axis=2)
        return jnp.concatenate([top, bot], axis=1)
    return jnp.concatenate([pair(wa), pair(wx)], axis=2).astype(BF16)


def kernel(x, norm_mix_g, w_in, conv_w, conv_b, lru_wa, lru_ba, lru_wx, lru_bx, lru_lambda,
           forget_b, w_branch_a, w_branch_b, w_out, norm_mlp_g, w_up, w_down, norm_final_g):
    B, S, D = x.shape
    assert D == D_MODEL
    T = B * S
    tm = min(512, S)
    tq = min(1024, S)
    tk = tq // 2
    assert S % tm == 0 and S % tq == 0 and tm % LANES == 0

    cuts = (0, D, 2 * D, 3 * D, 4 * D, 5 * D, 7 * D, 7 * D + N_HEADS)
    seg = [w_in[:, cuts[i]:cuts[i + 1]] for i in range(7)]
    w_main = jnp.concatenate([seg[0], seg[1], seg[5], seg[2], seg[3]], axis=1).astype(BF16)
    w_vt = seg[4].T.astype(BF16)
    w_f = jnp.pad(seg[6], ((0, 0), (0, LANES - N_HEADS))).astype(BF16)
    wd = _block_diag_pairs(lru_wa, lru_wx)
    row = lambda v: v.reshape(1, -1).astype(F32)
    fb = jnp.pad(forget_b, (0, LANES - N_HEADS)).reshape(1, LANES)
    e_rows = jnp.arange(F_PARTS * LANES)
    e_tgt = jnp.where(e_rows % LANES < N_HEADS, (e_rows % LANES) * F_PARTS + e_rows // LANES, -1)
    e = (e_tgt[:, None] == jnp.arange(LANES)[None, :]).astype(BF16)

    x2 = x.reshape(T, D)
    pf, qk, vt, fl = _in_proj(x2, row(norm_mix_g), w_main, w_vt, w_f, tm)
    yag, kf, ft = _lru(pf, fl, conv_w, row(conv_b), wd, row(lru_ba), row(lru_bx),
                       row(lru_lambda), fb, w_branch_a.astype(BF16), e, B, S, tm)
    ob = _attn(qk, kf, vt, ft, B, S, tq, tk, hp=2)
    out = _merge_mlp(ob, yag, pf, x2, w_branch_b.astype(BF16), w_out.astype(BF16),
                     row(norm_mlp_g), w_up.astype(BF16), w_down.astype(BF16),
                     row(norm_final_g), tm)
    return out.reshape(B, S, D)
```

```python
import functools
import math

import jax
import jax.numpy as jnp
from jax import lax
from jax.experimental import pallas as pl
from jax.experimental.pallas import tpu as pltpu

D_MODEL = 1024
N_HEADS = 8
HEAD_DIM = 128
LRU_BLOCKS = 16
LRU_BW = 64
CONV_W = 4
LRU_C = 8.0
D_FF = 4 * D_MODEL
RMS_EPS = 1e-6
LANES = 128
SUBLANES = 8
BF16_SUBLANES = 16
LOG2E = 1.4426950408889634
Q_SCALE = LOG2E / math.sqrt(HEAD_DIM)
VMEM_LIMIT = 56 * 1024 * 1024

BF16 = jnp.bfloat16
F32 = jnp.float32
_NT_DIMS = (((1,), (1,)), ((), ()))
W_IN_PF_SLABS = (0, 1, 5, 6)
W_IN_QK_SLABS = (2, 3)
W_IN_V_SLAB = 4
F_PARTS = 3


def _dot(a, b):
    return jnp.dot(a, b, preferred_element_type=F32)


def _rms(x, g):
    return x * lax.rsqrt(jnp.mean(x * x, axis=-1, keepdims=True) + RMS_EPS) * g


def _softplus(x):
    return jnp.maximum(x, 0.0) + jnp.log1p(jnp.exp(-jnp.abs(x)))


def _sigmoid(x):
    return 0.5 * jnp.tanh(0.5 * x) + 0.5


def _gelu_tanh(x):
    c = math.sqrt(2.0 / math.pi)
    half_x = 0.5 * x
    return half_x + half_x * jnp.tanh(x * (c + (c * 0.044715) * (x * x)))


def _const_spec(shape):
    nd = len(shape)
    return pl.BlockSpec(shape, lambda *_: (0,) * nd, pipeline_mode=pl.Buffered(1))


def _in_proj_kernel(x_ref, g_ref, w_ref, wvt_ref, wf_ref, pf_ref, qk_ref, vt_ref, fl_ref):
    u = _rms(x_ref[...], g_ref[...]).astype(BF16)
    fl_ref[...] = _dot(u, wf_ref[...])
    post = (None, _gelu_tanh, _sigmoid, _sigmoid)
    for j, src in enumerate(W_IN_PF_SLABS):
        res = _dot(u, w_ref[:, src * D_MODEL:(src + 1) * D_MODEL])
        pf_ref[:, j * D_MODEL:(j + 1) * D_MODEL] = res if post[j] is None else post[j](res)
    for j, src in enumerate(W_IN_QK_SLABS):
        res = _dot(u, w_ref[:, src * D_MODEL:(src + 1) * D_MODEL])
        if j == 0:
            res = res * Q_SCALE
        for h in range(N_HEADS):
            qk_ref[j, h] = res[:, h * HEAD_DIM:(h + 1) * HEAD_DIM].astype(BF16)
    vt = lax.dot_general(wvt_ref[...], u, _NT_DIMS, preferred_element_type=F32)
    for h in range(N_HEADS):
        vt_ref[h] = vt[h * HEAD_DIM:(h + 1) * HEAD_DIM, :].astype(BF16)


def _in_proj(x2, g, w_main, w_vt, w_f, tm):
    T = x2.shape[0]
    return pl.pallas_call(
        _in_proj_kernel,
        grid=(T // tm,),
        in_specs=[
            pl.BlockSpec((tm, D_MODEL), lambda i: (i, 0)),
            _const_spec((1, D_MODEL)),
            _const_spec((D_MODEL, 7 * D_MODEL)),
            _const_spec((D_MODEL, D_MODEL)),
            _const_spec((D_MODEL, LANES)),
        ],
        out_specs=[
            pl.BlockSpec((tm, 4 * D_MODEL), lambda i: (i, 0)),
            pl.BlockSpec((2, N_HEADS, tm, HEAD_DIM), lambda i: (0, 0, i, 0)),
            pl.BlockSpec((N_HEADS, HEAD_DIM, tm), lambda i: (0, 0, i)),
            pl.BlockSpec((tm, LANES), lambda i: (i, 0)),
        ],
        out_shape=[
            jax.ShapeDtypeStruct((T, 4 * D_MODEL), F32),
            jax.ShapeDtypeStruct((2, N_HEADS, T, HEAD_DIM), BF16),
            jax.ShapeDtypeStruct((N_HEADS, HEAD_DIM, T), BF16),
            jax.ShapeDtypeStruct((T, LANES), F32),
        ],
        compiler_params=pltpu.CompilerParams(
            dimension_semantics=("arbitrary",), vmem_limit_bytes=VMEM_LIMIT),
        name="in_proj",
    )(x2, g, w_main, w_vt, w_f)


def _lru_kernel(xl_ref, gl_ref, ga_ref, fl_ref, cw_ref, cb_ref, wd_ref, ba_ref, bx_ref,
                lam_ref, fb_ref, wa_ref, e_ref,
                yag_ref, kf_ref, ft_ref,
                xq_sc, a_sc, b_sc, h_sc, halo_sc, hc_sc, fc_sc, *, tm):
    s = pl.program_id(1)
    seg = tm // SUBLANES
    pitch = seg + SUBLANES
    gps = seg // SUBLANES
    nc = D_MODEL // LANES
    n_halo = CONV_W - 1

    @pl.when(s == 0)
    def _():
        halo_sc[...] = jnp.zeros_like(halo_sc)
        hc_sc[...] = jnp.zeros_like(hc_sc)
        fc_sc[...] = jnp.zeros_like(fc_sc)

    for g in range(tm // SUBLANES):
        u, j0 = g // gps, (g % gps) * SUBLANES
        for c in range(nc):
            xq_sc[c, pl.ds(j0 * SUBLANES + u, SUBLANES, stride=SUBLANES), :] = (
                xl_ref[g * SUBLANES:(g + 1) * SUBLANES, c * LANES:(c + 1) * LANES])

    sub = lax.broadcasted_iota(jnp.int32, (SUBLANES, LANES), 0)
    c_all = -LRU_C * _softplus(-lam_ref[...])
    for c in range(nc):
        cols = slice(c * LANES, (c + 1) * LANES)
        x = xq_sc[c]
        wrap = []
        for q in range(n_halo):
            rows = slice(q * SUBLANES, (q + 1) * SUBLANES)
            rolled = pltpu.roll(x[tm - n_halo * SUBLANES:, :][rows], 1, axis=0)
            wrap.append(jnp.where(sub == 0, halo_sc[c, rows, :], rolled))
            halo_sc[c, rows, :] = rolled
        xa = cb_ref[:, cols] + cw_ref[CONV_W - 1:CONV_W, cols] * x
        for d in range(1, CONV_W):
            xd = jnp.concatenate(wrap[n_halo - d:] + [x[:tm - d * SUBLANES, :]], axis=0)
            xa = xa + cw_ref[CONV_W - 1 - d:CONV_W - d, cols] * xd
        ri = _dot(xa.astype(BF16), wd_ref[c])
        t_r = jnp.tanh(0.5 * ri[:, :LANES] + 0.5 * ba_ref[:, cols])
        t_i = jnp.tanh(0.5 * ri[:, LANES:] + 0.5 * bx_ref[:, cols])
        half_c = 0.5 * c_all[:, cols]
        a = jnp.exp(half_c * t_r + half_c)
        half_xa = 0.5 * xa
        one_m_a2 = 1.0 - a * a
        mult = one_m_a2 * lax.rsqrt(jnp.maximum(one_m_a2, 1e-30))
        a_sc[c] = a
        b_sc[c] = mult * (half_xa * t_i + half_xa)

    def scan_step(j, carry):
        r0 = pl.multiple_of(j * SUBLANES, SUBLANES)
        out = []
        for c in range(nc):
            h, p = carry[c]
            a = a_sc[c, pl.ds(r0, SUBLANES), :]
            h = a * h + b_sc[c, pl.ds(r0, SUBLANES), :]
            p = a * p
            b_sc[c, pl.ds(r0, SUBLANES), :] = h
            a_sc[c, pl.ds(r0, SUBLANES), :] = p
            out.append((h, p))
        return tuple(out)

    ends = lax.fori_loop(
        0, seg, scan_step,
        tuple((jnp.zeros((SUBLANES, LANES), F32), jnp.ones((SUBLANES, LANES), F32))
              for _ in range(nc)), unroll=SUBLANES)

    carry_in = []
    for c in range(nc):
        cols = slice(c * LANES, (c + 1) * LANES)
        h_end, p_end = ends[c]
        for d in (1, 2, 4):
            keep = sub >= d
            p_sh = jnp.where(keep, pltpu.roll(p_end, d, axis=0), 1.0)
            h_sh = jnp.where(keep, pltpu.roll(h_end, d, axis=0), 0.0)
            h_end = p_end * h_sh + h_end
            p_end = p_end * p_sh
        blk_in = hc_sc[:, cols]
        seg_out = h_end + p_end * blk_in
        carry_in.append(jnp.where(sub == 0, blk_in, pltpu.roll(seg_out, 1, axis=0)))
        hc_sc[:, cols] = jnp.broadcast_to(seg_out[SUBLANES - 1:SUBLANES, :], (SUBLANES, LANES))

    def fix_step(j, _):
        r0 = pl.multiple_of(j * SUBLANES, SUBLANES)
        for c in range(nc):
            h = b_sc[c, pl.ds(r0, SUBLANES), :] + a_sc[c, pl.ds(r0, SUBLANES), :] * carry_in[c]
            h_sc[c, pl.ds(j, SUBLANES, stride=pitch), :] = h
        return 0

    lax.fori_loop(0, seg, fix_step, 0, unroll=SUBLANES)

    h = jnp.concatenate(
        [jnp.concatenate([h_sc[c, u * pitch:u * pitch + seg, :] for u in range(SUBLANES)], axis=0)
         for c in range(nc)], axis=1)
    y = (gl_ref[...] * h).astype(BF16)
    yag_ref[...] = ga_ref[...] * _dot(y, wa_ref[...])

    lf = -_softplus(-(fl_ref[...] + fb_ref[...])) * LOG2E
    rowf = lax.broadcasted_iota(jnp.int32, (tm, LANES), 0)
    d = 1
    while d < tm:
        lf = lf + jnp.where(rowf >= d, pltpu.roll(lf, d, axis=0), 0.0)
        d *= 2
    fblk = lf + fc_sc[...]
    fc_sc[...] = fblk[tm - 1:tm, :]
    ft_ref[...] = fblk.T[0:N_HEADS, :]
    hi = (-fblk).astype(BF16)
    r1 = -fblk - hi.astype(F32)
    mid = r1.astype(BF16)
    lo = (r1 - mid.astype(F32)).astype(BF16)
    kf_ref[...] = _dot(jnp.concatenate([hi, mid, lo], axis=1), e_ref[...]).astype(BF16)


def _lru(pf, fl, cw, cb, wd, ba, bx, lam, fb, wa, e, B, S, tm):
    T = B * S
    ns = S // tm
    row_blk = lambda c: pl.BlockSpec((tm, D_MODEL), lambda b, s, c=c: (b * ns + s, c))
    return pl.pallas_call(
        functools.partial(_lru_kernel, tm=tm),
        grid=(B, ns),
        in_specs=[
            row_blk(0), row_blk(1), row_blk(2),
            pl.BlockSpec((tm, LANES), lambda b, s: (b * ns + s, 0)),
            _const_spec((CONV_W, D_MODEL)),
            _const_spec((1, D_MODEL)),
            _const_spec((D_MODEL // LANES, LANES, 2 * LANES)),
            _const_spec((1, D_MODEL)),
            _const_spec((1, D_MODEL)),
            _const_spec((1, D_MODEL)),
            _const_spec((1, LANES)),
            _const_spec((D_MODEL, D_MODEL)),
            _const_spec((F_PARTS * LANES, LANES)),
        ],
        out_specs=[
            pl.BlockSpec((tm, D_MODEL), lambda b, s: (b * ns + s, 0)),
            pl.BlockSpec((tm, LANES), lambda b, s: (b * ns + s, 0)),
            pl.BlockSpec((None, N_HEADS, tm), lambda b, s: (b, 0, s)),
        ],
        out_shape=[
            jax.ShapeDtypeStruct((T, D_MODEL), F32),
            jax.ShapeDtypeStruct((T, LANES), BF16),
            jax.ShapeDtypeStruct((B, N_HEADS, S), F32),
        ],
        scratch_shapes=[
            pltpu.VMEM((D_MODEL // LANES, tm, LANES), F32),
            pltpu.VMEM((D_MODEL // LANES, tm, LANES), F32),
            pltpu.VMEM((D_MODEL // LANES, tm, LANES), F32),
            pltpu.VMEM((D_MODEL // LANES, tm + SUBLANES * SUBLANES, LANES), F32),
            pltpu.VMEM((D_MODEL // LANES, (CONV_W - 1) * SUBLANES, LANES), F32),
            pltpu.VMEM((SUBLANES, D_MODEL), F32),
            pltpu.VMEM((1, LANES), F32),
        ],
        compiler_params=pltpu.CompilerParams(
            dimension_semantics=("arbitrary", "arbitrary"), vmem_limit_bytes=VMEM_LIMIT),
        name="lru",
    )(pf, pf, pf, fl, cw, cb, wd, ba, bx, lam, fb, wa, e)


def _col_reduce(op, z):
    rows, cols = z.shape
    return op(op(z.reshape(SUBLANES, rows // SUBLANES, cols), axis=0), axis=0, keepdims=True)


def _attn_kernel(q_ref, qn_ref, k_ref, kf_ref, vt_ref, ft_ref, o_ref,
                 z_sc, p_sc, acc_sc, m_sc, zmax_sc, *, tq, tk, hp):
    hg = pl.program_id(1)
    qi = pl.program_id(2)
    lane = lax.broadcasted_iota(jnp.int32, (tq, LANES), 1)
    upper = slice(tk, tq)
    qa, qa_next, f_t = [], [], []
    for a in range(hp):
        h = hg * hp + a
        ones = jnp.where((lane >= F_PARTS * h) & (lane < F_PARTS * (h + 1)), 1.0, 0.0).astype(BF16)
        qa.append(jnp.concatenate([q_ref[a], ones], axis=1))
        qa_next.append(jnp.concatenate([qn_ref[a], ones], axis=1))
        f_t.append(ft_ref[pl.ds(h, 1), :])

    def qk(a, j, queries):
        c0 = pl.multiple_of(j * tk, tk)
        ka = jnp.concatenate([k_ref[a, pl.ds(c0, tk), :], kf_ref[pl.ds(c0, tk), :]], axis=1)
        return lax.dot_general(ka, queries, _NT_DIMS, preferred_element_type=F32)

    ones_rows = jnp.ones((BF16_SUBLANES, tk), BF16)

    def pv(a, j, p):
        c0 = pl.multiple_of(j * tk, tk)
        vt_aug = jnp.concatenate([vt_ref[a, :, pl.ds(c0, tk)], ones_rows], axis=0)
        return _dot(vt_aug, p)

    def softmax(z, m_row, zmax, f_row, masked):
        if masked:
            kr = lax.broadcasted_iota(jnp.int32, z.shape, 0)
            qc = lax.broadcasted_iota(jnp.int32, z.shape, 1)
            z = jnp.where(kr <= qc, z, -jnp.inf)
            zmax = _col_reduce(jnp.max, z)
        m_new = jnp.maximum(m_row, zmax + f_row)
        return m_new, jnp.exp2(m_row - m_new), jnp.exp2(z - (m_new - f_row)).astype(BF16)

    def trip(j, slot, masked=False):
        zmax_next = []
        for a in range(hp):
            if masked:
                z_sc[1 - slot, a, :, upper] = qk(a, j + 1, qa[a][upper])
            else:
                zn = qk(a, j + 1, qa[a])
                z_sc[1 - slot, a] = zn
                zmax_next.append(_col_reduce(jnp.max, zn))
        pvs = [pv(a, jnp.maximum(j - 1, 0), p_sc[a]) for a in range(hp)]
        for a in range(hp):
            m_new, alpha, p = softmax(z_sc[slot, a], m_sc[a], zmax_sc[a], f_t[a], masked)
            p_sc[a] = p
            m_sc[a] = m_new
            acc_sc[a] = alpha * (acc_sc[a] + pvs[a])
            if not masked:
                zmax_sc[a] = zmax_next[a]

    @pl.when((pl.program_id(0) == 0) & (hg == 0) & (qi == 0))
    def _():
        p_sc[...] = jnp.zeros_like(p_sc)

    @pl.when(qi == 0)
    def _():
        for a in range(hp):
            z0 = qk(a, 0, qa[a])
            z_sc[0, a] = z0
            zmax_sc[a] = _col_reduce(jnp.max, z0)

    for a in range(hp):
        m_sc[a] = jnp.full((1, tq), -jnp.inf, F32)
        acc_sc[a] = jnp.zeros((HEAD_DIM + BF16_SUBLANES, tq), F32)

    @pl.loop(0, qi)
    def _(i):
        trip(2 * i, 0)
        trip(2 * i + 1, 1)

    first = 2 * qi
    trip(first, 0, masked=True)
    for a in range(hp):
        acc = acc_sc[a] + pv(a, first, p_sc[a])
        _, alpha, p = softmax(z_sc[1, a, :, upper], m_sc[a][:, upper], None, f_t[a][:, upper], True)
        acc_up = alpha * acc[:, upper] + pv(a, first + 1, p)
        out = jnp.concatenate(
            [acc[:HEAD_DIM, :tk] / acc[HEAD_DIM:HEAD_DIM + 1, :tk],
             acc_up[:HEAD_DIM] / acc_up[HEAD_DIM:HEAD_DIM + 1]], axis=1)
        o_ref[a] = out.T.astype(BF16)

    for a in range(hp):
        zn = qk(a, 0, qa_next[a])
        z_sc[0, a] = zn
        zmax_sc[a] = _col_reduce(jnp.max, zn)


def _attn(qk, kf, vt, ft, B, S, tq, tk, hp):
    T = B * S
    nq = S // tq
    assert tq == 2 * tk
    return pl.pallas_call(
        functools.partial(_attn_kernel, tq=tq, tk=tk, hp=hp),
        grid=(B, N_HEADS // hp, nq),
        in_specs=[
            pl.BlockSpec((None, hp, tq, HEAD_DIM), lambda b, g, i: (0, g, b * nq + i, 0)),
            pl.BlockSpec((None, hp, tq, HEAD_DIM),
                         lambda b, g, i: (0, g, b * nq + jnp.minimum(i + 1, nq - 1), 0)),
            pl.BlockSpec((None, hp, S, HEAD_DIM), lambda b, g, i: (1, g, b, 0)),
            pl.BlockSpec((S, LANES), lambda b, g, i: (b, 0)),
            pl.BlockSpec((hp, HEAD_DIM, S), lambda b, g, i: (g, 0, b)),
            pl.BlockSpec((None, N_HEADS, tq), lambda b, g, i: (b, 0, i)),
        ],
        out_specs=pl.BlockSpec((hp, tq, HEAD_DIM), lambda b, g, i: (g, b * nq + i, 0)),
        out_shape=jax.ShapeDtypeStruct((N_HEADS, T, HEAD_DIM), BF16),
        scratch_shapes=[
            pltpu.VMEM((2, hp, tk, tq), F32),
            pltpu.VMEM((hp, tk, tq), BF16),
            pltpu.VMEM((hp, HEAD_DIM + BF16_SUBLANES, tq), F32),
            pltpu.VMEM((hp, 1, tq), F32),
            pltpu.VMEM((hp, 1, tq), F32),
        ],
        compiler_params=pltpu.CompilerParams(
            dimension_semantics=("arbitrary", "arbitrary", "arbitrary"),
            vmem_limit_bytes=VMEM_LIMIT),
        name="fox_attn",
    )(qk, qk, qk, kf, vt, ft)


def _merge_mlp_kernel(ob_ref, yag_ref, gb_ref, x_ref, wb_ref, wo_ref, gm_ref, wu_ref,
                      wdn_ref, gf_ref, o_ref):
    ob = jnp.concatenate([ob_ref[h] for h in range(N_HEADS)], axis=1)
    yb = _dot(ob, wb_ref[...])
    mix = yag_ref[...] + gb_ref[...] * yb
    x1 = x_ref[...] + _dot(mix.astype(BF16), wo_ref[...])
    m = _rms(x1, gm_ref[...]).astype(BF16)
    acc = x1
    for c in range(D_FF // D_MODEL):
        cols = slice(c * D_MODEL, (c + 1) * D_MODEL)
        hc = jnp.maximum(_dot(m, wu_ref[:, cols]), 0.0)
        acc = acc + _dot((hc * hc).astype(BF16), wdn_ref[cols, :])
    o_ref[...] = _rms(acc, gf_ref[...])


def _merge_mlp(ob, yag, pf, x2, wb, wo, gm, wu, wdn, gf, tm):
    T = x2.shape[0]
    return pl.pallas_call(
        _merge_mlp_kernel,
        grid=(T // tm,),
        in_specs=[
            pl.BlockSpec((N_HEADS, tm, HEAD_DIM), lambda i: (0, i, 0)),
            pl.BlockSpec((tm, D_MODEL), lambda i: (i, 0)),
            pl.BlockSpec((tm, D_MODEL), lambda i: (i, 3)),
            pl.BlockSpec((tm, D_MODEL), lambda i: (i, 0)),
            _const_spec((D_MODEL, D_MODEL)),
            _const_spec((D_MODEL, D_MODEL)),
            _const_spec((1, D_MODEL)),
            _const_spec((D_MODEL, D_FF)),
            _const_spec((D_FF, D_MODEL)),
            _const_spec((1, D_MODEL)),
        ],
        out_specs=pl.BlockSpec((tm, D_MODEL), lambda i: (i, 0)),
        out_shape=jax.ShapeDtypeStruct((T, D_MODEL), F32),
        compiler_params=pltpu.CompilerParams(
            dimension_semantics=("arbitrary",), vmem_limit_bytes=VMEM_LIMIT),
        name="merge_mlp",
    )(ob, yag, pf, x2, wb, wo, gm, wu, wdn, gf)


def _block_diag_pairs(wa, wx):
    def pair(w):
        w = w.reshape(LRU_BLOCKS // 2, 2, LRU_BW, LRU_BW)
        z = jnp.zeros_like(w[:, 0])
        top = jnp.concatenate([w[:, 0], z], axis=2)
        bot = jnp.concatenate([z, w[:, 1]], axis=2)
        return jnp.concatenate([top, bot], axis=1)
    return jnp.concatenate([pair(wa), pair(wx)], axis=2).astype(BF16)


def kernel(x, norm_mix_g, w_in, conv_w, conv_b, lru_wa, lru_ba, lru_wx, lru_bx, lru_lambda,
           forget_b, w_branch_a, w_branch_b, w_out, norm_mlp_g, w_up, w_down, norm_final_g):
    B, S, D = x.shape
    assert D == D_MODEL
    T = B * S
    tm = min(512, S)
    tq = min(1024, S)
    tk = tq // 2
    assert S % tm == 0 and S % tq == 0 and tm % LANES == 0

    w_main = w_in[:, :7 * D].astype(BF16)
    w_vt = w_in[:, W_IN_V_SLAB * D:(W_IN_V_SLAB + 1) * D].T.astype(BF16)
    w_f = jnp.pad(w_in[:, 7 * D:], ((0, 0), (0, LANES - N_HEADS))).astype(BF16)
    wd = _block_diag_pairs(lru_wa, lru_wx)
    row = lambda v: v.reshape(1, -1).astype(F32)
    fb = jnp.pad(forget_b, (0, LANES - N_HEADS)).reshape(1, LANES)
    e_rows = jnp.arange(F_PARTS * LANES)
    e_tgt = jnp.where(e_rows % LANES < N_HEADS, (e_rows % LANES) * F_PARTS + e_rows // LANES, -1)
    e = (e_tgt[:, None] == jnp.arange(LANES)[None, :]).astype(BF16)

    x2 = x.reshape(T, D)
    pf, qk, vt, fl = _in_proj(x2, row(norm_mix_g), w_main, w_vt, w_f, tm)
    yag, kf, ft = _lru(pf, fl, conv_w, row(conv_b), wd, row(lru_ba), row(lru_bx),
                       row(lru_lambda), fb, w_branch_a.astype(BF16), e, B, S, tm)
    ob = _attn(qk, kf, vt, ft, B, S, tq, tk, hp=2)
    out = _merge_mlp(ob, yag, pf, x2, w_branch_b.astype(BF16), w_out.astype(BF16),
                     row(norm_mlp_g), w_up.astype(BF16), w_down.astype(BF16),
                     row(norm_final_g), tm)
    return out.reshape(B, S, D)
```

```python
import functools
import math

import jax
import jax.numpy as jnp
from jax import lax
from jax.experimental import pallas as pl
from jax.experimental.pallas import tpu as pltpu

D_MODEL = 1024
N_HEADS = 8
HEAD_DIM = 128
LRU_BLOCKS = 16
LRU_BW = 64
CONV_W = 4
LRU_C = 8.0
D_FF = 4 * D_MODEL
RMS_EPS = 1e-6
LANES = 128
SUBLANES = 8
BF16_SUBLANES = 16
LOG2E = 1.4426950408889634
Q_SCALE = LOG2E / math.sqrt(HEAD_DIM)
VMEM_LIMIT = 56 * 1024 * 1024

BF16 = jnp.bfloat16
F32 = jnp.float32
_NT_DIMS = (((1,), (1,)), ((), ()))
W_IN_PF_SLABS = (0, 1, 5, 6)
W_IN_QK_SLABS = (2, 3)
W_IN_V_SLAB = 4
P_UNDERFLOW_LOG2 = 160.0
NORM_MARGIN = 1.02
F_PARTS = 3


def _dot(a, b):
    return jnp.dot(a, b, preferred_element_type=F32)


def _rms(x, g):
    return x * lax.rsqrt(jnp.mean(x * x, axis=-1, keepdims=True) + RMS_EPS) * g


def _softplus(x):
    return jnp.maximum(x, 0.0) + jnp.log1p(jnp.exp(-jnp.abs(x)))


def _sigmoid(x):
    return 0.5 * jnp.tanh(0.5 * x) + 0.5


def _gelu_tanh(x):
    c = math.sqrt(2.0 / math.pi)
    half_x = 0.5 * x
    return half_x + half_x * jnp.tanh(x * (c + (c * 0.044715) * (x * x)))


def _const_spec(shape):
    nd = len(shape)
    return pl.BlockSpec(shape, lambda *_: (0,) * nd, pipeline_mode=pl.Buffered(1))


def _in_proj_kernel(x_ref, g_ref, w_ref, wvt_ref, wf_ref, pf_ref, qk_ref, vt_ref, fl_ref,
                    nrm_ref):
    u = _rms(x_ref[...], g_ref[...]).astype(BF16)
    fl_ref[...] = _dot(u, wf_ref[...])
    post = (None, _gelu_tanh, _sigmoid, _sigmoid)
    for j, src in enumerate(W_IN_PF_SLABS):
        res = _dot(u, w_ref[:, src * D_MODEL:(src + 1) * D_MODEL])
        pf_ref[:, j * D_MODEL:(j + 1) * D_MODEL] = res if post[j] is None else post[j](res)
    for j, src in enumerate(W_IN_QK_SLABS):
        res = _dot(u, w_ref[:, src * D_MODEL:(src + 1) * D_MODEL])
        if j == 0:
            res = res * Q_SCALE
        lane = lax.broadcasted_iota(jnp.int32, (1, LANES), 1)
        nrm = jnp.zeros((1, LANES), F32)
        for h in range(N_HEADS):
            xb = res[:, h * HEAD_DIM:(h + 1) * HEAD_DIM].astype(BF16)
            qk_ref[j, h] = xb
            xf = xb.astype(F32)
            n2 = jnp.max(jnp.sum(xf * xf, axis=1, keepdims=True), axis=0, keepdims=True)
            nrm = jnp.where(lane == h, n2, nrm)
        nrm_ref[j:j + 1, :] = nrm
    vt = lax.dot_general(wvt_ref[...], u, _NT_DIMS, preferred_element_type=F32)
    for h in range(N_HEADS):
        vt_ref[h] = vt[h * HEAD_DIM:(h + 1) * HEAD_DIM, :].astype(BF16)


def _in_proj(x2, g, w_main, w_vt, w_f, tm):
    T = x2.shape[0]
    return pl.pallas_call(
        _in_proj_kernel,
        grid=(T // tm,),
        in_specs=[
            pl.BlockSpec((tm, D_MODEL), lambda i: (i, 0)),
            _const_spec((1, D_MODEL)),
            _const_spec((D_MODEL, 7 * D_MODEL)),
            _const_spec((D_MODEL, D_MODEL)),
            _const_spec((D_MODEL, LANES)),
        ],
        out_specs=[
            pl.BlockSpec((tm, 4 * D_MODEL), lambda i: (i, 0)),
            pl.BlockSpec((2, N_HEADS, tm, HEAD_DIM), lambda i: (0, 0, i, 0)),
            pl.BlockSpec((N_HEADS, HEAD_DIM, tm), lambda i: (0, 0, i)),
            pl.BlockSpec((tm, LANES), lambda i: (i, 0)),
            pl.BlockSpec((None, 2, LANES), lambda i: (i, 0, 0)),
        ],
        out_shape=[
            jax.ShapeDtypeStruct((T, 4 * D_MODEL), F32),
            jax.ShapeDtypeStruct((2, N_HEADS, T, HEAD_DIM), BF16),
            jax.ShapeDtypeStruct((N_HEADS, HEAD_DIM, T), BF16),
            jax.ShapeDtypeStruct((T, LANES), F32),
            jax.ShapeDtypeStruct((T // tm, 2, LANES), F32),
        ],
        compiler_params=pltpu.CompilerParams(
            dimension_semantics=("arbitrary",), vmem_limit_bytes=VMEM_LIMIT),
        name="in_proj",
    )(x2, g, w_main, w_vt, w_f)


def _lru_kernel(xl_ref, gl_ref, ga_ref, fl_ref, cw_ref, cb_ref, wd_ref, ba_ref, bx_ref,
                lam_ref, fb_ref, wa_ref, e_ref,
                yag_ref, kf_ref, ft_ref, fend_ref,
                xq_sc, a_sc, b_sc, h_sc, halo_sc, hc_sc, fc_sc, *, tm):
    s = pl.program_id(1)
    seg = tm // SUBLANES
    pitch = seg + SUBLANES
    gps = seg // SUBLANES
    nc = D_MODEL // LANES
    n_halo = CONV_W - 1

    @pl.when(s == 0)
    def _():
        halo_sc[...] = jnp.zeros_like(halo_sc)
        hc_sc[...] = jnp.zeros_like(hc_sc)
        fc_sc[...] = jnp.zeros_like(fc_sc)

    for g in range(tm // SUBLANES):
        u, j0 = g // gps, (g % gps) * SUBLANES
        for c in range(nc):
            xq_sc[c, pl.ds(j0 * SUBLANES + u, SUBLANES, stride=SUBLANES), :] = (
                xl_ref[g * SUBLANES:(g + 1) * SUBLANES, c * LANES:(c + 1) * LANES])

    sub = lax.broadcasted_iota(jnp.int32, (SUBLANES, LANES), 0)
    c_all = -LRU_C * _softplus(-lam_ref[...])
    for c in range(nc):
        cols = slice(c * LANES, (c + 1) * LANES)
        x = xq_sc[c]
        wrap = []
        for q in range(n_halo):
            rows = slice(q * SUBLANES, (q + 1) * SUBLANES)
            rolled = pltpu.roll(x[tm - n_halo * SUBLANES:, :][rows], 1, axis=0)
            wrap.append(jnp.where(sub == 0, halo_sc[c, rows, :], rolled))
            halo_sc[c, rows, :] = rolled
        xa = cb_ref[:, cols] + cw_ref[CONV_W - 1:CONV_W, cols] * x
        for d in range(1, CONV_W):
            xd = jnp.concatenate(wrap[n_halo - d:] + [x[:tm - d * SUBLANES, :]], axis=0)
            xa = xa + cw_ref[CONV_W - 1 - d:CONV_W - d, cols] * xd
        ri = _dot(xa.astype(BF16), wd_ref[c])
        t_r = jnp.tanh(0.5 * ri[:, :LANES] + 0.5 * ba_ref[:, cols])
        t_i = jnp.tanh(0.5 * ri[:, LANES:] + 0.5 * bx_ref[:, cols])
        half_c = 0.5 * c_all[:, cols]
        a = jnp.exp(half_c * t_r + half_c)
        half_xa = 0.5 * xa
        one_m_a2 = 1.0 - a * a
        mult = one_m_a2 * lax.rsqrt(jnp.maximum(one_m_a2, 1e-30))
        a_sc[c] = a
        b_sc[c] = mult * (half_xa * t_i + half_xa)

    def scan_step(j, carry):
        r0 = pl.multiple_of(j * SUBLANES, SUBLANES)
        out = []
        for c in range(nc):
            h, p = carry[c]
            a = a_sc[c, pl.ds(r0, SUBLANES), :]
            h = a * h + b_sc[c, pl.ds(r0, SUBLANES), :]
            p = a * p
            b_sc[c, pl.ds(r0, SUBLANES), :] = h
            a_sc[c, pl.ds(r0, SUBLANES), :] = p
            out.append((h, p))
        return tuple(out)

    ends = lax.fori_loop(
        0, seg, scan_step,
        tuple((jnp.zeros((SUBLANES, LANES), F32), jnp.ones((SUBLANES, LANES), F32))
              for _ in range(nc)), unroll=SUBLANES)

    carry_in = []
    for c in range(nc):
        cols = slice(c * LANES, (c + 1) * LANES)
        h_end, p_end = ends[c]
        for d in (1, 2, 4):
            keep = sub >= d
            p_sh = jnp.where(keep, pltpu.roll(p_end, d, axis=0), 1.0)
            h_sh = jnp.where(keep, pltpu.roll(h_end, d, axis=0), 0.0)
            h_end = p_end * h_sh + h_end
            p_end = p_end * p_sh
        blk_in = hc_sc[:, cols]
        seg_out = h_end + p_end * blk_in
        carry_in.append(jnp.where(sub == 0, blk_in, pltpu.roll(seg_out, 1, axis=0)))
        hc_sc[:, cols] = jnp.broadcast_to(seg_out[SUBLANES - 1:SUBLANES, :], (SUBLANES, LANES))

    def fix_step(j, _):
        r0 = pl.multiple_of(j * SUBLANES, SUBLANES)
        for c in range(nc):
            h = b_sc[c, pl.ds(r0, SUBLANES), :] + a_sc[c, pl.ds(r0, SUBLANES), :] * carry_in[c]
            h_sc[c, pl.ds(j, SUBLANES, stride=pitch), :] = h
        return 0

    lax.fori_loop(0, seg, fix_step, 0, unroll=SUBLANES)

    h = jnp.concatenate(
        [jnp.concatenate([h_sc[c, u * pitch:u * pitch + seg, :] for u in range(SUBLANES)], axis=0)
         for c in range(nc)], axis=1)
    y = (gl_ref[...] * h).astype(BF16)
    yag_ref[...] = ga_ref[...] * _dot(y, wa_ref[...])

    lf = -_softplus(-(fl_ref[...] + fb_ref[...])) * LOG2E
    rowf = lax.broadcasted_iota(jnp.int32, (tm, LANES), 0)
    d = 1
    while d < tm:
        lf = lf + jnp.where(rowf >= d, pltpu.roll(lf, d, axis=0), 0.0)
        d *= 2
    fblk = lf + fc_sc[...]
    fc_sc[...] = fblk[tm - 1:tm, :]
    ft_ref[...] = fblk.T[0:N_HEADS, :]
    fend_ref[...] = fblk[tm - 1:tm, :]
    hi = (-fblk).astype(BF16)
    r1 = -fblk - hi.astype(F32)
    mid = r1.astype(BF16)
    lo = (r1 - mid.astype(F32)).astype(BF16)
    kf_ref[...] = _dot(jnp.concatenate([hi, mid, lo], axis=1), e_ref[...]).astype(BF16)


def _lru(pf, fl, cw, cb, wd, ba, bx, lam, fb, wa, e, B, S, tm):
    T = B * S
    ns = S // tm
    row_blk = lambda c: pl.BlockSpec((tm, D_MODEL), lambda b, s, c=c: (b * ns + s, c))
    return pl.pallas_call(
        functools.partial(_lru_kernel, tm=tm),
        grid=(B, ns),
        in_specs=[
            row_blk(0), row_blk(1), row_blk(2),
            pl.BlockSpec((tm, LANES), lambda b, s: (b * ns + s, 0)),
            _const_spec((CONV_W, D_MODEL)),
            _const_spec((1, D_MODEL)),
            _const_spec((D_MODEL // LANES, LANES, 2 * LANES)),
            _const_spec((1, D_MODEL)),
            _const_spec((1, D_MODEL)),
            _const_spec((1, D_MODEL)),
            _const_spec((1, LANES)),
            _const_spec((D_MODEL, D_MODEL)),
            _const_spec((F_PARTS * LANES, LANES)),
        ],
        out_specs=[
            pl.BlockSpec((tm, D_MODEL), lambda b, s: (b * ns + s, 0)),
            pl.BlockSpec((tm, LANES), lambda b, s: (b * ns + s, 0)),
            pl.BlockSpec((None, N_HEADS, tm), lambda b, s: (b, 0, s)),
            pl.BlockSpec((None, 1, LANES), lambda b, s: (b * ns + s, 0, 0)),
        ],
        out_shape=[
            jax.ShapeDtypeStruct((T, D_MODEL), F32),
            jax.ShapeDtypeStruct((T, LANES), BF16),
            jax.ShapeDtypeStruct((B, N_HEADS, S), F32),
            jax.ShapeDtypeStruct((T // tm, 1, LANES), F32),
        ],
        scratch_shapes=[
            pltpu.VMEM((D_MODEL // LANES, tm, LANES), F32),
            pltpu.VMEM((D_MODEL // LANES, tm, LANES), F32),
            pltpu.VMEM((D_MODEL // LANES, tm, LANES), F32),
            pltpu.VMEM((D_MODEL // LANES, tm + SUBLANES * SUBLANES, LANES), F32),
            pltpu.VMEM((D_MODEL // LANES, (CONV_W - 1) * SUBLANES, LANES), F32),
            pltpu.VMEM((SUBLANES, D_MODEL), F32),
            pltpu.VMEM((1, LANES), F32),
        ],
        compiler_params=pltpu.CompilerParams(
            dimension_semantics=("arbitrary", "arbitrary"), vmem_limit_bytes=VMEM_LIMIT),
        name="lru",
    )(pf, pf, pf, fl, cw, cb, wd, ba, bx, lam, fb, wa, e)


def _col_reduce(op, z):
    rows, cols = z.shape
    return op(op(z.reshape(SUBLANES, rows // SUBLANES, cols), axis=0), axis=0, keepdims=True)


def _attn_kernel(q_ref, qn_ref, k_ref, kf_ref, vt_ref, ft_ref, nrm_ref, fend_ref, o_ref,
                 z_sc, p_sc, acc_sc, m_sc, zmax_sc, skip_sm, *, tq, tk, hp, nq):
    hg = pl.program_id(1)
    qi = pl.program_id(2)
    n_blk = nrm_ref.shape[0]
    head_lane = lax.broadcasted_iota(jnp.int32, (1, LANES), 1)
    blk = lax.broadcasted_iota(jnp.int32, (n_blk, LANES), 0)

    def dead_pairs(tile):
        q2 = jnp.maximum(nrm_ref[2 * tile, 0:1, :], nrm_ref[2 * tile + 1, 0:1, :])
        k2 = jnp.max(nrm_ref[:, 1, :], axis=0, keepdims=True)
        bound = (2.0 * NORM_MARGIN) * jnp.sqrt(q2 * k2) + fend_ref[jnp.maximum(2 * tile - 1, 0)]
        dead = (bound - fend_ref[:, 0, :] <= -P_UNDERFLOW_LOG2) & (blk < 2 * tile)
        first_live = jnp.min(jnp.where(dead, n_blk, blk), axis=0, keepdims=True)
        mine = (head_lane >= hg * hp) & (head_lane < (hg + 1) * hp)
        return jnp.min(jnp.where(mine, first_live, n_blk)) // 2

    lane = lax.broadcasted_iota(jnp.int32, (tq, LANES), 1)
    upper = slice(tk, tq)
    qa, qa_next, f_t = [], [], []
    for a in range(hp):
        h = hg * hp + a
        ones = jnp.where((lane >= F_PARTS * h) & (lane < F_PARTS * (h + 1)), 1.0, 0.0).astype(BF16)
        qa.append(jnp.concatenate([q_ref[a], ones], axis=1))
        qa_next.append(jnp.concatenate([qn_ref[a], ones], axis=1))
        f_t.append(ft_ref[pl.ds(h, 1), :])

    def qk(a, j, queries):
        c0 = pl.multiple_of(j * tk, tk)
        ka = jnp.concatenate([k_ref[a, pl.ds(c0, tk), :], kf_ref[pl.ds(c0, tk), :]], axis=1)
        return lax.dot_general(ka, queries, _NT_DIMS, preferred_element_type=F32)

    ones_rows = jnp.ones((BF16_SUBLANES, tk), BF16)

    def pv(a, j, p):
        c0 = pl.multiple_of(j * tk, tk)
        vt_aug = jnp.concatenate([vt_ref[a, :, pl.ds(c0, tk)], ones_rows], axis=0)
        return _dot(vt_aug, p)

    def softmax(z, m_row, zmax, f_row, masked):
        if masked:
            kr = lax.broadcasted_iota(jnp.int32, z.shape, 0)
            qc = lax.broadcasted_iota(jnp.int32, z.shape, 1)
            z = jnp.where(kr <= qc, z, -jnp.inf)
            zmax = _col_reduce(jnp.max, z)
        m_new = jnp.maximum(m_row, zmax + f_row)
        return m_new, jnp.exp2(m_row - m_new), jnp.exp2(z - (m_new - f_row)).astype(BF16)

    def trip(j, slot, masked=False):
        zmax_next = []
        for a in range(hp):
            if masked:
                z_sc[1 - slot, a, :, upper] = qk(a, j + 1, qa[a][upper])
            else:
                zn = qk(a, j + 1, qa[a])
                z_sc[1 - slot, a] = zn
                zmax_next.append(_col_reduce(jnp.max, zn))
        pvs = [pv(a, jnp.maximum(j - 1, 0), p_sc[a]) for a in range(hp)]
        for a in range(hp):
            m_new, alpha, p = softmax(z_sc[slot, a], m_sc[a], zmax_sc[a], f_t[a], masked)
            p_sc[a] = p
            m_sc[a] = m_new
            acc_sc[a] = alpha * (acc_sc[a] + pvs[a])
            if not masked:
                zmax_sc[a] = zmax_next[a]

    @pl.when((pl.program_id(0) == 0) & (hg == 0) & (qi == 0))
    def _():
        p_sc[...] = jnp.zeros_like(p_sc)

    @pl.when(qi == 0)
    def _():
        skip_sm[0] = 0
        for a in range(hp):
            z0 = qk(a, 0, qa[a])
            z_sc[0, a] = z0
            zmax_sc[a] = _col_reduce(jnp.max, z0)

    for a in range(hp):
        m_sc[a] = jnp.full((1, tq), -jnp.inf, F32)
        acc_sc[a] = jnp.zeros((HEAD_DIM + BF16_SUBLANES, tq), F32)

    @pl.loop(skip_sm[0], qi)
    def _(i):
        trip(2 * i, 0)
        trip(2 * i + 1, 1)

    first = 2 * qi
    trip(first, 0, masked=True)
    for a in range(hp):
        acc = acc_sc[a] + pv(a, first, p_sc[a])
        _, alpha, p = softmax(z_sc[1, a, :, upper], m_sc[a][:, upper], None, f_t[a][:, upper], True)
        acc_up = alpha * acc[:, upper] + pv(a, first + 1, p)
        out = jnp.concatenate(
            [acc[:HEAD_DIM, :tk] / acc[HEAD_DIM:HEAD_DIM + 1, :tk],
             acc_up[:HEAD_DIM] / acc_up[HEAD_DIM:HEAD_DIM + 1]], axis=1)
        o_ref[a] = out.T.astype(BF16)

    skip_next = dead_pairs(jnp.minimum(qi + 1, nq - 1))
    skip_sm[0] = skip_next
    for a in range(hp):
        zn = qk(a, 2 * skip_next, qa_next[a])
        z_sc[0, a] = zn
        zmax_sc[a] = _col_reduce(jnp.max, zn)


def _attn(qk, kf, vt, ft, nrm, fend, B, S, tq, tk, hp):
    T = B * S
    nq = S // tq
    n_blk = S // tk
    assert tq == 2 * tk and nrm.shape[0] == B * n_blk and fend.shape[0] == B * n_blk
    return pl.pallas_call(
        functools.partial(_attn_kernel, tq=tq, tk=tk, hp=hp, nq=nq),
        grid=(B, N_HEADS // hp, nq),
        in_specs=[
            pl.BlockSpec((None, hp, tq, HEAD_DIM), lambda b, g, i: (0, g, b * nq + i, 0)),
            pl.BlockSpec((None, hp, tq, HEAD_DIM),
                         lambda b, g, i: (0, g, b * nq + jnp.minimum(i + 1, nq - 1), 0)),
            pl.BlockSpec((None, hp, S, HEAD_DIM), lambda b, g, i: (1, g, b, 0)),
            pl.BlockSpec((S, LANES), lambda b, g, i: (b, 0)),
            pl.BlockSpec((hp, HEAD_DIM, S), lambda b, g, i: (g, 0, b)),
            pl.BlockSpec((None, N_HEADS, tq), lambda b, g, i: (b, 0, i)),
            pl.BlockSpec((n_blk, 2, LANES), lambda b, g, i: (b, 0, 0)),
            pl.BlockSpec((n_blk, 1, LANES), lambda b, g, i: (b, 0, 0)),
        ],
        out_specs=pl.BlockSpec((hp, tq, HEAD_DIM), lambda b, g, i: (g, b * nq + i, 0)),
        out_shape=jax.ShapeDtypeStruct((N_HEADS, T, HEAD_DIM), BF16),
        scratch_shapes=[
            pltpu.VMEM((2, hp, tk, tq), F32),
            pltpu.VMEM((hp, tk, tq), BF16),
            pltpu.VMEM((hp, HEAD_DIM + BF16_SUBLANES, tq), F32),
            pltpu.VMEM((hp, 1, tq), F32),
            pltpu.VMEM((hp, 1, tq), F32),
            pltpu.SMEM((1,), jnp.int32),
        ],
        compiler_params=pltpu.CompilerParams(
            dimension_semantics=("arbitrary", "arbitrary", "arbitrary"),
            vmem_limit_bytes=VMEM_LIMIT),
        name="fox_attn",
    )(qk, qk, qk, kf, vt, ft, nrm, fend)


def _merge_mlp_kernel(ob_ref, yag_ref, gb_ref, x_ref, wb_ref, wo_ref, gm_ref, wu_ref,
                      wdn_ref, gf_ref, o_ref):
    ob = jnp.concatenate([ob_ref[h] for h in range(N_HEADS)], axis=1)
    yb = _dot(ob, wb_ref[...])
    mix = yag_ref[...] + gb_ref[...] * yb
    x1 = x_ref[...] + _dot(mix.astype(BF16), wo_ref[...])
    m = _rms(x1, gm_ref[...]).astype(BF16)
    acc = x1
    for c in range(D_FF // D_MODEL):
        cols = slice(c * D_MODEL, (c + 1) * D_MODEL)
        hc = jnp.maximum(_dot(m, wu_ref[:, cols]), 0.0)
        acc = acc + _dot((hc * hc).astype(BF16), wdn_ref[cols, :])
    o_ref[...] = _rms(acc, gf_ref[...])


def _merge_mlp(ob, yag, pf, x2, wb, wo, gm, wu, wdn, gf, tm):
    T = x2.shape[0]
    return pl.pallas_call(
        _merge_mlp_kernel,
        grid=(T // tm,),
        in_specs=[
            pl.BlockSpec((N_HEADS, tm, HEAD_DIM), lambda i: (0, i, 0)),
            pl.BlockSpec((tm, D_MODEL), lambda i: (i, 0)),
            pl.BlockSpec((tm, D_MODEL), lambda i: (i, 3)),
            pl.BlockSpec((tm, D_MODEL), lambda i: (i, 0)),
            _const_spec((D_MODEL, D_MODEL)),
            _const_spec((D_MODEL, D_MODEL)),
            _const_spec((1, D_MODEL)),
            _const_spec((D_MODEL, D_FF)),
            _const_spec((D_FF, D_MODEL)),
            _const_spec((1, D_MODEL)),
        ],
        out_specs=pl.BlockSpec((tm, D_MODEL), lambda i: (i, 0)),
        out_shape=jax.ShapeDtypeStruct((T, D_MODEL), F32),
        compiler_params=pltpu.CompilerParams(
            dimension_semantics=("arbitrary",), vmem_limit_bytes=VMEM_LIMIT),
        name="merge_mlp",
    )(ob, yag, pf, x2, wb, wo, gm, wu, wdn, gf)


def _block_diag_pairs(wa, wx):
    def pair(w):
        w = w.reshape(LRU_BLOCKS // 2, 2, LRU_BW, LRU_BW)
        z = jnp.zeros_like(w[:, 0])
        top = jnp.concatenate([w[:, 0], z], axis=2)
        bot = jnp.concatenate([z, w[:, 1]], axis=2)
        return jnp.concatenate([top, bot], axis=1)
    return jnp.concatenate([pair(wa), pair(wx)], axis=2).astype(BF16)


def kernel(x, norm_mix_g, w_in, conv_w, conv_b, lru_wa, lru_ba, lru_wx, lru_bx, lru_lambda,
           forget_b, w_branch_a, w_branch_b, w_out, norm_mlp_g, w_up, w_down, norm_final_g):
    B, S, D = x.shape
    assert D == D_MODEL
    T = B * S
    tm = min(512, S)
    tq = min(1024, S)
    tk = tq // 2
    assert S % tm == 0 and S % tq == 0 and tm % LANES == 0

    w_main = w_in[:, :7 * D].astype(BF16)
    w_vt = w_in[:, W_IN_V_SLAB * D:(W_IN_V_SLAB + 1) * D].T.astype(BF16)
    w_f = jnp.pad(w_in[:, 7 * D:], ((0, 0), (0, LANES - N_HEADS))).astype(BF16)
    wd = _block_diag_pairs(lru_wa, lru_wx)
    row = lambda v: v.reshape(1, -1).astype(F32)
    fb = jnp.pad(forget_b, (0, LANES - N_HEADS)).reshape(1, LANES)
    e_rows = jnp.arange(F_PARTS * LANES)
    e_tgt = jnp.where(e_rows % LANES < N_HEADS, (e_rows % LANES) * F_PARTS + e_rows // LANES, -1)
    e = (e_tgt[:, None] == jnp.arange(LANES)[None, :]).astype(BF16)

    x2 = x.reshape(T, D)
    pf, qk, vt, fl, nrm = _in_proj(x2, row(norm_mix_g), w_main, w_vt, w_f, tm)
    yag, kf, ft, fend = _lru(pf, fl, conv_w, row(conv_b), wd, row(lru_ba), row(lru_bx),
                             row(lru_lambda), fb, w_branch_a.astype(BF16), e, B, S, tm)
    assert tk == tm
    ob = _attn(qk, kf, vt, ft, nrm, fend, B, S, tq, tk, hp=2)
    out = _merge_mlp(ob, yag, pf, x2, w_branch_b.astype(BF16), w_out.astype(BF16),
                     row(norm_mlp_g), w_up.astype(BF16), w_down.astype(BF16),
                     row(norm_final_g), tm)
    return out.reshape(B, S, D)
```

```python
import functools
import math

import jax
import jax.numpy as jnp
from jax import lax
from jax.experimental import pallas as pl
from jax.experimental.pallas import tpu as pltpu

D_MODEL = 1024
N_HEADS = 8
HEAD_DIM = 128
LRU_BLOCKS = 16
LRU_BW = 64
CONV_W = 4
LRU_C = 8.0
D_FF = 4 * D_MODEL
RMS_EPS = 1e-6
LANES = 128
SUBLANES = 8
BF16_SUBLANES = 16
LOG2E = 1.4426950408889634
Q_SCALE = LOG2E / math.sqrt(HEAD_DIM)
VMEM_LIMIT = 56 * 1024 * 1024

BF16 = jnp.bfloat16
F32 = jnp.float32
_NT_DIMS = (((1,), (1,)), ((), ()))
W_IN_PF_SLABS = (0, 1, 5, 6)
W_IN_QK_SLABS = (2, 3)
W_IN_V_SLAB = 4
P_UNDERFLOW_LOG2 = 160.0
NORM_MARGIN = 1.02
F_PARTS = 3


def _dot(a, b):
    return jnp.dot(a, b, preferred_element_type=F32)


def _rms(x, g):
    return x * lax.rsqrt(jnp.mean(x * x, axis=-1, keepdims=True) + RMS_EPS) * g


def _softplus(x):
    return jnp.maximum(x, 0.0) + jnp.log1p(jnp.exp(-jnp.abs(x)))


def _sigmoid(x):
    return 0.5 * jnp.tanh(0.5 * x) + 0.5


def _gelu_tanh(x):
    c = math.sqrt(2.0 / math.pi)
    half_x = 0.5 * x
    return half_x + half_x * jnp.tanh(x * (c + (c * 0.044715) * (x * x)))


def _const_spec(shape):
    nd = len(shape)
    return pl.BlockSpec(shape, lambda *_: (0,) * nd, pipeline_mode=pl.Buffered(1))


def _in_proj_kernel(x_ref, g_ref, w_ref, wvt_ref, wf_ref, pf_ref, qk_ref, vt_ref, fl_ref,
                    nrm_ref):
    u = _rms(x_ref[...], g_ref[...]).astype(BF16)
    fl_ref[...] = _dot(u, wf_ref[...])
    post = (None, _gelu_tanh, _sigmoid, _sigmoid)
    for j, src in enumerate(W_IN_PF_SLABS):
        res = _dot(u, w_ref[:, src * D_MODEL:(src + 1) * D_MODEL])
        pf_ref[:, j * D_MODEL:(j + 1) * D_MODEL] = res if post[j] is None else post[j](res)
    for j, src in enumerate(W_IN_QK_SLABS):
        res = _dot(u, w_ref[:, src * D_MODEL:(src + 1) * D_MODEL])
        if j == 0:
            res = res * Q_SCALE
        lane = lax.broadcasted_iota(jnp.int32, (1, LANES), 1)
        nrm = jnp.zeros((1, LANES), F32)
        for h in range(N_HEADS):
            xb = res[:, h * HEAD_DIM:(h + 1) * HEAD_DIM].astype(BF16)
            qk_ref[j, h] = xb
            xf = xb.astype(F32)
            n2 = jnp.max(jnp.sum(xf * xf, axis=1, keepdims=True), axis=0, keepdims=True)
            nrm = jnp.where(lane == h, n2, nrm)
        nrm_ref[j:j + 1, :] = nrm
    vt = lax.dot_general(wvt_ref[...], u, _NT_DIMS, preferred_element_type=F32)
    for h in range(N_HEADS):
        vt_ref[h] = vt[h * HEAD_DIM:(h + 1) * HEAD_DIM, :].astype(BF16)


def _in_proj(x2, g, w_main, w_vt, w_f, tm):
    T = x2.shape[0]
    return pl.pallas_call(
        _in_proj_kernel,
        grid=(T // tm,),
        in_specs=[
            pl.BlockSpec((tm, D_MODEL), lambda i: (i, 0)),
            _const_spec((1, D_MODEL)),
            _const_spec(w_main.shape),
            _const_spec((D_MODEL, D_MODEL)),
            _const_spec((D_MODEL, LANES)),
        ],
        out_specs=[
            pl.BlockSpec((tm, 4 * D_MODEL), lambda i: (i, 0)),
            pl.BlockSpec((2, N_HEADS, tm, HEAD_DIM), lambda i: (0, 0, i, 0)),
            pl.BlockSpec((N_HEADS, HEAD_DIM, tm), lambda i: (0, 0, i)),
            pl.BlockSpec((tm, LANES), lambda i: (i, 0)),
            pl.BlockSpec((None, 2, LANES), lambda i: (i, 0, 0)),
        ],
        out_shape=[
            jax.ShapeDtypeStruct((T, 4 * D_MODEL), F32),
            jax.ShapeDtypeStruct((2, N_HEADS, T, HEAD_DIM), BF16),
            jax.ShapeDtypeStruct((N_HEADS, HEAD_DIM, T), BF16),
            jax.ShapeDtypeStruct((T, LANES), F32),
            jax.ShapeDtypeStruct((T // tm, 2, LANES), F32),
        ],
        compiler_params=pltpu.CompilerParams(
            dimension_semantics=("arbitrary",), vmem_limit_bytes=VMEM_LIMIT),
        name="in_proj",
    )(x2, g, w_main, w_vt, w_f)


def _lru_kernel(xl_ref, gl_ref, ga_ref, fl_ref, cw_ref, cb_ref, wd_ref, ba_ref, bx_ref,
                lam_ref, fb_ref, wa_ref, e_ref,
                yag_ref, kf_ref, ft_ref, fend_ref,
                xq_sc, a_sc, b_sc, h_sc, halo_sc, hc_sc, fc_sc, *, tm):
    s = pl.program_id(1)
    seg = tm // SUBLANES
    pitch = seg + SUBLANES
    gps = seg // SUBLANES
    nc = D_MODEL // LANES
    n_halo = CONV_W - 1

    @pl.when(s == 0)
    def _():
        halo_sc[...] = jnp.zeros_like(halo_sc)
        hc_sc[...] = jnp.zeros_like(hc_sc)
        fc_sc[...] = jnp.zeros_like(fc_sc)

    for g in range(tm // SUBLANES):
        u, j0 = g // gps, (g % gps) * SUBLANES
        for c in range(nc):
            xq_sc[c, pl.ds(j0 * SUBLANES + u, SUBLANES, stride=SUBLANES), :] = (
                xl_ref[g * SUBLANES:(g + 1) * SUBLANES, c * LANES:(c + 1) * LANES])

    sub = lax.broadcasted_iota(jnp.int32, (SUBLANES, LANES), 0)
    c_all = -LRU_C * _softplus(-lam_ref[...])
    for c in range(nc):
        cols = slice(c * LANES, (c + 1) * LANES)
        x = xq_sc[c]
        wrap = []
        for q in range(n_halo):
            rows = slice(q * SUBLANES, (q + 1) * SUBLANES)
            rolled = pltpu.roll(x[tm - n_halo * SUBLANES:, :][rows], 1, axis=0)
            wrap.append(jnp.where(sub == 0, halo_sc[c, rows, :], rolled))
            halo_sc[c, rows, :] = rolled
        xa = cb_ref[:, cols] + cw_ref[CONV_W - 1:CONV_W, cols] * x
        for d in range(1, CONV_W):
            xd = jnp.concatenate(wrap[n_halo - d:] + [x[:tm - d * SUBLANES, :]], axis=0)
            xa = xa + cw_ref[CONV_W - 1 - d:CONV_W - d, cols] * xd
        ri = _dot(xa.astype(BF16), wd_ref[c])
        t_r = jnp.tanh(0.5 * ri[:, :LANES] + 0.5 * ba_ref[:, cols])
        t_i = jnp.tanh(0.5 * ri[:, LANES:] + 0.5 * bx_ref[:, cols])
        half_c = 0.5 * c_all[:, cols]
        a = jnp.exp(half_c * t_r + half_c)
        half_xa = 0.5 * xa
        one_m_a2 = 1.0 - a * a
        mult = one_m_a2 * lax.rsqrt(jnp.maximum(one_m_a2, 1e-30))
        a_sc[c] = a
        b_sc[c] = mult * (half_xa * t_i + half_xa)

    def scan_step(j, carry):
        r0 = pl.multiple_of(j * SUBLANES, SUBLANES)
        out = []
        for c in range(nc):
            h, p = carry[c]
            a = a_sc[c, pl.ds(r0, SUBLANES), :]
            h = a * h + b_sc[c, pl.ds(r0, SUBLANES), :]
            p = a * p
            b_sc[c, pl.ds(r0, SUBLANES), :] = h
            a_sc[c, pl.ds(r0, SUBLANES), :] = p
            out.append((h, p))
        return tuple(out)

    ends = lax.fori_loop(
        0, seg, scan_step,
        tuple((jnp.zeros((SUBLANES, LANES), F32), jnp.ones((SUBLANES, LANES), F32))
              for _ in range(nc)), unroll=SUBLANES)

    carry_in = []
    for c in range(nc):
        cols = slice(c * LANES, (c + 1) * LANES)
        h_end, p_end = ends[c]
        for d in (1, 2, 4):
            keep = sub >= d
            p_sh = jnp.where(keep, pltpu.roll(p_end, d, axis=0), 1.0)
            h_sh = jnp.where(keep, pltpu.roll(h_end, d, axis=0), 0.0)
            h_end = p_end * h_sh + h_end
            p_end = p_end * p_sh
        blk_in = hc_sc[:, cols]
        seg_out = h_end + p_end * blk_in
        carry_in.append(jnp.where(sub == 0, blk_in, pltpu.roll(seg_out, 1, axis=0)))
        hc_sc[:, cols] = jnp.broadcast_to(seg_out[SUBLANES - 1:SUBLANES, :], (SUBLANES, LANES))

    def fix_step(j, _):
        r0 = pl.multiple_of(j * SUBLANES, SUBLANES)
        for c in range(nc):
            h = b_sc[c, pl.ds(r0, SUBLANES), :] + a_sc[c, pl.ds(r0, SUBLANES), :] * carry_in[c]
            h_sc[c, pl.ds(j, SUBLANES, stride=pitch), :] = h
        return 0

    lax.fori_loop(0, seg, fix_step, 0, unroll=SUBLANES)

    h = jnp.concatenate(
        [jnp.concatenate([h_sc[c, u * pitch:u * pitch + seg, :] for u in range(SUBLANES)], axis=0)
         for c in range(nc)], axis=1)
    y = (gl_ref[...] * h).astype(BF16)
    yag_ref[...] = ga_ref[...] * _dot(y, wa_ref[...])

    lf = -_softplus(-(fl_ref[...] + fb_ref[...])) * LOG2E
    rowf = lax.broadcasted_iota(jnp.int32, (tm, LANES), 0)
    d = 1
    while d < tm:
        lf = lf + jnp.where(rowf >= d, pltpu.roll(lf, d, axis=0), 0.0)
        d *= 2
    fblk = lf + fc_sc[...]
    fc_sc[...] = fblk[tm - 1:tm, :]
    ft_ref[...] = fblk.T[0:N_HEADS, :]
    fend_ref[...] = fblk[tm - 1:tm, :]
    hi = (-fblk).astype(BF16)
    r1 = -fblk - hi.astype(F32)
    mid = r1.astype(BF16)
    lo = (r1 - mid.astype(F32)).astype(BF16)
    kf_ref[...] = _dot(jnp.concatenate([hi, mid, lo], axis=1), e_ref[...]).astype(BF16)


def _lru(pf, fl, cw, cb, wd, ba, bx, lam, fb, wa, e, B, S, tm):
    T = B * S
    ns = S // tm
    row_blk = lambda c: pl.BlockSpec((tm, D_MODEL), lambda b, s, c=c: (b * ns + s, c))
    return pl.pallas_call(
        functools.partial(_lru_kernel, tm=tm),
        grid=(B, ns),
        in_specs=[
            row_blk(0), row_blk(1), row_blk(2),
            pl.BlockSpec((tm, LANES), lambda b, s: (b * ns + s, 0)),
            _const_spec((CONV_W, D_MODEL)),
            _const_spec((1, D_MODEL)),
            _const_spec((D_MODEL // LANES, LANES, 2 * LANES)),
            _const_spec((1, D_MODEL)),
            _const_spec((1, D_MODEL)),
            _const_spec((1, D_MODEL)),
            _const_spec((1, LANES)),
            _const_spec((D_MODEL, D_MODEL)),
            _const_spec((F_PARTS * LANES, LANES)),
        ],
        out_specs=[
            pl.BlockSpec((tm, D_MODEL), lambda b, s: (b * ns + s, 0)),
            pl.BlockSpec((tm, LANES), lambda b, s: (b * ns + s, 0)),
            pl.BlockSpec((None, N_HEADS, tm), lambda b, s: (b, 0, s)),
            pl.BlockSpec((None, 1, LANES), lambda b, s: (b * ns + s, 0, 0)),
        ],
        out_shape=[
            jax.ShapeDtypeStruct((T, D_MODEL), F32),
            jax.ShapeDtypeStruct((T, LANES), BF16),
            jax.ShapeDtypeStruct((B, N_HEADS, S), F32),
            jax.ShapeDtypeStruct((T // tm, 1, LANES), F32),
        ],
        scratch_shapes=[
            pltpu.VMEM((D_MODEL // LANES, tm, LANES), F32),
            pltpu.VMEM((D_MODEL // LANES, tm, LANES), F32),
            pltpu.VMEM((D_MODEL // LANES, tm, LANES), F32),
            pltpu.VMEM((D_MODEL // LANES, tm + SUBLANES * SUBLANES, LANES), F32),
            pltpu.VMEM((D_MODEL // LANES, (CONV_W - 1) * SUBLANES, LANES), F32),
            pltpu.VMEM((SUBLANES, D_MODEL), F32),
            pltpu.VMEM((1, LANES), F32),
        ],
        compiler_params=pltpu.CompilerParams(
            dimension_semantics=("arbitrary", "arbitrary"), vmem_limit_bytes=VMEM_LIMIT),
        name="lru",
    )(pf, pf, pf, fl, cw, cb, wd, ba, bx, lam, fb, wa, e)


def _col_reduce(op, z):
    rows, cols = z.shape
    return op(op(z.reshape(SUBLANES, rows // SUBLANES, cols), axis=0), axis=0, keepdims=True)


def _attn_kernel(q_ref, qn_ref, k_ref, kf_ref, vt_ref, ft_ref, nrm_ref, fend_ref, o_ref,
                 z_sc, p_sc, acc_sc, m_sc, zmax_sc, skip_sm, *, tq, tk, hp, nq):
    hg = pl.program_id(1)
    qi = pl.program_id(2)
    n_blk = nrm_ref.shape[0]
    head_lane = lax.broadcasted_iota(jnp.int32, (1, LANES), 1)
    blk = lax.broadcasted_iota(jnp.int32, (n_blk, LANES), 0)

    def first_live_block(tile):
        q2 = jnp.maximum(nrm_ref[2 * tile, 0:1, :], nrm_ref[2 * tile + 1, 0:1, :])
        k2 = jnp.max(nrm_ref[:, 1, :], axis=0, keepdims=True)
        bound = (2.0 * NORM_MARGIN) * jnp.sqrt(q2 * k2) + fend_ref[jnp.maximum(2 * tile - 1, 0)]
        dead = (bound - fend_ref[:, 0, :] <= -P_UNDERFLOW_LOG2) & (blk < 2 * tile)
        first_live = jnp.min(jnp.where(dead, n_blk, blk), axis=0, keepdims=True)
        mine = (head_lane >= hg * hp) & (head_lane < (hg + 1) * hp)
        return jnp.min(jnp.where(mine, first_live, n_blk))

    lane = lax.broadcasted_iota(jnp.int32, (tq, LANES), 1)
    upper = slice(tk, tq)
    qa, qa_next, f_t = [], [], []
    for a in range(hp):
        h = hg * hp + a
        ones = jnp.where((lane >= F_PARTS * h) & (lane < F_PARTS * (h + 1)), 1.0, 0.0).astype(BF16)
        qa.append(jnp.concatenate([q_ref[a], ones], axis=1))
        qa_next.append(jnp.concatenate([qn_ref[a], ones], axis=1))
        f_t.append(ft_ref[pl.ds(h, 1), :])

    def qk(a, j, queries):
        c0 = pl.multiple_of(j * tk, tk)
        ka = jnp.concatenate([k_ref[a, pl.ds(c0, tk), :], kf_ref[pl.ds(c0, tk), :]], axis=1)
        return lax.dot_general(ka, queries, _NT_DIMS, preferred_element_type=F32)

    ones_rows = jnp.ones((BF16_SUBLANES, tk), BF16)

    def pv(a, j, p):
        c0 = pl.multiple_of(j * tk, tk)
        vt_aug = jnp.concatenate([vt_ref[a, :, pl.ds(c0, tk)], ones_rows], axis=0)
        return _dot(vt_aug, p)

    def softmax(z, m_row, zmax, f_row, masked):
        if masked:
            kr = lax.broadcasted_iota(jnp.int32, z.shape, 0)
            qc = lax.broadcasted_iota(jnp.int32, z.shape, 1)
            z = jnp.where(kr <= qc, z, -jnp.inf)
            zmax = _col_reduce(jnp.max, z)
        m_new = jnp.maximum(m_row, zmax + f_row)
        return m_new, jnp.exp2(m_row - m_new), jnp.exp2(z - (m_new - f_row)).astype(BF16)

    def trip(j, slot, masked=False):
        zmax_next = []
        for a in range(hp):
            if masked:
                z_sc[1 - slot, a, :, upper] = qk(a, j + 1, qa[a][upper])
            else:
                zn = qk(a, j + 1, qa[a])
                z_sc[1 - slot, a] = zn
                zmax_next.append(_col_reduce(jnp.max, zn))
        pvs = [pv(a, jnp.maximum(j - 1, 0), p_sc[a]) for a in range(hp)]
        for a in range(hp):
            m_new, alpha, p = softmax(z_sc[slot, a], m_sc[a], zmax_sc[a], f_t[a], masked)
            p_sc[a] = p
            m_sc[a] = m_new
            acc_sc[a] = alpha * (acc_sc[a] + pvs[a])
            if not masked:
                zmax_sc[a] = zmax_next[a]

    @pl.when((pl.program_id(0) == 0) & (hg == 0) & (qi == 0))
    def _():
        p_sc[...] = jnp.zeros_like(p_sc)

    @pl.when(qi == 0)
    def _():
        skip_sm[0] = 0
        for a in range(hp):
            z0 = qk(a, 0, qa[a])
            z_sc[0, a] = z0
            zmax_sc[a] = _col_reduce(jnp.max, z0)

    for a in range(hp):
        m_sc[a] = jnp.full((1, tq), -jnp.inf, F32)
        acc_sc[a] = jnp.zeros((HEAD_DIM + BF16_SUBLANES, tq), F32)

    start = skip_sm[0]

    @pl.when((start & 1) == 1)
    def _():
        trip(start, 1)

    @pl.loop((start + 1) >> 1, qi)
    def _(i):
        trip(2 * i, 0)
        trip(2 * i + 1, 1)

    first = 2 * qi
    trip(first, 0, masked=True)
    for a in range(hp):
        acc = acc_sc[a] + pv(a, first, p_sc[a])
        _, alpha, p = softmax(z_sc[1, a, :, upper], m_sc[a][:, upper], None, f_t[a][:, upper], True)
        acc_up = alpha * acc[:, upper] + pv(a, first + 1, p)
        out = jnp.concatenate(
            [acc[:HEAD_DIM, :tk] / acc[HEAD_DIM:HEAD_DIM + 1, :tk],
             acc_up[:HEAD_DIM] / acc_up[HEAD_DIM:HEAD_DIM + 1]], axis=1)
        o_ref[a] = out.T.astype(BF16)

    start_next = first_live_block(jnp.minimum(qi + 1, nq - 1))
    skip_sm[0] = start_next
    for a in range(hp):
        zn = qk(a, start_next, qa_next[a])
        z_sc[start_next & 1, a] = zn
        zmax_sc[a] = _col_reduce(jnp.max, zn)


def _attn(qk, kf, vt, ft, nrm, fend, B, S, tq, tk, hp):
    T = B * S
    nq = S // tq
    n_blk = S // tk
    assert tq == 2 * tk and nrm.shape[0] == B * n_blk and fend.shape[0] == B * n_blk
    return pl.pallas_call(
        functools.partial(_attn_kernel, tq=tq, tk=tk, hp=hp, nq=nq),
        grid=(B, N_HEADS // hp, nq),
        in_specs=[
            pl.BlockSpec((None, hp, tq, HEAD_DIM), lambda b, g, i: (0, g, b * nq + i, 0)),
            pl.BlockSpec((None, hp, tq, HEAD_DIM),
                         lambda b, g, i: (0, g, b * nq + jnp.minimum(i + 1, nq - 1), 0)),
            pl.BlockSpec((None, hp, S, HEAD_DIM), lambda b, g, i: (1, g, b, 0)),
            pl.BlockSpec((S, LANES), lambda b, g, i: (b, 0)),
            pl.BlockSpec((hp, HEAD_DIM, S), lambda b, g, i: (g, 0, b)),
            pl.BlockSpec((None, N_HEADS, tq), lambda b, g, i: (b, 0, i)),
            pl.BlockSpec((n_blk, 2, LANES), lambda b, g, i: (b, 0, 0)),
            pl.BlockSpec((n_blk, 1, LANES), lambda b, g, i: (b, 0, 0)),
        ],
        out_specs=pl.BlockSpec((hp, tq, HEAD_DIM), lambda b, g, i: (g, b * nq + i, 0)),
        out_shape=jax.ShapeDtypeStruct((N_HEADS, T, HEAD_DIM), BF16),
        scratch_shapes=[
            pltpu.VMEM((2, hp, tk, tq), F32),
            pltpu.VMEM((hp, tk, tq), BF16),
            pltpu.VMEM((hp, HEAD_DIM + BF16_SUBLANES, tq), F32),
            pltpu.VMEM((hp, 1, tq), F32),
            pltpu.VMEM((hp, 1, tq), F32),
            pltpu.SMEM((1,), jnp.int32),
        ],
        compiler_params=pltpu.CompilerParams(
            dimension_semantics=("arbitrary", "arbitrary", "arbitrary"),
            vmem_limit_bytes=VMEM_LIMIT),
        name="fox_attn",
    )(qk, qk, qk, kf, vt, ft, nrm, fend)


def _merge_mlp_kernel(ob_ref, yag_ref, gb_ref, x_ref, wb_ref, wo_ref, gm_ref, wu_ref,
                      wdn_ref, gf_ref, o_ref):
    ob = jnp.concatenate([ob_ref[h] for h in range(N_HEADS)], axis=1)
    yb = _dot(ob, wb_ref[...])
    mix = yag_ref[...] + gb_ref[...] * yb
    x1 = x_ref[...] + _dot(mix.astype(BF16), wo_ref[...])
    m = _rms(x1, gm_ref[...]).astype(BF16)
    acc = x1
    for c in range(D_FF // D_MODEL):
        cols = slice(c * D_MODEL, (c + 1) * D_MODEL)
        hc = jnp.maximum(_dot(m, wu_ref[:, cols]), 0.0)
        acc = acc + _dot((hc * hc).astype(BF16), wdn_ref[cols, :])
    o_ref[...] = _rms(acc, gf_ref[...])


def _merge_mlp(ob, yag, pf, x2, wb, wo, gm, wu, wdn, gf, tm):
    T = x2.shape[0]
    return pl.pallas_call(
        _merge_mlp_kernel,
        grid=(T // tm,),
        in_specs=[
            pl.BlockSpec((N_HEADS, tm, HEAD_DIM), lambda i: (0, i, 0)),
            pl.BlockSpec((tm, D_MODEL), lambda i: (i, 0)),
            pl.BlockSpec((tm, D_MODEL), lambda i: (i, 3)),
            pl.BlockSpec((tm, D_MODEL), lambda i: (i, 0)),
            _const_spec((D_MODEL, D_MODEL)),
            _const_spec((D_MODEL, D_MODEL)),
            _const_spec((1, D_MODEL)),
            _const_spec((D_MODEL, D_FF)),
            _const_spec((D_FF, D_MODEL)),
            _const_spec((1, D_MODEL)),
        ],
        out_specs=pl.BlockSpec((tm, D_MODEL), lambda i: (i, 0)),
        out_shape=jax.ShapeDtypeStruct((T, D_MODEL), F32),
        compiler_params=pltpu.CompilerParams(
            dimension_semantics=("arbitrary",), vmem_limit_bytes=VMEM_LIMIT),
        name="merge_mlp",
    )(ob, yag, pf, x2, wb, wo, gm, wu, wdn, gf)


def _block_diag_pairs(wa, wx):
    def pair(w):
        w = w.reshape(LRU_BLOCKS // 2, 2, LRU_BW, LRU_BW)
        z = jnp.zeros_like(w[:, 0])
        top = jnp.concatenate([w[:, 0], z], axis=2)
        bot = jnp.concatenate([z, w[:, 1]], axis=2)
        return jnp.concatenate([top, bot], axis=1)
    return jnp.concatenate([pair(wa), pair(wx)], axis=2).astype(BF16)


def kernel(x, norm_mix_g, w_in, conv_w, conv_b, lru_wa, lru_ba, lru_wx, lru_bx, lru_lambda,
           forget_b, w_branch_a, w_branch_b, w_out, norm_mlp_g, w_up, w_down, norm_final_g):
    B, S, D = x.shape
    assert D == D_MODEL
    T = B * S
    tm = min(512, S)
    tq = min(1024, S)
    tk = tq // 2
    assert S % tm == 0 and S % tq == 0 and tm % LANES == 0

    w_main = w_in.astype(BF16)
    w_vt = w_in[:, W_IN_V_SLAB * D:(W_IN_V_SLAB + 1) * D].T.astype(BF16)
    w_f = jnp.pad(w_in[:, 7 * D:], ((0, 0), (0, LANES - N_HEADS))).astype(BF16)
    wd = _block_diag_pairs(lru_wa, lru_wx)
    row = lambda v: v.reshape(1, -1).astype(F32)
    fb = jnp.pad(forget_b, (0, LANES - N_HEADS)).reshape(1, LANES)
    e_rows = jnp.arange(F_PARTS * LANES)
    e_tgt = jnp.where(e_rows % LANES < N_HEADS, (e_rows % LANES) * F_PARTS + e_rows // LANES, -1)
    e = (e_tgt[:, None] == jnp.arange(LANES)[None, :]).astype(BF16)

    x2 = x.reshape(T, D)
    pf, qk, vt, fl, nrm = _in_proj(x2, row(norm_mix_g), w_main, w_vt, w_f, tm)
    yag, kf, ft, fend = _lru(pf, fl, conv_w, row(conv_b), wd, row(lru_ba), row(lru_bx),
                             row(lru_lambda), fb, w_branch_a.astype(BF16), e, B, S, tm)
    assert tk == tm
    ob = _attn(qk, kf, vt, ft, nrm, fend, B, S, tq, tk, hp=2)
    out = _merge_mlp(ob, yag, pf, x2, w_branch_b.astype(BF16), w_out.astype(BF16),
                     row(norm_mlp_g), w_up.astype(BF16), w_down.astype(BF16),
                     row(norm_final_g), tm)
    return out.reshape(B, S, D)
```

```python
import functools
import math

import jax
import jax.numpy as jnp
from jax import lax
from jax.experimental import pallas as pl
from jax.experimental.pallas import tpu as pltpu

D_MODEL = 1024
N_HEADS = 8
HEAD_DIM = 128
LRU_BLOCKS = 16
LRU_BW = 64
CONV_W = 4
LRU_C = 8.0
D_FF = 4 * D_MODEL
RMS_EPS = 1e-6
LANES = 128
SUBLANES = 8
BF16_SUBLANES = 16
LOG2E = 1.4426950408889634
Q_SCALE = LOG2E / math.sqrt(HEAD_DIM)
VMEM_LIMIT = 56 * 1024 * 1024

BF16 = jnp.bfloat16
F32 = jnp.float32
_NT_DIMS = (((1,), (1,)), ((), ()))
W_IN_PF_SLABS = (0, 1, 5, 6)
W_IN_QK_SLABS = (2, 3)
W_IN_V_SLAB = 4
P_UNDERFLOW_LOG2 = 160.0
NORM_MARGIN = 1.02
F_PARTS = 3


def _dot(a, b):
    return jnp.dot(a, b, preferred_element_type=F32)


def _rms(x, g):
    return x * lax.rsqrt(jnp.mean(x * x, axis=-1, keepdims=True) + RMS_EPS) * g


def _softplus(x):
    return jnp.maximum(x, 0.0) + jnp.log1p(jnp.exp(-jnp.abs(x)))


def _sigmoid(x):
    return 0.5 * jnp.tanh(0.5 * x) + 0.5


def _gelu_tanh(x):
    c = math.sqrt(2.0 / math.pi)
    half_x = 0.5 * x
    return half_x + half_x * jnp.tanh(x * (c + (c * 0.044715) * (x * x)))


def _const_spec(shape):
    nd = len(shape)
    return pl.BlockSpec(shape, lambda *_: (0,) * nd, pipeline_mode=pl.Buffered(1))


def _in_proj_kernel(x_ref, xn_ref, g_ref, w_ref, wvt_ref, wf_ref, pf_ref, qk_ref, vt_ref, fl_ref,
                    nrm_ref, u_sc):
    @pl.when(pl.program_id(0) == 0)
    def _():
        u_sc[...] = _rms(x_ref[...], g_ref[...]).astype(BF16)

    u = u_sc[...]
    fl_ref[...] = _dot(u, wf_ref[...])
    post = (None, _gelu_tanh, _sigmoid, _sigmoid)
    for j, src in enumerate(W_IN_PF_SLABS):
        res = _dot(u, w_ref[:, src * D_MODEL:(src + 1) * D_MODEL])
        pf_ref[:, j * D_MODEL:(j + 1) * D_MODEL] = res if post[j] is None else post[j](res)
    for j, src in enumerate(W_IN_QK_SLABS):
        res = _dot(u, w_ref[:, src * D_MODEL:(src + 1) * D_MODEL])
        if j == 0:
            res = res * Q_SCALE
        lane = lax.broadcasted_iota(jnp.int32, (1, LANES), 1)
        nrm = jnp.zeros((1, LANES), F32)
        for h in range(N_HEADS):
            xb = res[:, h * HEAD_DIM:(h + 1) * HEAD_DIM].astype(BF16)
            qk_ref[j, h] = xb
            xf = xb.astype(F32)
            n2 = jnp.max(jnp.sum(xf * xf, axis=1, keepdims=True), axis=0, keepdims=True)
            nrm = jnp.where(lane == h, n2, nrm)
        nrm_ref[j:j + 1, :] = nrm
    vt = lax.dot_general(wvt_ref[...], u, _NT_DIMS, preferred_element_type=F32)
    for h in range(N_HEADS):
        vt_ref[h] = vt[h * HEAD_DIM:(h + 1) * HEAD_DIM, :].astype(BF16)
    u_sc[...] = _rms(xn_ref[...], g_ref[...]).astype(BF16)


def _in_proj(x2, g, w_main, w_vt, w_f, tm):
    T = x2.shape[0]
    n_tiles = T // tm
    return pl.pallas_call(
        _in_proj_kernel,
        grid=(n_tiles,),
        in_specs=[
            pl.BlockSpec((tm, D_MODEL), lambda i: (i, 0)),
            pl.BlockSpec((tm, D_MODEL), lambda i: (jnp.minimum(i + 1, n_tiles - 1), 0)),
            _const_spec((1, D_MODEL)),
            _const_spec(w_main.shape),
            _const_spec((D_MODEL, D_MODEL)),
            _const_spec((D_MODEL, LANES)),
        ],
        out_specs=[
            pl.BlockSpec((tm, 4 * D_MODEL), lambda i: (i, 0)),
            pl.BlockSpec((2, N_HEADS, tm, HEAD_DIM), lambda i: (0, 0, i, 0)),
            pl.BlockSpec((N_HEADS, HEAD_DIM, tm), lambda i: (0, 0, i)),
            pl.BlockSpec((tm, LANES), lambda i: (i, 0)),
            pl.BlockSpec((None, 2, LANES), lambda i: (i, 0, 0)),
        ],
        out_shape=[
            jax.ShapeDtypeStruct((T, 4 * D_MODEL), F32),
            jax.ShapeDtypeStruct((2, N_HEADS, T, HEAD_DIM), BF16),
            jax.ShapeDtypeStruct((N_HEADS, HEAD_DIM, T), BF16),
            jax.ShapeDtypeStruct((T, LANES), F32),
            jax.ShapeDtypeStruct((T // tm, 2, LANES), F32),
        ],
        scratch_shapes=[pltpu.VMEM((tm, D_MODEL), BF16)],
        compiler_params=pltpu.CompilerParams(
            dimension_semantics=("arbitrary",), vmem_limit_bytes=VMEM_LIMIT),
        name="in_proj",
    )(x2, x2, g, w_main, w_vt, w_f)


def _lru_kernel(xl_ref, gl_ref, ga_ref, fl_ref, cw_ref, cb_ref, wd_ref, ba_ref, bx_ref,
                lam_ref, fb_ref, wa_ref, e_ref,
                yag_ref, kf_ref, ft_ref, fend_ref,
                xq_sc, a_sc, b_sc, h_sc, halo_sc, hc_sc, fc_sc, *, tm):
    s = pl.program_id(1)
    seg = tm // SUBLANES
    pitch = seg + SUBLANES
    gps = seg // SUBLANES
    nc = D_MODEL // LANES
    n_halo = CONV_W - 1

    @pl.when(s == 0)
    def _():
        halo_sc[...] = jnp.zeros_like(halo_sc)
        hc_sc[...] = jnp.zeros_like(hc_sc)
        fc_sc[...] = jnp.zeros_like(fc_sc)

    for g in range(tm // SUBLANES):
        u, j0 = g // gps, (g % gps) * SUBLANES
        for c in range(nc):
            xq_sc[c, pl.ds(j0 * SUBLANES + u, SUBLANES, stride=SUBLANES), :] = (
                xl_ref[g * SUBLANES:(g + 1) * SUBLANES, c * LANES:(c + 1) * LANES])

    sub = lax.broadcasted_iota(jnp.int32, (SUBLANES, LANES), 0)
    c_all = -LRU_C * _softplus(-lam_ref[...])
    for c in range(nc):
        cols = slice(c * LANES, (c + 1) * LANES)
        x = xq_sc[c]
        wrap = []
        for q in range(n_halo):
            rows = slice(q * SUBLANES, (q + 1) * SUBLANES)
            rolled = pltpu.roll(x[tm - n_halo * SUBLANES:, :][rows], 1, axis=0)
            wrap.append(jnp.where(sub == 0, halo_sc[c, rows, :], rolled))
            halo_sc[c, rows, :] = rolled
        xa = cb_ref[:, cols] + cw_ref[CONV_W - 1:CONV_W, cols] * x
        for d in range(1, CONV_W):
            xd = jnp.concatenate(wrap[n_halo - d:] + [x[:tm - d * SUBLANES, :]], axis=0)
            xa = xa + cw_ref[CONV_W - 1 - d:CONV_W - d, cols] * xd
        ri = _dot(xa.astype(BF16), wd_ref[c])
        t_r = jnp.tanh(ri[:, :LANES] + ba_ref[:, cols])
        t_i = jnp.tanh(ri[:, LANES:] + bx_ref[:, cols])
        half_c = 0.5 * c_all[:, cols]
        a = jnp.exp(half_c * t_r + half_c)
        half_xa = 0.5 * xa
        one_m_a2 = 1.0 - a * a
        mult = one_m_a2 * lax.rsqrt(jnp.maximum(one_m_a2, 1e-30))
        a_sc[c] = a
        b_sc[c] = mult * (half_xa * t_i + half_xa)

    def scan_step(j, carry):
        r0 = pl.multiple_of(j * SUBLANES, SUBLANES)
        out = []
        for c in range(nc):
            h, p = carry[c]
            a = a_sc[c, pl.ds(r0, SUBLANES), :]
            h = a * h + b_sc[c, pl.ds(r0, SUBLANES), :]
            p = a * p
            b_sc[c, pl.ds(r0, SUBLANES), :] = h
            a_sc[c, pl.ds(r0, SUBLANES), :] = p
            out.append((h, p))
        return tuple(out)

    ends = lax.fori_loop(
        0, seg, scan_step,
        tuple((jnp.zeros((SUBLANES, LANES), F32), jnp.ones((SUBLANES, LANES), F32))
              for _ in range(nc)), unroll=SUBLANES)

    carry_in = []
    for c in range(nc):
        cols = slice(c * LANES, (c + 1) * LANES)
        h_end, p_end = ends[c]
        for d in (1, 2, 4):
            keep = sub >= d
            p_sh = jnp.where(keep, pltpu.roll(p_end, d, axis=0), 1.0)
            h_sh = jnp.where(keep, pltpu.roll(h_end, d, axis=0), 0.0)
            h_end = p_end * h_sh + h_end
            p_end = p_end * p_sh
        blk_in = hc_sc[:, cols]
        seg_out = h_end + p_end * blk_in
        carry_in.append(jnp.where(sub == 0, blk_in, pltpu.roll(seg_out, 1, axis=0)))
        hc_sc[:, cols] = jnp.broadcast_to(seg_out[SUBLANES - 1:SUBLANES, :], (SUBLANES, LANES))

    def fix_step(j, _):
        r0 = pl.multiple_of(j * SUBLANES, SUBLANES)
        for c in range(nc):
            h = b_sc[c, pl.ds(r0, SUBLANES), :] + a_sc[c, pl.ds(r0, SUBLANES), :] * carry_in[c]
            h_sc[c, pl.ds(j, SUBLANES, stride=pitch), :] = h
        return 0

    lax.fori_loop(0, seg, fix_step, 0, unroll=SUBLANES)

    h = jnp.concatenate(
        [jnp.concatenate([h_sc[c, u * pitch:u * pitch + seg, :] for u in range(SUBLANES)], axis=0)
         for c in range(nc)], axis=1)
    y = (gl_ref[...] * h).astype(BF16)
    yag_ref[...] = ga_ref[...] * _dot(y, wa_ref[...])

    lf = -_softplus(-(fl_ref[...] + fb_ref[...])) * LOG2E
    rowf = lax.broadcasted_iota(jnp.int32, (tm, LANES), 0)
    d = 1
    while d < tm:
        lf = lf + jnp.where(rowf >= d, pltpu.roll(lf, d, axis=0), 0.0)
        d *= 2
    fblk = lf + fc_sc[...]
    fc_sc[...] = fblk[tm - 1:tm, :]
    ft_ref[...] = fblk.T[0:N_HEADS, :]
    fend_ref[...] = fblk[tm - 1:tm, :]
    hi = (-fblk).astype(BF16)
    r1 = -fblk - hi.astype(F32)
    mid = r1.astype(BF16)
    lo = (r1 - mid.astype(F32)).astype(BF16)
    kf_ref[...] = _dot(jnp.concatenate([hi, mid, lo], axis=1), e_ref[...]).astype(BF16)


def _lru(pf, fl, cw, cb, wd, ba, bx, lam, fb, wa, e, B, S, tm):
    T = B * S
    ns = S // tm
    row_blk = lambda c: pl.BlockSpec((tm, D_MODEL), lambda b, s, c=c: (b * ns + s, c))
    return pl.pallas_call(
        functools.partial(_lru_kernel, tm=tm),
        grid=(B, ns),
        in_specs=[
            row_blk(0), row_blk(1), row_blk(2),
            pl.BlockSpec((tm, LANES), lambda b, s: (b * ns + s, 0)),
            _const_spec((CONV_W, D_MODEL)),
            _const_spec((1, D_MODEL)),
            _const_spec((D_MODEL // LANES, LANES, 2 * LANES)),
            _const_spec((1, D_MODEL)),
            _const_spec((1, D_MODEL)),
            _const_spec((1, D_MODEL)),
            _const_spec((1, LANES)),
            _const_spec((D_MODEL, D_MODEL)),
            _const_spec((F_PARTS * LANES, LANES)),
        ],
        out_specs=[
            pl.BlockSpec((tm, D_MODEL), lambda b, s: (b * ns + s, 0)),
            pl.BlockSpec((tm, LANES), lambda b, s: (b * ns + s, 0)),
            pl.BlockSpec((None, N_HEADS, tm), lambda b, s: (b, 0, s)),
            pl.BlockSpec((None, 1, LANES), lambda b, s: (b * ns + s, 0, 0)),
        ],
        out_shape=[
            jax.ShapeDtypeStruct((T, D_MODEL), F32),
            jax.ShapeDtypeStruct((T, LANES), BF16),
            jax.ShapeDtypeStruct((B, N_HEADS, S), F32),
            jax.ShapeDtypeStruct((T // tm, 1, LANES), F32),
        ],
        scratch_shapes=[
            pltpu.VMEM((D_MODEL // LANES, tm, LANES), F32),
            pltpu.VMEM((D_MODEL // LANES, tm, LANES), F32),
            pltpu.VMEM((D_MODEL // LANES, tm, LANES), F32),
            pltpu.VMEM((D_MODEL // LANES, tm + SUBLANES * SUBLANES, LANES), F32),
            pltpu.VMEM((D_MODEL // LANES, (CONV_W - 1) * SUBLANES, LANES), F32),
            pltpu.VMEM((SUBLANES, D_MODEL), F32),
            pltpu.VMEM((1, LANES), F32),
        ],
        compiler_params=pltpu.CompilerParams(
            dimension_semantics=("arbitrary", "arbitrary"), vmem_limit_bytes=VMEM_LIMIT),
        name="lru",
    )(pf, pf, pf, fl, cw, cb, wd, ba, bx, lam, fb, wa, e)


def _col_reduce(op, z):
    rows, cols = z.shape
    return op(op(z.reshape(SUBLANES, rows // SUBLANES, cols), axis=0), axis=0, keepdims=True)


def _attn_kernel(q_ref, qn_ref, k_ref, kf_ref, vt_ref, ft_ref, nrm_ref, fend_ref, o_ref,
                 z_sc, p_sc, acc_sc, m_sc, zmax_sc, skip_sm, *, tq, tk, hp, nq):
    hg = pl.program_id(1)
    qi = pl.program_id(2)
    n_blk = nrm_ref.shape[0]
    head_lane = lax.broadcasted_iota(jnp.int32, (1, LANES), 1)
    blk = lax.broadcasted_iota(jnp.int32, (n_blk, LANES), 0)

    def first_live_block(tile):
        q2 = jnp.maximum(nrm_ref[2 * tile, 0:1, :], nrm_ref[2 * tile + 1, 0:1, :])
        k2 = jnp.max(nrm_ref[:, 1, :], axis=0, keepdims=True)
        bound = (2.0 * NORM_MARGIN) * jnp.sqrt(q2 * k2) + fend_ref[jnp.maximum(2 * tile - 1, 0)]
        dead = (bound - fend_ref[:, 0, :] <= -P_UNDERFLOW_LOG2) & (blk < 2 * tile)
        first_live = jnp.min(jnp.where(dead, n_blk, blk), axis=0, keepdims=True)
        mine = (head_lane >= hg * hp) & (head_lane < (hg + 1) * hp)
        return jnp.min(jnp.where(mine, first_live, n_blk))

    lane = lax.broadcasted_iota(jnp.int32, (tq, LANES), 1)
    upper = slice(tk, tq)
    qa, qa_next, f_t = [], [], []
    for a in range(hp):
        h = hg * hp + a
        ones = jnp.where((lane >= F_PARTS * h) & (lane < F_PARTS * (h + 1)), 1.0, 0.0).astype(BF16)
        qa.append(jnp.concatenate([q_ref[a], ones], axis=1))
        qa_next.append(jnp.concatenate([qn_ref[a], ones], axis=1))
        f_t.append(ft_ref[pl.ds(h, 1), :])

    def qk(a, j, queries):
        c0 = pl.multiple_of(j * tk, tk)
        ka = jnp.concatenate([k_ref[a, pl.ds(c0, tk), :], kf_ref[pl.ds(c0, tk), :]], axis=1)
        return lax.dot_general(ka, queries, _NT_DIMS, preferred_element_type=F32)

    ones_rows = jnp.ones((BF16_SUBLANES, tk), BF16)

    def pv(a, j, p):
        c0 = pl.multiple_of(j * tk, tk)
        vt_aug = jnp.concatenate([vt_ref[a, :, pl.ds(c0, tk)], ones_rows], axis=0)
        return _dot(vt_aug, p)

    def softmax(z, m_row, zmax, f_row, masked):
        if masked:
            kr = lax.broadcasted_iota(jnp.int32, z.shape, 0)
            qc = lax.broadcasted_iota(jnp.int32, z.shape, 1)
            z = jnp.where(kr <= qc, z, -jnp.inf)
            zmax = _col_reduce(jnp.max, z)
        m_new = jnp.maximum(m_row, zmax + f_row)
        return m_new, jnp.exp2(m_row - m_new), jnp.exp2(z - (m_new - f_row)).astype(BF16)

    def trip(j, slot, masked=False):
        zmax_next = []
        for a in range(hp):
            if masked:
                z_sc[1 - slot, a, :, upper] = qk(a, j + 1, qa[a][upper])
            else:
                zn = qk(a, j + 1, qa[a])
                z_sc[1 - slot, a] = zn
                zmax_next.append(_col_reduce(jnp.max, zn))
        pvs = [pv(a, jnp.maximum(j - 1, 0), p_sc[a]) for a in range(hp)]
        for a in range(hp):
            m_new, alpha, p = softmax(z_sc[slot, a], m_sc[a], zmax_sc[a], f_t[a], masked)
            p_sc[a] = p
            m_sc[a] = m_new
            acc_sc[a] = alpha * (acc_sc[a] + pvs[a])
            if not masked:
                zmax_sc[a] = zmax_next[a]

    @pl.when((pl.program_id(0) == 0) & (hg == 0) & (qi == 0))
    def _():
        p_sc[...] = jnp.zeros_like(p_sc)

    @pl.when(qi == 0)
    def _():
        skip_sm[0] = 0
        for a in range(hp):
            z0 = qk(a, 0, qa[a])
            z_sc[0, a] = z0
            zmax_sc[a] = _col_reduce(jnp.max, z0)

    for a in range(hp):
        m_sc[a] = jnp.full((1, tq), -jnp.inf, F32)
        acc_sc[a] = jnp.zeros((HEAD_DIM + BF16_SUBLANES, tq), F32)

    start = skip_sm[0]

    @pl.when((start & 1) == 1)
    def _():
        trip(start, 1)

    @pl.loop((start + 1) >> 1, qi)
    def _(i):
        trip(2 * i, 0)
        trip(2 * i + 1, 1)

    first = 2 * qi
    trip(first, 0, masked=True)
    for a in range(hp):
        acc = acc_sc[a] + pv(a, first, p_sc[a])
        _, alpha, p = softmax(z_sc[1, a, :, upper], m_sc[a][:, upper], None, f_t[a][:, upper], True)
        acc_up = alpha * acc[:, upper] + pv(a, first + 1, p)
        out = jnp.concatenate(
            [acc[:HEAD_DIM, :tk] / acc[HEAD_DIM:HEAD_DIM + 1, :tk],
             acc_up[:HEAD_DIM] / acc_up[HEAD_DIM:HEAD_DIM + 1]], axis=1)
        o_ref[a] = out.T.astype(BF16)

    start_next = first_live_block(jnp.minimum(qi + 1, nq - 1))
    skip_sm[0] = start_next
    for a in range(hp):
        zn = qk(a, start_next, qa_next[a])
        z_sc[start_next & 1, a] = zn
        zmax_sc[a] = _col_reduce(jnp.max, zn)


def _attn(qk, kf, vt, ft, nrm, fend, B, S, tq, tk, hp):
    T = B * S
    nq = S // tq
    n_blk = S // tk
    assert tq == 2 * tk and nrm.shape[0] == B * n_blk and fend.shape[0] == B * n_blk
    return pl.pallas_call(
        functools.partial(_attn_kernel, tq=tq, tk=tk, hp=hp, nq=nq),
        grid=(B, N_HEADS // hp, nq),
        in_specs=[
            pl.BlockSpec((None, hp, tq, HEAD_DIM), lambda b, g, i: (0, g, b * nq + i, 0)),
            pl.BlockSpec((None, hp, tq, HEAD_DIM),
                         lambda b, g, i: (0, g, b * nq + jnp.minimum(i + 1, nq - 1), 0)),
            pl.BlockSpec((None, hp, S, HEAD_DIM), lambda b, g, i: (1, g, b, 0)),
            pl.BlockSpec((S, LANES), lambda b, g, i: (b, 0)),
            pl.BlockSpec((hp, HEAD_DIM, S), lambda b, g, i: (g, 0, b)),
            pl.BlockSpec((None, N_HEADS, tq), lambda b, g, i: (b, 0, i)),
            pl.BlockSpec((n_blk, 2, LANES), lambda b, g, i: (b, 0, 0)),
            pl.BlockSpec((n_blk, 1, LANES), lambda b, g, i: (b, 0, 0)),
        ],
        out_specs=pl.BlockSpec((hp, tq, HEAD_DIM), lambda b, g, i: (g, b * nq + i, 0)),
        out_shape=jax.ShapeDtypeStruct((N_HEADS, T, HEAD_DIM), BF16),
        scratch_shapes=[
            pltpu.VMEM((2, hp, tk, tq), F32),
            pltpu.VMEM((hp, tk, tq), BF16),
            pltpu.VMEM((hp, HEAD_DIM + BF16_SUBLANES, tq), F32),
            pltpu.VMEM((hp, 1, tq), F32),
            pltpu.VMEM((hp, 1, tq), F32),
            pltpu.SMEM((1,), jnp.int32),
        ],
        compiler_params=pltpu.CompilerParams(
            dimension_semantics=("arbitrary", "arbitrary", "arbitrary"),
            vmem_limit_bytes=VMEM_LIMIT),
        name="fox_attn",
    )(qk, qk, qk, kf, vt, ft, nrm, fend)


def _merge_mlp_kernel(ob_ref, yag_ref, gb_ref, x_ref, wb_ref, wo_ref, gm_ref, wu_ref,
                      wdn_ref, gf_ref, o_ref):
    ob = jnp.concatenate([ob_ref[h] for h in range(N_HEADS)], axis=1)
    yb = _dot(ob, wb_ref[...])
    mix = yag_ref[...] + gb_ref[...] * yb
    x1 = x_ref[...] + _dot(mix.astype(BF16), wo_ref[...])
    m = _rms(x1, gm_ref[...]).astype(BF16)
    acc = x1
    for c in range(D_FF // D_MODEL):
        cols = slice(c * D_MODEL, (c + 1) * D_MODEL)
        hc = jnp.maximum(_dot(m, wu_ref[:, cols]), 0.0)
        acc = acc + _dot((hc * hc).astype(BF16), wdn_ref[cols, :])
    o_ref[...] = _rms(acc, gf_ref[...])


def _merge_mlp(ob, yag, pf, x2, wb, wo, gm, wu, wdn, gf, tm):
    T = x2.shape[0]
    return pl.pallas_call(
        _merge_mlp_kernel,
        grid=(T // tm,),
        in_specs=[
            pl.BlockSpec((N_HEADS, tm, HEAD_DIM), lambda i: (0, i, 0)),
            pl.BlockSpec((tm, D_MODEL), lambda i: (i, 0)),
            pl.BlockSpec((tm, D_MODEL), lambda i: (i, 3)),
            pl.BlockSpec((tm, D_MODEL), lambda i: (i, 0)),
            _const_spec((D_MODEL, D_MODEL)),
            _const_spec((D_MODEL, D_MODEL)),
            _const_spec((1, D_MODEL)),
            _const_spec((D_MODEL, D_FF)),
            _const_spec((D_FF, D_MODEL)),
            _const_spec((1, D_MODEL)),
        ],
        out_specs=pl.BlockSpec((tm, D_MODEL), lambda i: (i, 0)),
        out_shape=jax.ShapeDtypeStruct((T, D_MODEL), F32),
        compiler_params=pltpu.CompilerParams(
            dimension_semantics=("arbitrary",), vmem_limit_bytes=VMEM_LIMIT),
        name="merge_mlp",
    )(ob, yag, pf, x2, wb, wo, gm, wu, wdn, gf)


def _block_diag_pairs(wa, wx):
    def pair(w):
        w = w.reshape(LRU_BLOCKS // 2, 2, LRU_BW, LRU_BW)
        z = jnp.zeros_like(w[:, 0])
        top = jnp.concatenate([w[:, 0], z], axis=2)
        bot = jnp.concatenate([z, w[:, 1]], axis=2)
        return jnp.concatenate([top, bot], axis=1)
    return jnp.concatenate([pair(wa), pair(wx)], axis=2).astype(BF16)


def kernel(x, norm_mix_g, w_in, conv_w, conv_b, lru_wa, lru_ba, lru_wx, lru_bx, lru_lambda,
           forget_b, w_branch_a, w_branch_b, w_out, norm_mlp_g, w_up, w_down, norm_final_g):
    B, S, D = x.shape
    assert D == D_MODEL
    T = B * S
    tm = min(512, S)
    tq = min(1024, S)
    tk = tq // 2
    assert S % tm == 0 and S % tq == 0 and tm % LANES == 0

    w_main = w_in.astype(BF16)
    w_vt = w_in[:, W_IN_V_SLAB * D:(W_IN_V_SLAB + 1) * D].T.astype(BF16)
    w_f = jnp.pad(w_in[:, 7 * D:], ((0, 0), (0, LANES - N_HEADS))).astype(BF16)
    wd = _block_diag_pairs(0.5 * lru_wa, 0.5 * lru_wx)
    row = lambda v: v.reshape(1, -1).astype(F32)
    fb = jnp.pad(forget_b, (0, LANES - N_HEADS)).reshape(1, LANES)
    e_rows = jnp.arange(F_PARTS * LANES)
    e_tgt = jnp.where(e_rows % LANES < N_HEADS, (e_rows % LANES) * F_PARTS + e_rows // LANES, -1)
    e = (e_tgt[:, None] == jnp.arange(LANES)[None, :]).astype(BF16)

    x2 = x.reshape(T, D)
    pf, qk, vt, fl, nrm = _in_proj(x2, row(norm_mix_g), w_main, w_vt, w_f, tm)
    yag, kf, ft, fend = _lru(pf, fl, conv_w, row(conv_b), wd, row(0.5 * lru_ba), row(0.5 * lru_bx),
                             row(lru_lambda), fb, w_branch_a.astype(BF16), e, B, S, tm)
    assert tk == tm
    ob = _attn(qk, kf, vt, ft, nrm, fend, B, S, tq, tk, hp=2)
    out = _merge_mlp(ob, yag, pf, x2, w_branch_b.astype(BF16), w_out.astype(BF16),
                     row(norm_mlp_g), w_up.astype(BF16), w_down.astype(BF16),
                     row(norm_final_g), tm)
    return out.reshape(B, S, D)
```

```python
import functools
import math

import jax
import jax.numpy as jnp
from jax import lax
from jax.experimental import pallas as pl
from jax.experimental.pallas import tpu as pltpu

D_MODEL = 1024
N_HEADS = 8
HEAD_DIM = 128
LRU_BLOCKS = 16
LRU_BW = 64
CONV_W = 4
LRU_C = 8.0
D_FF = 4 * D_MODEL
RMS_EPS = 1e-6
LANES = 128
SUBLANES = 8
BF16_SUBLANES = 16
LOG2E = 1.4426950408889634
Q_SCALE = LOG2E / math.sqrt(HEAD_DIM)
VMEM_LIMIT = 56 * 1024 * 1024

BF16 = jnp.bfloat16
F32 = jnp.float32
_NT_DIMS = (((1,), (1,)), ((), ()))
W_IN_XL_SLAB = 0
W_IN_PF_SLABS = (1, 5, 6)
W_IN_QK_SLABS = (2, 3)
W_IN_V_SLAB = 4
P_UNDERFLOW_LOG2 = 160.0
NORM_MARGIN = 1.02
F_PARTS = 3


def _dot(a, b):
    return jnp.dot(a, b, preferred_element_type=F32)


def _rms(x, g):
    return x * lax.rsqrt(jnp.mean(x * x, axis=-1, keepdims=True) + RMS_EPS) * g


def _softplus(x):
    return jnp.maximum(x, 0.0) + jnp.log1p(jnp.exp(-jnp.abs(x)))


def _sigmoid(x):
    return 0.5 * jnp.tanh(0.5 * x) + 0.5


def _gelu_tanh(x):
    c = math.sqrt(2.0 / math.pi)
    half_x = 0.5 * x
    return half_x + half_x * jnp.tanh(x * (c + (c * 0.044715) * (x * x)))


def _const_spec(shape):
    nd = len(shape)
    return pl.BlockSpec(shape, lambda *_: (0,) * nd, pipeline_mode=pl.Buffered(1))


def _in_proj_kernel(x_ref, g_ref, w_ref, wvt_ref, wf_ref, xq_ref, pf_ref, qk_ref, vt_ref, fl_ref,
                    nrm_ref):
    tm = x_ref.shape[0]
    u = _rms(x_ref[...], g_ref[...]).astype(BF16)
    fl_ref[...] = _dot(u, wf_ref[...])
    xl = _dot(u, w_ref[:, W_IN_XL_SLAB * D_MODEL:(W_IN_XL_SLAB + 1) * D_MODEL])
    gps = tm // (SUBLANES * SUBLANES)
    for g in range(tm // SUBLANES):
        seg, j0 = g // gps, (g % gps) * SUBLANES
        for c in range(D_MODEL // LANES):
            xq_ref[c, pl.ds(j0 * SUBLANES + seg, SUBLANES, stride=SUBLANES), :] = (
                xl[g * SUBLANES:(g + 1) * SUBLANES, c * LANES:(c + 1) * LANES])
    post = (_gelu_tanh, _sigmoid, _sigmoid)
    for j, src in enumerate(W_IN_PF_SLABS):
        res = _dot(u, w_ref[:, src * D_MODEL:(src + 1) * D_MODEL])
        pf_ref[:, j * D_MODEL:(j + 1) * D_MODEL] = post[j](res)
    for j, src in enumerate(W_IN_QK_SLABS):
        res = _dot(u, w_ref[:, src * D_MODEL:(src + 1) * D_MODEL])
        if j == 0:
            res = res * Q_SCALE
        lane = lax.broadcasted_iota(jnp.int32, (1, LANES), 1)
        nrm = jnp.zeros((1, LANES), F32)
        for h in range(N_HEADS):
            xb = res[:, h * HEAD_DIM:(h + 1) * HEAD_DIM].astype(BF16)
            qk_ref[j, h] = xb
            xf = xb.astype(F32)
            n2 = jnp.max(jnp.sum(xf * xf, axis=1, keepdims=True), axis=0, keepdims=True)
            nrm = jnp.where(lane == h, n2, nrm)
        nrm_ref[j:j + 1, :] = nrm
    vt = lax.dot_general(wvt_ref[...], u, _NT_DIMS, preferred_element_type=F32)
    for h in range(N_HEADS):
        vt_ref[h] = vt[h * HEAD_DIM:(h + 1) * HEAD_DIM, :].astype(BF16)


def _in_proj(x2, g, w_main, w_vt, w_f, tm):
    T = x2.shape[0]
    return pl.pallas_call(
        _in_proj_kernel,
        grid=(T // tm,),
        in_specs=[
            pl.BlockSpec((tm, D_MODEL), lambda i: (i, 0)),
            _const_spec((1, D_MODEL)),
            _const_spec(w_main.shape),
            _const_spec((D_MODEL, D_MODEL)),
            _const_spec((D_MODEL, LANES)),
        ],
        out_specs=[
            pl.BlockSpec((None, D_MODEL // LANES, tm, LANES), lambda i: (i, 0, 0, 0)),
            pl.BlockSpec((tm, 3 * D_MODEL), lambda i: (i, 0)),
            pl.BlockSpec((2, N_HEADS, tm, HEAD_DIM), lambda i: (0, 0, i, 0)),
            pl.BlockSpec((N_HEADS, HEAD_DIM, tm), lambda i: (0, 0, i)),
            pl.BlockSpec((tm, LANES), lambda i: (i, 0)),
            pl.BlockSpec((None, 2, LANES), lambda i: (i, 0, 0)),
        ],
        out_shape=[
            jax.ShapeDtypeStruct((T // tm, D_MODEL // LANES, tm, LANES), F32),
            jax.ShapeDtypeStruct((T, 3 * D_MODEL), F32),
            jax.ShapeDtypeStruct((2, N_HEADS, T, HEAD_DIM), BF16),
            jax.ShapeDtypeStruct((N_HEADS, HEAD_DIM, T), BF16),
            jax.ShapeDtypeStruct((T, LANES), F32),
            jax.ShapeDtypeStruct((T // tm, 2, LANES), F32),
        ],
        compiler_params=pltpu.CompilerParams(
            dimension_semantics=("arbitrary",), vmem_limit_bytes=VMEM_LIMIT),
        name="in_proj",
    )(x2, g, w_main, w_vt, w_f)


def _lru_kernel(xq_ref, gl_ref, ga_ref, fl_ref, cw_ref, cb_ref, wd_ref, ba_ref, bx_ref,
                lam_ref, fb_ref, wa_ref, e_ref,
                yag_ref, kf_ref, ft_ref, fend_ref,
                a_sc, b_sc, h_sc, halo_sc, hc_sc, fc_sc, *, tm):
    s = pl.program_id(1)
    seg = tm // SUBLANES
    pitch = seg + SUBLANES
    nc = D_MODEL // LANES
    n_halo = CONV_W - 1

    @pl.when(s == 0)
    def _():
        halo_sc[...] = jnp.zeros_like(halo_sc)
        hc_sc[...] = jnp.zeros_like(hc_sc)
        fc_sc[...] = jnp.zeros_like(fc_sc)

    sub = lax.broadcasted_iota(jnp.int32, (SUBLANES, LANES), 0)
    c_all = -LRU_C * _softplus(-lam_ref[...])
    for c in range(nc):
        cols = slice(c * LANES, (c + 1) * LANES)
        x = xq_ref[c]
        wrap = []
        for q in range(n_halo):
            rows = slice(q * SUBLANES, (q + 1) * SUBLANES)
            rolled = pltpu.roll(x[tm - n_halo * SUBLANES:, :][rows], 1, axis=0)
            wrap.append(jnp.where(sub == 0, halo_sc[c, rows, :], rolled))
            halo_sc[c, rows, :] = rolled
        xa = cb_ref[:, cols] + cw_ref[CONV_W - 1:CONV_W, cols] * x
        for d in range(1, CONV_W):
            xd = jnp.concatenate(wrap[n_halo - d:] + [x[:tm - d * SUBLANES, :]], axis=0)
            xa = xa + cw_ref[CONV_W - 1 - d:CONV_W - d, cols] * xd
        ri = _dot(xa.astype(BF16), wd_ref[c])
        t_r = jnp.tanh(ri[:, :LANES] + ba_ref[:, cols])
        t_i = jnp.tanh(ri[:, LANES:] + bx_ref[:, cols])
        half_c = 0.5 * c_all[:, cols]
        a = jnp.exp(half_c * t_r + half_c)
        half_xa = 0.5 * xa
        one_m_a2 = 1.0 - a * a
        mult = one_m_a2 * lax.rsqrt(jnp.maximum(one_m_a2, 1e-30))
        a_sc[c] = a
        b_sc[c] = mult * (half_xa * t_i + half_xa)

    def scan_step(j, carry):
        r0 = pl.multiple_of(j * SUBLANES, SUBLANES)
        out = []
        for c in range(nc):
            h, p = carry[c]
            a = a_sc[c, pl.ds(r0, SUBLANES), :]
            h = a * h + b_sc[c, pl.ds(r0, SUBLANES), :]
            p = a * p
            b_sc[c, pl.ds(r0, SUBLANES), :] = h
            a_sc[c, pl.ds(r0, SUBLANES), :] = p
            out.append((h, p))
        return tuple(out)

    ends = lax.fori_loop(
        0, seg, scan_step,
        tuple((jnp.zeros((SUBLANES, LANES), F32), jnp.ones((SUBLANES, LANES), F32))
              for _ in range(nc)), unroll=SUBLANES)

    carry_in = []
    for c in range(nc):
        cols = slice(c * LANES, (c + 1) * LANES)
        h_end, p_end = ends[c]
        for d in (1, 2, 4):
            keep = sub >= d
            p_sh = jnp.where(keep, pltpu.roll(p_end, d, axis=0), 1.0)
            h_sh = jnp.where(keep, pltpu.roll(h_end, d, axis=0), 0.0)
            h_end = p_end * h_sh + h_end
            p_end = p_end * p_sh
        blk_in = hc_sc[:, cols]
        seg_out = h_end + p_end * blk_in
        carry_in.append(jnp.where(sub == 0, blk_in, pltpu.roll(seg_out, 1, axis=0)))
        hc_sc[:, cols] = jnp.broadcast_to(seg_out[SUBLANES - 1:SUBLANES, :], (SUBLANES, LANES))

    def fix_step(j, _):
        r0 = pl.multiple_of(j * SUBLANES, SUBLANES)
        for c in range(nc):
            h = b_sc[c, pl.ds(r0, SUBLANES), :] + a_sc[c, pl.ds(r0, SUBLANES), :] * carry_in[c]
            h_sc[c, pl.ds(j, SUBLANES, stride=pitch), :] = h
        return 0

    lax.fori_loop(0, seg, fix_step, 0, unroll=SUBLANES)

    h = jnp.concatenate(
        [jnp.concatenate([h_sc[c, u * pitch:u * pitch + seg, :] for u in range(SUBLANES)], axis=0)
         for c in range(nc)], axis=1)
    y = (gl_ref[...] * h).astype(BF16)
    yag_ref[...] = ga_ref[...] * _dot(y, wa_ref[...])

    lf = -_softplus(-(fl_ref[...] + fb_ref[...])) * LOG2E
    rowf = lax.broadcasted_iota(jnp.int32, (tm, LANES), 0)
    d = 1
    while d < tm:
        lf = lf + jnp.where(rowf >= d, pltpu.roll(lf, d, axis=0), 0.0)
        d *= 2
    fblk = lf + fc_sc[...]
    fc_sc[...] = fblk[tm - 1:tm, :]
    ft_ref[...] = fblk.T[0:N_HEADS, :]
    fend_ref[...] = fblk[tm - 1:tm, :]
    hi = (-fblk).astype(BF16)
    r1 = -fblk - hi.astype(F32)
    mid = r1.astype(BF16)
    lo = (r1 - mid.astype(F32)).astype(BF16)
    kf_ref[...] = _dot(jnp.concatenate([hi, mid, lo], axis=1), e_ref[...]).astype(BF16)


def _lru(xq, pf, fl, cw, cb, wd, ba, bx, lam, fb, wa, e, B, S, tm):
    T = B * S
    ns = S // tm
    row_blk = lambda c: pl.BlockSpec((tm, D_MODEL), lambda b, s, c=c: (b * ns + s, c))
    return pl.pallas_call(
        functools.partial(_lru_kernel, tm=tm),
        grid=(B, ns),
        in_specs=[
            pl.BlockSpec((None, D_MODEL // LANES, tm, LANES), lambda b, s: (b * ns + s, 0, 0, 0)),
            row_blk(0), row_blk(1),
            pl.BlockSpec((tm, LANES), lambda b, s: (b * ns + s, 0)),
            _const_spec((CONV_W, D_MODEL)),
            _const_spec((1, D_MODEL)),
            _const_spec((D_MODEL // LANES, LANES, 2 * LANES)),
            _const_spec((1, D_MODEL)),
            _const_spec((1, D_MODEL)),
            _const_spec((1, D_MODEL)),
            _const_spec((1, LANES)),
            _const_spec((D_MODEL, D_MODEL)),
            _const_spec((F_PARTS * LANES, LANES)),
        ],
        out_specs=[
            pl.BlockSpec((tm, D_MODEL), lambda b, s: (b * ns + s, 0)),
            pl.BlockSpec((tm, LANES), lambda b, s: (b * ns + s, 0)),
            pl.BlockSpec((None, N_HEADS, tm), lambda b, s: (b, 0, s)),
            pl.BlockSpec((None, 1, LANES), lambda b, s: (b * ns + s, 0, 0)),
        ],
        out_shape=[
            jax.ShapeDtypeStruct((T, D_MODEL), F32),
            jax.ShapeDtypeStruct((T, LANES), BF16),
            jax.ShapeDtypeStruct((B, N_HEADS, S), F32),
            jax.ShapeDtypeStruct((T // tm, 1, LANES), F32),
        ],
        scratch_shapes=[
            pltpu.VMEM((D_MODEL // LANES, tm, LANES), F32),
            pltpu.VMEM((D_MODEL // LANES, tm, LANES), F32),
            pltpu.VMEM((D_MODEL // LANES, tm + SUBLANES * SUBLANES, LANES), F32),
            pltpu.VMEM((D_MODEL // LANES, (CONV_W - 1) * SUBLANES, LANES), F32),
            pltpu.VMEM((SUBLANES, D_MODEL), F32),
            pltpu.VMEM((1, LANES), F32),
        ],
        compiler_params=pltpu.CompilerParams(
            dimension_semantics=("arbitrary", "arbitrary"), vmem_limit_bytes=VMEM_LIMIT),
        name="lru",
    )(xq, pf, pf, fl, cw, cb, wd, ba, bx, lam, fb, wa, e)


def _col_reduce(op, z):
    rows, cols = z.shape
    return op(op(z.reshape(SUBLANES, rows // SUBLANES, cols), axis=0), axis=0, keepdims=True)


def _attn_kernel(q_ref, qn_ref, k_ref, kf_ref, vt_ref, ft_ref, nrm_ref, fend_ref, o_ref,
                 z_sc, p_sc, acc_sc, m_sc, zmax_sc, skip_sm, *, tq, tk, hp, nq):
    hg = pl.program_id(1)
    qi = pl.program_id(2)
    n_blk = nrm_ref.shape[0]
    head_lane = lax.broadcasted_iota(jnp.int32, (1, LANES), 1)
    blk = lax.broadcasted_iota(jnp.int32, (n_blk, LANES), 0)

    def first_live_block(tile):
        q2 = jnp.maximum(nrm_ref[2 * tile, 0:1, :], nrm_ref[2 * tile + 1, 0:1, :])
        k2 = jnp.max(nrm_ref[:, 1, :], axis=0, keepdims=True)
        bound = (2.0 * NORM_MARGIN) * jnp.sqrt(q2 * k2) + fend_ref[jnp.maximum(2 * tile - 1, 0)]
        dead = (bound - fend_ref[:, 0, :] <= -P_UNDERFLOW_LOG2) & (blk < 2 * tile)
        first_live = jnp.min(jnp.where(dead, n_blk, blk), axis=0, keepdims=True)
        mine = (head_lane >= hg * hp) & (head_lane < (hg + 1) * hp)
        return jnp.min(jnp.where(mine, first_live, n_blk))

    lane = lax.broadcasted_iota(jnp.int32, (tq, LANES), 1)
    upper = slice(tk, tq)
    qa, qa_next, f_t = [], [], []
    for a in range(hp):
        h = hg * hp + a
        ones = jnp.where((lane >= F_PARTS * h) & (lane < F_PARTS * (h + 1)), 1.0, 0.0).astype(BF16)
        qa.append(jnp.concatenate([q_ref[a], ones], axis=1))
        qa_next.append(jnp.concatenate([qn_ref[a], ones], axis=1))
        f_t.append(ft_ref[pl.ds(h, 1), :])

    def qk(a, j, queries):
        c0 = pl.multiple_of(j * tk, tk)
        ka = jnp.concatenate([k_ref[a, pl.ds(c0, tk), :], kf_ref[pl.ds(c0, tk), :]], axis=1)
        return lax.dot_general(ka, queries, _NT_DIMS, preferred_element_type=F32)

    ones_rows = jnp.ones((BF16_SUBLANES, tk), BF16)

    def pv(a, j, p):
        c0 = pl.multiple_of(j * tk, tk)
        vt_aug = jnp.concatenate([vt_ref[a, :, pl.ds(c0, tk)], ones_rows], axis=0)
        return _dot(vt_aug, p)

    def softmax(z, m_row, zmax, f_row, masked):
        if masked:
            kr = lax.broadcasted_iota(jnp.int32, z.shape, 0)
            qc = lax.broadcasted_iota(jnp.int32, z.shape, 1)
            z = jnp.where(kr <= qc, z, -jnp.inf)
            zmax = _col_reduce(jnp.max, z)
        m_new = jnp.maximum(m_row, zmax + f_row)
        return m_new, jnp.exp2(m_row - m_new), jnp.exp2(z - (m_new - f_row)).astype(BF16)

    def trip(j, slot, masked=False):
        zmax_next = []
        for a in range(hp):
            if masked:
                z_sc[1 - slot, a, :, upper] = qk(a, j + 1, qa[a][upper])
            else:
                zn = qk(a, j + 1, qa[a])
                z_sc[1 - slot, a] = zn
                zmax_next.append(_col_reduce(jnp.max, zn))
        pvs = [pv(a, jnp.maximum(j - 1, 0), p_sc[a]) for a in range(hp)]
        for a in range(hp):
            m_new, alpha, p = softmax(z_sc[slot, a], m_sc[a], zmax_sc[a], f_t[a], masked)
            p_sc[a] = p
            m_sc[a] = m_new
            acc_sc[a] = alpha * (acc_sc[a] + pvs[a])
            if not masked:
                zmax_sc[a] = zmax_next[a]

    @pl.when((pl.program_id(0) == 0) & (hg == 0) & (qi == 0))
    def _():
        p_sc[...] = jnp.zeros_like(p_sc)

    @pl.when(qi == 0)
    def _():
        skip_sm[0] = 0
        for a in range(hp):
            z0 = qk(a, 0, qa[a])
            z_sc[0, a] = z0
            zmax_sc[a] = _col_reduce(jnp.max, z0)

    for a in range(hp):
        m_sc[a] = jnp.full((1, tq), -jnp.inf, F32)
        acc_sc[a] = jnp.zeros((HEAD_DIM + BF16_SUBLANES, tq), F32)

    start = skip_sm[0]

    @pl.when((start & 1) == 1)
    def _():
        trip(start, 1)

    @pl.loop((start + 1) >> 1, qi)
    def _(i):
        trip(2 * i, 0)
        trip(2 * i + 1, 1)

    first = 2 * qi
    trip(first, 0, masked=True)
    for a in range(hp):
        acc = acc_sc[a] + pv(a, first, p_sc[a])
        _, alpha, p = softmax(z_sc[1, a, :, upper], m_sc[a][:, upper], None, f_t[a][:, upper], True)
        acc_up = alpha * acc[:, upper] + pv(a, first + 1, p)
        out = jnp.concatenate(
            [acc[:HEAD_DIM, :tk] / acc[HEAD_DIM:HEAD_DIM + 1, :tk],
             acc_up[:HEAD_DIM] / acc_up[HEAD_DIM:HEAD_DIM + 1]], axis=1)
        o_ref[a] = out.T.astype(BF16)

    start_next = first_live_block(jnp.minimum(qi + 1, nq - 1))
    skip_sm[0] = start_next
    for a in range(hp):
        zn = qk(a, start_next, qa_next[a])
        z_sc[start_next & 1, a] = zn
        zmax_sc[a] = _col_reduce(jnp.max, zn)


def _attn(qk, kf, vt, ft, nrm, fend, B, S, tq, tk, hp):
    T = B * S
    nq = S // tq
    n_blk = S // tk
    assert tq == 2 * tk and nrm.shape[0] == B * n_blk and fend.shape[0] == B * n_blk
    return pl.pallas_call(
        functools.partial(_attn_kernel, tq=tq, tk=tk, hp=hp, nq=nq),
        grid=(B, N_HEADS // hp, nq),
        in_specs=[
            pl.BlockSpec((None, hp, tq, HEAD_DIM), lambda b, g, i: (0, g, b * nq + i, 0)),
            pl.BlockSpec((None, hp, tq, HEAD_DIM),
                         lambda b, g, i: (0, g, b * nq + jnp.minimum(i + 1, nq - 1), 0)),
            pl.BlockSpec((None, hp, S, HEAD_DIM), lambda b, g, i: (1, g, b, 0)),
            pl.BlockSpec((S, LANES), lambda b, g, i: (b, 0)),
            pl.BlockSpec((hp, HEAD_DIM, S), lambda b, g, i: (g, 0, b)),
            pl.BlockSpec((None, N_HEADS, tq), lambda b, g, i: (b, 0, i)),
            pl.BlockSpec((n_blk, 2, LANES), lambda b, g, i: (b, 0, 0)),
            pl.BlockSpec((n_blk, 1, LANES), lambda b, g, i: (b, 0, 0)),
        ],
        out_specs=pl.BlockSpec((hp, tq, HEAD_DIM), lambda b, g, i: (g, b * nq + i, 0)),
        out_shape=jax.ShapeDtypeStruct((N_HEADS, T, HEAD_DIM), BF16),
        scratch_shapes=[
            pltpu.VMEM((2, hp, tk, tq), F32),
            pltpu.VMEM((hp, tk, tq), BF16),
            pltpu.VMEM((hp, HEAD_DIM + BF16_SUBLANES, tq), F32),
            pltpu.VMEM((hp, 1, tq), F32),
            pltpu.VMEM((hp, 1, tq), F32),
            pltpu.SMEM((1,), jnp.int32),
        ],
        compiler_params=pltpu.CompilerParams(
            dimension_semantics=("arbitrary", "arbitrary", "arbitrary"),
            vmem_limit_bytes=VMEM_LIMIT),
        name="fox_attn",
    )(qk, qk, qk, kf, vt, ft, nrm, fend)


def _merge_mlp_kernel(ob_ref, yag_ref, gb_ref, x_ref, wb_ref, wo_ref, gm_ref, wu_ref,
                      wdn_ref, gf_ref, o_ref):
    ob = jnp.concatenate([ob_ref[h] for h in range(N_HEADS)], axis=1)
    yb = _dot(ob, wb_ref[...])
    mix = yag_ref[...] + gb_ref[...] * yb
    x1 = x_ref[...] + _dot(mix.astype(BF16), wo_ref[...])
    m = _rms(x1, gm_ref[...]).astype(BF16)
    acc = x1
    for c in range(D_FF // D_MODEL):
        cols = slice(c * D_MODEL, (c + 1) * D_MODEL)
        hc = jnp.maximum(_dot(m, wu_ref[:, cols]), 0.0)
        acc = acc + _dot((hc * hc).astype(BF16), wdn_ref[cols, :])
    o_ref[...] = _rms(acc, gf_ref[...])


def _merge_mlp(ob, yag, pf, x2, wb, wo, gm, wu, wdn, gf, tm):
    T = x2.shape[0]
    return pl.pallas_call(
        _merge_mlp_kernel,
        grid=(T // tm,),
        in_specs=[
            pl.BlockSpec((N_HEADS, tm, HEAD_DIM), lambda i: (0, i, 0)),
            pl.BlockSpec((tm, D_MODEL), lambda i: (i, 0)),
            pl.BlockSpec((tm, D_MODEL), lambda i: (i, 2)),
            pl.BlockSpec((tm, D_MODEL), lambda i: (i, 0)),
            _const_spec((D_MODEL, D_MODEL)),
            _const_spec((D_MODEL, D_MODEL)),
            _const_spec((1, D_MODEL)),
            _const_spec((D_MODEL, D_FF)),
            _const_spec((D_FF, D_MODEL)),
            _const_spec((1, D_MODEL)),
        ],
        out_specs=pl.BlockSpec((tm, D_MODEL), lambda i: (i, 0)),
        out_shape=jax.ShapeDtypeStruct((T, D_MODEL), F32),
        compiler_params=pltpu.CompilerParams(
            dimension_semantics=("arbitrary",), vmem_limit_bytes=VMEM_LIMIT),
        name="merge_mlp",
    )(ob, yag, pf, x2, wb, wo, gm, wu, wdn, gf)


def _block_diag_pairs(wa, wx):
    def pair(w):
        w = w.reshape(LRU_BLOCKS // 2, 2, LRU_BW, LRU_BW)
        z = jnp.zeros_like(w[:, 0])
        top = jnp.concatenate([w[:, 0], z], axis=2)
        bot = jnp.concatenate([z, w[:, 1]], axis=2)
        return jnp.concatenate([top, bot], axis=1)
    return jnp.concatenate([pair(wa), pair(wx)], axis=2).astype(BF16)


def kernel(x, norm_mix_g, w_in, conv_w, conv_b, lru_wa, lru_ba, lru_wx, lru_bx, lru_lambda,
           forget_b, w_branch_a, w_branch_b, w_out, norm_mlp_g, w_up, w_down, norm_final_g):
    B, S, D = x.shape
    assert D == D_MODEL
    T = B * S
    tm = min(512, S)
    tq = min(1024, S)
    tk = tq // 2
    assert S % tm == 0 and S % tq == 0 and tm % LANES == 0

    w_main = w_in.astype(BF16)
    w_vt = w_in[:, W_IN_V_SLAB * D:(W_IN_V_SLAB + 1) * D].T.astype(BF16)
    w_f = jnp.pad(w_in[:, 7 * D:], ((0, 0), (0, LANES - N_HEADS))).astype(BF16)
    wd = _block_diag_pairs(0.5 * lru_wa, 0.5 * lru_wx)
    row = lambda v: v.reshape(1, -1).astype(F32)
    fb = jnp.pad(forget_b, (0, LANES - N_HEADS)).reshape(1, LANES)
    e_rows = jnp.arange(F_PARTS * LANES)
    e_tgt = jnp.where(e_rows % LANES < N_HEADS, (e_rows % LANES) * F_PARTS + e_rows // LANES, -1)
    e = (e_tgt[:, None] == jnp.arange(LANES)[None, :]).astype(BF16)

    x2 = x.reshape(T, D)
    xq, pf, qk, vt, fl, nrm = _in_proj(x2, row(norm_mix_g), w_main, w_vt, w_f, tm)
    yag, kf, ft, fend = _lru(xq, pf, fl, conv_w, row(conv_b), wd, row(0.5 * lru_ba), row(0.5 * lru_bx),
                             row(lru_lambda), fb, w_branch_a.astype(BF16), e, B, S, tm)
    assert tk == tm
    ob = _attn(qk, kf, vt, ft, nrm, fend, B, S, tq, tk, hp=2)
    out = _merge_mlp(ob, yag, pf, x2, w_branch_b.astype(BF16), w_out.astype(BF16),
                     row(norm_mlp_g), w_up.astype(BF16), w_down.astype(BF16),
                     row(norm_final_g), tm)
    return out.reshape(B, S, D)
```

```python
import functools
import math

import jax
import jax.numpy as jnp
from jax import lax
from jax.experimental import pallas as pl
from jax.experimental.pallas import tpu as pltpu

D_MODEL = 1024
N_HEADS = 8
HEAD_DIM = 128
LRU_BLOCKS = 16
LRU_BW = 64
CONV_W = 4
LRU_C = 8.0
D_FF = 4 * D_MODEL
RMS_EPS = 1e-6
LANES = 128
SUBLANES = 8
BF16_SUBLANES = 16
LOG2E = 1.4426950408889634
Q_SCALE = LOG2E / math.sqrt(HEAD_DIM)
VMEM_LIMIT = 56 * 1024 * 1024

BF16 = jnp.bfloat16
F32 = jnp.float32
_NT_DIMS = (((1,), (1,)), ((), ()))
W_IN_PF_SLABS = (0, 1, 5, 6)
W_IN_QK_SLABS = (2, 3)
W_IN_V_SLAB = 4
P_UNDERFLOW_LOG2 = 160.0
NORM_MARGIN = 1.02
F_PARTS = 3


def _dot(a, b):
    return jnp.dot(a, b, preferred_element_type=F32)


def _rms(x, g):
    return x * lax.rsqrt(jnp.mean(x * x, axis=-1, keepdims=True) + RMS_EPS) * g


def _softplus(x):
    return jnp.maximum(x, 0.0) + jnp.log1p(jnp.exp(-jnp.abs(x)))


def _sigmoid(x):
    return 0.5 * jnp.tanh(0.5 * x) + 0.5


def _gelu_tanh(x):
    c = math.sqrt(2.0 / math.pi)
    half_x = 0.5 * x
    return half_x + half_x * jnp.tanh(x * (c + (c * 0.044715) * (x * x)))


def _const_spec(shape):
    nd = len(shape)
    return pl.BlockSpec(shape, lambda *_: (0,) * nd, pipeline_mode=pl.Buffered(1))


def _in_proj_kernel(x_ref, g_ref, w_ref, wvt_ref, wf_ref, pf_ref, qk_ref, vt_ref, fl_ref,
                    nrm_ref):
    u = _rms(x_ref[...], g_ref[...]).astype(BF16)
    fl_ref[...] = _dot(u, wf_ref[...])
    post = (None, _gelu_tanh, _sigmoid, _sigmoid)
    for j, src in enumerate(W_IN_PF_SLABS):
        res = _dot(u, w_ref[:, src * D_MODEL:(src + 1) * D_MODEL])
        pf_ref[:, j * D_MODEL:(j + 1) * D_MODEL] = res if post[j] is None else post[j](res)
    for j, src in enumerate(W_IN_QK_SLABS):
        res = _dot(u, w_ref[:, src * D_MODEL:(src + 1) * D_MODEL])
        if j == 0:
            res = res * Q_SCALE
        lane = lax.broadcasted_iota(jnp.int32, (1, LANES), 1)
        nrm = jnp.zeros((1, LANES), F32)
        for h in range(N_HEADS):
            xb = res[:, h * HEAD_DIM:(h + 1) * HEAD_DIM].astype(BF16)
            qk_ref[j, h] = xb
            xf = xb.astype(F32)
            n2 = jnp.max(jnp.sum(xf * xf, axis=1, keepdims=True), axis=0, keepdims=True)
            nrm = jnp.where(lane == h, n2, nrm)
        nrm_ref[j:j + 1, :] = nrm
    vt = lax.dot_general(wvt_ref[...], u, _NT_DIMS, preferred_element_type=F32)
    for h in range(N_HEADS):
        vt_ref[h] = vt[h * HEAD_DIM:(h + 1) * HEAD_DIM, :].astype(BF16)


def _in_proj(x2, g, w_main, w_vt, w_f, tm):
    T = x2.shape[0]
    return pl.pallas_call(
        _in_proj_kernel,
        grid=(T // tm,),
        in_specs=[
            pl.BlockSpec((tm, D_MODEL), lambda i: (i, 0)),
            _const_spec((1, D_MODEL)),
            _const_spec(w_main.shape),
            _const_spec((D_MODEL, D_MODEL)),
            _const_spec((D_MODEL, LANES)),
        ],
        out_specs=[
            pl.BlockSpec((tm, 4 * D_MODEL), lambda i: (i, 0)),
            pl.BlockSpec((2, N_HEADS, tm, HEAD_DIM), lambda i: (0, 0, i, 0)),
            pl.BlockSpec((N_HEADS, HEAD_DIM, tm), lambda i: (0, 0, i)),
            pl.BlockSpec((tm, LANES), lambda i: (i, 0)),
            pl.BlockSpec((None, 2, LANES), lambda i: (i, 0, 0)),
        ],
        out_shape=[
            jax.ShapeDtypeStruct((T, 4 * D_MODEL), F32),
            jax.ShapeDtypeStruct((2, N_HEADS, T, HEAD_DIM), BF16),
            jax.ShapeDtypeStruct((N_HEADS, HEAD_DIM, T), BF16),
            jax.ShapeDtypeStruct((T, LANES), F32),
            jax.ShapeDtypeStruct((T // tm, 2, LANES), F32),
        ],
        compiler_params=pltpu.CompilerParams(
            dimension_semantics=("arbitrary",), vmem_limit_bytes=VMEM_LIMIT),
        name="in_proj",
    )(x2, g, w_main, w_vt, w_f)


def _lru_kernel(xl_ref, gl_ref, ga_ref, fl_ref, cw_ref, cb_ref, wd_ref, ba_ref, bx_ref,
                lam_ref, fb_ref, wa_ref, e_ref,
                yag_ref, kf_ref, ft_ref, fend_ref,
                xq_sc, a_sc, b_sc, h_sc, halo_sc, hc_sc, fc_sc, *, tm):
    s = pl.program_id(1)
    seg = tm // SUBLANES
    pitch = seg + SUBLANES
    gps = seg // SUBLANES
    nc = D_MODEL // LANES
    n_halo = CONV_W - 1

    @pl.when(s == 0)
    def _():
        halo_sc[...] = jnp.zeros_like(halo_sc)
        hc_sc[...] = jnp.zeros_like(hc_sc)
        fc_sc[...] = jnp.zeros_like(fc_sc)

    for g in range(tm // SUBLANES):
        u, j0 = g // gps, (g % gps) * SUBLANES
        for c in range(nc):
            xq_sc[c, pl.ds(j0 * SUBLANES + u, SUBLANES, stride=SUBLANES), :] = (
                xl_ref[g * SUBLANES:(g + 1) * SUBLANES, c * LANES:(c + 1) * LANES])

    sub = lax.broadcasted_iota(jnp.int32, (SUBLANES, LANES), 0)
    c_all = -LRU_C * _softplus(-lam_ref[...])
    for c in range(nc):
        cols = slice(c * LANES, (c + 1) * LANES)
        x = xq_sc[c]
        wrap = []
        for q in range(n_halo):
            rows = slice(q * SUBLANES, (q + 1) * SUBLANES)
            rolled = pltpu.roll(x[tm - n_halo * SUBLANES:, :][rows], 1, axis=0)
            wrap.append(jnp.where(sub == 0, halo_sc[c, rows, :], rolled))
            halo_sc[c, rows, :] = rolled
        xa = cb_ref[:, cols] + cw_ref[CONV_W - 1:CONV_W, cols] * x
        for d in range(1, CONV_W):
            xd = jnp.concatenate(wrap[n_halo - d:] + [x[:tm - d * SUBLANES, :]], axis=0)
            xa = xa + cw_ref[CONV_W - 1 - d:CONV_W - d, cols] * xd
        ri = _dot(xa.astype(BF16), wd_ref[c])
        t_r = jnp.tanh(0.5 * ri[:, :LANES] + 0.5 * ba_ref[:, cols])
        t_i = jnp.tanh(0.5 * ri[:, LANES:] + 0.5 * bx_ref[:, cols])
        half_c = 0.5 * c_all[:, cols]
        a = jnp.exp(half_c * t_r + half_c)
        half_xa = 0.5 * xa
        one_m_a2 = 1.0 - a * a
        mult = one_m_a2 * lax.rsqrt(jnp.maximum(one_m_a2, 1e-30))
        a_sc[c] = a
        b_sc[c] = mult * (half_xa * t_i + half_xa)

    def scan_step(j, carry):
        r0 = pl.multiple_of(j * SUBLANES, SUBLANES)
        out = []
        for c in range(nc):
            h, p = carry[c]
            a = a_sc[c, pl.ds(r0, SUBLANES), :]
            h = a * h + b_sc[c, pl.ds(r0, SUBLANES), :]
            p = a * p
            b_sc[c, pl.ds(r0, SUBLANES), :] = h
            a_sc[c, pl.ds(r0, SUBLANES), :] = p
            out.append((h, p))
        return tuple(out)

    ends = lax.fori_loop(
        0, seg, scan_step,
        tuple((jnp.zeros((SUBLANES, LANES), F32), jnp.ones((SUBLANES, LANES), F32))
              for _ in range(nc)), unroll=SUBLANES)

    carry_in = []
    for c in range(nc):
        cols = slice(c * LANES, (c + 1) * LANES)
        h_end, p_end = ends[c]
        for d in (1, 2, 4):
            keep = sub >= d
            p_sh = jnp.where(keep, pltpu.roll(p_end, d, axis=0), 1.0)
            h_sh = jnp.where(keep, pltpu.roll(h_end, d, axis=0), 0.0)
            h_end = p_end * h_sh + h_end
            p_end = p_end * p_sh
        blk_in = hc_sc[:, cols]
        seg_out = h_end + p_end * blk_in
        carry_in.append(jnp.where(sub == 0, blk_in, pltpu.roll(seg_out, 1, axis=0)))
        hc_sc[:, cols] = jnp.broadcast_to(seg_out[SUBLANES - 1:SUBLANES, :], (SUBLANES, LANES))

    def fix_step(j, _):
        r0 = pl.multiple_of(j * SUBLANES, SUBLANES)
        for c in range(nc):
            h = b_sc[c, pl.ds(r0, SUBLANES), :] + a_sc[c, pl.ds(r0, SUBLANES), :] * carry_in[c]
            h_sc[c, pl.ds(j, SUBLANES, stride=pitch), :] = h
        return 0

    lax.fori_loop(0, seg, fix_step, 0, unroll=SUBLANES)

    h = jnp.concatenate(
        [jnp.concatenate([h_sc[c, u * pitch:u * pitch + seg, :] for u in range(SUBLANES)], axis=0)
         for c in range(nc)], axis=1)
    y = (gl_ref[...] * h).astype(BF16)
    yag_ref[...] = ga_ref[...] * _dot(y, wa_ref[...])

    lf = -_softplus(-(fl_ref[...] + fb_ref[...])) * LOG2E
    rowf = lax.broadcasted_iota(jnp.int32, (tm, LANES), 0)
    d = 1
    while d < tm:
        lf = lf + jnp.where(rowf >= d, pltpu.roll(lf, d, axis=0), 0.0)
        d *= 2
    fblk = lf + fc_sc[...]
    fc_sc[...] = fblk[tm - 1:tm, :]
    ft_ref[...] = fblk.T[0:N_HEADS, :]
    fend_ref[...] = fblk[tm - 1:tm, :]
    hi = (-fblk).astype(BF16)
    r1 = -fblk - hi.astype(F32)
    mid = r1.astype(BF16)
    lo = (r1 - mid.astype(F32)).astype(BF16)
    kf_ref[...] = _dot(jnp.concatenate([hi, mid, lo], axis=1), e_ref[...]).astype(BF16)


def _lru(pf, fl, cw, cb, wd, ba, bx, lam, fb, wa, e, B, S, tm):
    T = B * S
    ns = S // tm
    row_blk = lambda c: pl.BlockSpec((tm, D_MODEL), lambda b, s, c=c: (b * ns + s, c))
    return pl.pallas_call(
        functools.partial(_lru_kernel, tm=tm),
        grid=(B, ns),
        in_specs=[
            row_blk(0), row_blk(1), row_blk(2),
            pl.BlockSpec((tm, LANES), lambda b, s: (b * ns + s, 0)),
            _const_spec((CONV_W, D_MODEL)),
            _const_spec((1, D_MODEL)),
            _const_spec((D_MODEL // LANES, LANES, 2 * LANES)),
            _const_spec((1, D_MODEL)),
            _const_spec((1, D_MODEL)),
            _const_spec((1, D_MODEL)),
            _const_spec((1, LANES)),
            _const_spec((D_MODEL, D_MODEL)),
            _const_spec((F_PARTS * LANES, LANES)),
        ],
        out_specs=[
            pl.BlockSpec((tm, D_MODEL), lambda b, s: (b * ns + s, 0)),
            pl.BlockSpec((tm, LANES), lambda b, s: (b * ns + s, 0)),
            pl.BlockSpec((None, N_HEADS, tm), lambda b, s: (b, 0, s)),
            pl.BlockSpec((None, 1, LANES), lambda b, s: (b * ns + s, 0, 0)),
        ],
        out_shape=[
            jax.ShapeDtypeStruct((T, D_MODEL), F32),
            jax.ShapeDtypeStruct((T, LANES), BF16),
            jax.ShapeDtypeStruct((B, N_HEADS, S), F32),
            jax.ShapeDtypeStruct((T // tm, 1, LANES), F32),
        ],
        scratch_shapes=[
            pltpu.VMEM((D_MODEL // LANES, tm, LANES), F32),
            pltpu.VMEM((D_MODEL // LANES, tm, LANES), F32),
            pltpu.VMEM((D_MODEL // LANES, tm, LANES), F32),
            pltpu.VMEM((D_MODEL // LANES, tm + SUBLANES * SUBLANES, LANES), F32),
            pltpu.VMEM((D_MODEL // LANES, (CONV_W - 1) * SUBLANES, LANES), F32),
            pltpu.VMEM((SUBLANES, D_MODEL), F32),
            pltpu.VMEM((1, LANES), F32),
        ],
        compiler_params=pltpu.CompilerParams(
            dimension_semantics=("arbitrary", "arbitrary"), vmem_limit_bytes=VMEM_LIMIT),
        name="lru",
    )(pf, pf, pf, fl, cw, cb, wd, ba, bx, lam, fb, wa, e)


def _col_reduce(op, z):
    rows, cols = z.shape
    return op(op(z.reshape(SUBLANES, rows // SUBLANES, cols), axis=0), axis=0, keepdims=True)


def _attn_kernel(q_ref, qn_ref, k_ref, kf_ref, k0n_ref, kf0n_ref, vt_ref, ft_ref, nrm_ref, fend_ref,
                 o_ref, z_sc, p_sc, acc_sc, m_sc, zmax_sc, skip_sm, *, tq, tk, hp, nq):
    hg = pl.program_id(1)
    qi = pl.program_id(2)
    n_blk = nrm_ref.shape[0]
    head_lane = lax.broadcasted_iota(jnp.int32, (1, LANES), 1)
    blk = lax.broadcasted_iota(jnp.int32, (n_blk, LANES), 0)

    def first_live_block(tile):
        q2 = jnp.maximum(nrm_ref[2 * tile, 0:1, :], nrm_ref[2 * tile + 1, 0:1, :])
        k2 = jnp.max(nrm_ref[:, 1, :], axis=0, keepdims=True)
        bound = (2.0 * NORM_MARGIN) * jnp.sqrt(q2 * k2) + fend_ref[jnp.maximum(2 * tile - 1, 0)]
        dead = (bound - fend_ref[:, 0, :] <= -P_UNDERFLOW_LOG2) & (blk < 2 * tile)
        first_live = jnp.min(jnp.where(dead, n_blk, blk), axis=0, keepdims=True)
        mine = (head_lane >= hg * hp) & (head_lane < (hg + 1) * hp)
        return jnp.min(jnp.where(mine, first_live, n_blk))

    lane = lax.broadcasted_iota(jnp.int32, (tq, LANES), 1)
    upper = slice(tk, tq)
    n_groups = pl.num_programs(0) * pl.num_programs(1)
    group_next = jnp.minimum(pl.program_id(0) * pl.num_programs(1) + hg + 1, n_groups - 1)
    last_tile = qi == nq - 1
    hg_next = jnp.where(last_tile, group_next % pl.num_programs(1), hg)

    def piece_ones(h):
        return jnp.where((lane >= F_PARTS * h) & (lane < F_PARTS * (h + 1)), 1.0, 0.0).astype(BF16)

    qa, qa_next, f_t = [], [], []
    for a in range(hp):
        h = hg * hp + a
        qa.append(jnp.concatenate([q_ref[a], piece_ones(h)], axis=1))
        qa_next.append(jnp.concatenate([qn_ref[a], piece_ones(hg_next * hp + a)], axis=1))
        f_t.append(ft_ref[pl.ds(h, 1), :])

    def qk(a, j, queries):
        c0 = pl.multiple_of(j * tk, tk)
        ka = jnp.concatenate([k_ref[a, pl.ds(c0, tk), :], kf_ref[pl.ds(c0, tk), :]], axis=1)
        return lax.dot_general(ka, queries, _NT_DIMS, preferred_element_type=F32)

    ones_rows = jnp.ones((BF16_SUBLANES, tk), BF16)

    def pv(a, j, p):
        c0 = pl.multiple_of(j * tk, tk)
        vt_aug = jnp.concatenate([vt_ref[a, :, pl.ds(c0, tk)], ones_rows], axis=0)
        return _dot(vt_aug, p)

    def softmax(z, m_row, zmax, f_row, masked):
        if masked:
            kr = lax.broadcasted_iota(jnp.int32, z.shape, 0)
            qc = lax.broadcasted_iota(jnp.int32, z.shape, 1)
            z = jnp.where(kr <= qc, z, -jnp.inf)
            zmax = _col_reduce(jnp.max, z)
        m_new = jnp.maximum(m_row, zmax + f_row)
        return m_new, jnp.exp2(m_row - m_new), jnp.exp2(z - (m_new - f_row)).astype(BF16)

    def trip(j, slot, masked=False):
        zmax_next = []
        for a in range(hp):
            if masked:
                z_sc[1 - slot, a, :, upper] = qk(a, j + 1, qa[a][upper])
            else:
                zn = qk(a, j + 1, qa[a])
                z_sc[1 - slot, a] = zn
                zmax_next.append(_col_reduce(jnp.max, zn))
        pvs = [pv(a, jnp.maximum(j - 1, 0), p_sc[a]) for a in range(hp)]
        for a in range(hp):
            if masked:
                z_lo = z_sc[slot, a, :, :tk]
                kr = lax.broadcasted_iota(jnp.int32, z_lo.shape, 0)
                qc = lax.broadcasted_iota(jnp.int32, z_lo.shape, 1)
                z_lo = jnp.where(kr <= qc, z_lo, -jnp.inf)
                zmax = jnp.concatenate([_col_reduce(jnp.max, z_lo), zmax_sc[a][:, upper]], axis=1)
                m_new = jnp.maximum(m_sc[a], zmax + f_t[a])
                shift = m_new - f_t[a]
                alpha = jnp.exp2(m_sc[a] - m_new)
                p_sc[a, :, :tk] = jnp.exp2(z_lo - shift[:, :tk]).astype(BF16)
                p_sc[a, :, upper] = jnp.exp2(z_sc[slot, a, :, upper] - shift[:, upper]).astype(BF16)
            else:
                m_new, alpha, p = softmax(z_sc[slot, a], m_sc[a], zmax_sc[a], f_t[a], False)
                p_sc[a] = p
                zmax_sc[a] = zmax_next[a]
            m_sc[a] = m_new
            acc_sc[a] = alpha * (acc_sc[a] + pvs[a])

    @pl.when((pl.program_id(0) == 0) & (hg == 0) & (qi == 0))
    def _():
        p_sc[...] = jnp.zeros_like(p_sc)
        skip_sm[0] = 0
        for a in range(hp):
            z0 = qk(a, 0, qa[a])
            z_sc[0, a] = z0
            zmax_sc[a] = _col_reduce(jnp.max, z0)

    for a in range(hp):
        m_sc[a] = jnp.full((1, tq), -jnp.inf, F32)
        acc_sc[a] = jnp.zeros((HEAD_DIM + BF16_SUBLANES, tq), F32)

    start = skip_sm[0]

    @pl.when((start & 1) == 1)
    def _():
        trip(start, 1)

    @pl.loop((start + 1) >> 1, qi)
    def _(i):
        trip(2 * i, 0)
        trip(2 * i + 1, 1)

    first = 2 * qi
    trip(first, 0, masked=True)
    for a in range(hp):
        acc = acc_sc[a] + pv(a, first, p_sc[a])
        _, alpha, p = softmax(z_sc[1, a, :, upper], m_sc[a][:, upper], None, f_t[a][:, upper], True)
        acc_up = alpha * acc[:, upper] + pv(a, first + 1, p)
        out = jnp.concatenate(
            [acc[:HEAD_DIM, :tk] / acc[HEAD_DIM:HEAD_DIM + 1, :tk],
             acc_up[:HEAD_DIM] / acc_up[HEAD_DIM:HEAD_DIM + 1]], axis=1)
        o_ref[a] = out.T.astype(BF16)

    start_next = jnp.where(last_tile, 0, first_live_block(jnp.minimum(qi + 1, nq - 1)))
    skip_sm[0] = start_next
    c0 = pl.multiple_of(start_next * tk, tk)
    for a in range(hp):
        ka_here = jnp.concatenate([k_ref[a, pl.ds(c0, tk), :], kf_ref[pl.ds(c0, tk), :]], axis=1)
        ka_group = jnp.concatenate([k0n_ref[a], kf0n_ref[...]], axis=1)
        zn = lax.dot_general(jnp.where(last_tile, ka_group, ka_here), qa_next[a], _NT_DIMS,
                             preferred_element_type=F32)
        z_sc[start_next & 1, a] = zn
        zmax_sc[a] = _col_reduce(jnp.max, zn)


def _attn(qk, kf, vt, ft, nrm, fend, B, S, tq, tk, hp):
    T = B * S
    nq = S // tq
    n_blk = S // tk
    n_grp = N_HEADS // hp
    assert tq == 2 * tk and nrm.shape[0] == B * n_blk and fend.shape[0] == B * n_blk

    def next_group(b, g):
        gn = jnp.minimum(b * n_grp + g + 1, B * n_grp - 1)
        return gn // n_grp, gn % n_grp

    def q_next(b, g, i):
        bn, gn = next_group(b, g)
        last = i == nq - 1
        return (0, jnp.where(last, gn, g), jnp.where(last, bn * nq, b * nq + i + 1), 0)
    return pl.pallas_call(
        functools.partial(_attn_kernel, tq=tq, tk=tk, hp=hp, nq=nq),
        grid=(B, N_HEADS // hp, nq),
        in_specs=[
            pl.BlockSpec((None, hp, tq, HEAD_DIM), lambda b, g, i: (0, g, b * nq + i, 0)),
            pl.BlockSpec((None, hp, tq, HEAD_DIM), q_next),
            pl.BlockSpec((None, hp, S, HEAD_DIM), lambda b, g, i: (1, g, b, 0)),
            pl.BlockSpec((S, LANES), lambda b, g, i: (b, 0)),
            pl.BlockSpec((None, hp, tk, HEAD_DIM),
                         lambda b, g, i: (1, next_group(b, g)[1], next_group(b, g)[0] * n_blk, 0)),
            pl.BlockSpec((tk, LANES), lambda b, g, i: (next_group(b, g)[0] * n_blk, 0)),
            pl.BlockSpec((hp, HEAD_DIM, S), lambda b, g, i: (g, 0, b)),
            pl.BlockSpec((None, N_HEADS, tq), lambda b, g, i: (b, 0, i)),
            pl.BlockSpec((n_blk, 2, LANES), lambda b, g, i: (b, 0, 0)),
            pl.BlockSpec((n_blk, 1, LANES), lambda b, g, i: (b, 0, 0)),
        ],
        out_specs=pl.BlockSpec((hp, tq, HEAD_DIM), lambda b, g, i: (g, b * nq + i, 0)),
        out_shape=jax.ShapeDtypeStruct((N_HEADS, T, HEAD_DIM), BF16),
        scratch_shapes=[
            pltpu.VMEM((2, hp, tk, tq), F32),
            pltpu.VMEM((hp, tk, tq), BF16),
            pltpu.VMEM((hp, HEAD_DIM + BF16_SUBLANES, tq), F32),
            pltpu.VMEM((hp, 1, tq), F32),
            pltpu.VMEM((hp, 1, tq), F32),
            pltpu.SMEM((1,), jnp.int32),
        ],
        compiler_params=pltpu.CompilerParams(
            dimension_semantics=("arbitrary", "arbitrary", "arbitrary"),
            vmem_limit_bytes=VMEM_LIMIT),
        name="fox_attn",
    )(qk, qk, qk, kf, qk, kf, vt, ft, nrm, fend)


def _merge_mlp_kernel(ob_ref, yag_ref, gb_ref, x_ref, wb_ref, wo_ref, gm_ref, wu_ref,
                      wdn_ref, gf_ref, o_ref):
    ob = jnp.concatenate([ob_ref[h] for h in range(N_HEADS)], axis=1)
    yb = _dot(ob, wb_ref[...])
    mix = yag_ref[...] + gb_ref[...] * yb
    x1 = x_ref[...] + _dot(mix.astype(BF16), wo_ref[...])
    m = _rms(x1, gm_ref[...]).astype(BF16)
    acc = x1
    for c in range(D_FF // D_MODEL):
        cols = slice(c * D_MODEL, (c + 1) * D_MODEL)
        hc = jnp.maximum(_dot(m, wu_ref[:, cols]), 0.0)
        acc = acc + _dot((hc * hc).astype(BF16), wdn_ref[cols, :])
    o_ref[...] = _rms(acc, gf_ref[...])


def _merge_mlp(ob, yag, pf, x2, wb, wo, gm, wu, wdn, gf, tm):
    T = x2.shape[0]
    return pl.pallas_call(
        _merge_mlp_kernel,
        grid=(T // tm,),
        in_specs=[
            pl.BlockSpec((N_HEADS, tm, HEAD_DIM), lambda i: (0, i, 0)),
            pl.BlockSpec((tm, D_MODEL), lambda i: (i, 0)),
            pl.BlockSpec((tm, D_MODEL), lambda i: (i, 3)),
            pl.BlockSpec((tm, D_MODEL), lambda i: (i, 0)),
            _const_spec((D_MODEL, D_MODEL)),
            _const_spec((D_MODEL, D_MODEL)),
            _const_spec((1, D_MODEL)),
            _const_spec((D_MODEL, D_FF)),
            _const_spec((D_FF, D_MODEL)),
            _const_spec((1, D_MODEL)),
        ],
        out_specs=pl.BlockSpec((tm, D_MODEL), lambda i: (i, 0)),
        out_shape=jax.ShapeDtypeStruct((T, D_MODEL), F32),
        compiler_params=pltpu.CompilerParams(
            dimension_semantics=("arbitrary",), vmem_limit_bytes=VMEM_LIMIT),
        name="merge_mlp",
    )(ob, yag, pf, x2, wb, wo, gm, wu, wdn, gf)


def _block_diag_pairs(wa, wx):
    def pair(w):
        w = w.reshape(LRU_BLOCKS // 2, 2, LRU_BW, LRU_BW)
        z = jnp.zeros_like(w[:, 0])
        top = jnp.concatenate([w[:, 0], z], axis=2)
        bot = jnp.concatenate([z, w[:, 1]], axis=2)
        return jnp.concatenate([top, bot], axis=1)
    return jnp.concatenate([pair(wa), pair(wx)], axis=2).astype(BF16)


def kernel(x, norm_mix_g, w_in, conv_w, conv_b, lru_wa, lru_ba, lru_wx, lru_bx, lru_lambda,
           forget_b, w_branch_a, w_branch_b, w_out, norm_mlp_g, w_up, w_down, norm_final_g):
    B, S, D = x.shape
    assert D == D_MODEL
    T = B * S
    tm = min(512, S)
    tq = min(1024, S)
    tk = tq // 2
    assert S % tm == 0 and S % tq == 0 and tm % LANES == 0

    w_main = w_in.astype(BF16)
    w_vt = w_in[:, W_IN_V_SLAB * D:(W_IN_V_SLAB + 1) * D].T.astype(BF16)
    w_f = jnp.pad(w_in[:, 7 * D:], ((0, 0), (0, LANES - N_HEADS))).astype(BF16)
    wd = _block_diag_pairs(lru_wa, lru_wx)
    row = lambda v: v.reshape(1, -1).astype(F32)
    fb = jnp.pad(forget_b, (0, LANES - N_HEADS)).reshape(1, LANES)
    e_rows = jnp.arange(F_PARTS * LANES)
    e_tgt = jnp.where(e_rows % LANES < N_HEADS, (e_rows % LANES) * F_PARTS + e_rows // LANES, -1)
    e = (e_tgt[:, None] == jnp.arange(LANES)[None, :]).astype(BF16)

    x2 = x.reshape(T, D)
    pf, qk, vt, fl, nrm = _in_proj(x2, row(norm_mix_g), w_main, w_vt, w_f, tm)
    yag, kf, ft, fend = _lru(pf, fl, conv_w, row(conv_b), wd, row(lru_ba), row(lru_bx),
                             row(lru_lambda), fb, w_branch_a.astype(BF16), e, B, S, tm)
    assert tk == tm
    ob = _attn(qk, kf, vt, ft, nrm, fend, B, S, tq, tk, hp=2)
    out = _merge_mlp(ob, yag, pf, x2, w_branch_b.astype(BF16), w_out.astype(BF16),
                     row(norm_mlp_g), w_up.astype(BF16), w_down.astype(BF16),
                     row(norm_final_g), tm)
    return out.reshape(B, S, D)
```

```python
import functools
import math

import jax
import jax.numpy as jnp
from jax import lax
from jax.experimental import pallas as pl
from jax.experimental.pallas import tpu as pltpu

D_MODEL = 1024
N_HEADS = 8
HEAD_DIM = 128
LRU_BLOCKS = 16
LRU_BW = 64
CONV_W = 4
LRU_C = 8.0
D_FF = 4 * D_MODEL
RMS_EPS = 1e-6
LANES = 128
SUBLANES = 8
BF16_SUBLANES = 16
LOG2E = 1.4426950408889634
Q_SCALE = LOG2E / math.sqrt(HEAD_DIM)
VMEM_LIMIT = 56 * 1024 * 1024

BF16 = jnp.bfloat16
F32 = jnp.float32
_NT_DIMS = (((1,), (1,)), ((), ()))
W_IN_PF_SLABS = (0, 1, 5, 6)
W_IN_QK_SLABS = (2, 3)
W_IN_V_SLAB = 4
P_UNDERFLOW_LOG2 = 160.0
NORM_MARGIN = 1.02
F_PARTS = 3


def _dot(a, b):
    return jnp.dot(a, b, preferred_element_type=F32)


def _rms(x, g):
    return x * lax.rsqrt(jnp.mean(x * x, axis=-1, keepdims=True) + RMS_EPS) * g


def _softplus(x):
    return jnp.maximum(x, 0.0) + jnp.log1p(jnp.exp(-jnp.abs(x)))


def _sigmoid(x):
    return 0.5 * jnp.tanh(0.5 * x) + 0.5


def _gelu_tanh(x):
    c = math.sqrt(2.0 / math.pi)
    half_x = 0.5 * x
    return half_x + half_x * jnp.tanh(x * (c + (c * 0.044715) * (x * x)))


def _const_spec(shape):
    nd = len(shape)
    return pl.BlockSpec(shape, lambda *_: (0,) * nd, pipeline_mode=pl.Buffered(1))


def _in_proj_kernel(x_ref, g_ref, w_ref, wvt_ref, wf_ref, pf_ref, qk_ref, vt_ref, fl_ref,
                    nrm_ref):
    u = _rms(x_ref[...], g_ref[...]).astype(BF16)
    fl_ref[...] = _dot(u, wf_ref[...])
    post = (None, _gelu_tanh, _sigmoid, _sigmoid)
    for j, src in enumerate(W_IN_PF_SLABS):
        res = _dot(u, w_ref[:, src * D_MODEL:(src + 1) * D_MODEL])
        pf_ref[:, j * D_MODEL:(j + 1) * D_MODEL] = res if post[j] is None else post[j](res)
    for j, src in enumerate(W_IN_QK_SLABS):
        res = _dot(u, w_ref[:, src * D_MODEL:(src + 1) * D_MODEL])
        if j == 0:
            res = res * Q_SCALE
        lane = lax.broadcasted_iota(jnp.int32, (1, LANES), 1)
        nrm = jnp.zeros((1, LANES), F32)
        for h in range(N_HEADS):
            xb = res[:, h * HEAD_DIM:(h + 1) * HEAD_DIM].astype(BF16)
            qk_ref[j, h] = xb
            xf = xb.astype(F32)
            n2 = jnp.max(jnp.sum(xf * xf, axis=1, keepdims=True), axis=0, keepdims=True)
            nrm = jnp.where(lane == h, n2, nrm)
        nrm_ref[j:j + 1, :] = nrm
    vt = lax.dot_general(wvt_ref[...], u, _NT_DIMS, preferred_element_type=F32)
    for h in range(N_HEADS):
        vt_ref[h] = vt[h * HEAD_DIM:(h + 1) * HEAD_DIM, :].astype(BF16)


def _in_proj(x2, g, w_main, w_vt, w_f, tm):
    T = x2.shape[0]
    return pl.pallas_call(
        _in_proj_kernel,
        grid=(T // tm,),
        in_specs=[
            pl.BlockSpec((tm, D_MODEL), lambda i: (i, 0)),
            _const_spec((1, D_MODEL)),
            _const_spec(w_main.shape),
            _const_spec((D_MODEL, D_MODEL)),
            _const_spec((D_MODEL, LANES)),
        ],
        out_specs=[
            pl.BlockSpec((tm, 4 * D_MODEL), lambda i: (i, 0)),
            pl.BlockSpec((2, N_HEADS, tm, HEAD_DIM), lambda i: (0, 0, i, 0)),
            pl.BlockSpec((N_HEADS, HEAD_DIM, tm), lambda i: (0, 0, i)),
            pl.BlockSpec((tm, LANES), lambda i: (i, 0)),
            pl.BlockSpec((None, 2, LANES), lambda i: (i, 0, 0)),
        ],
        out_shape=[
            jax.ShapeDtypeStruct((T, 4 * D_MODEL), F32),
            jax.ShapeDtypeStruct((2, N_HEADS, T, HEAD_DIM), BF16),
            jax.ShapeDtypeStruct((N_HEADS, HEAD_DIM, T), BF16),
            jax.ShapeDtypeStruct((T, LANES), F32),
            jax.ShapeDtypeStruct((T // tm, 2, LANES), F32),
        ],
        compiler_params=pltpu.CompilerParams(
            dimension_semantics=("arbitrary",), vmem_limit_bytes=VMEM_LIMIT),
        name="in_proj",
    )(x2, g, w_main, w_vt, w_f)


def _lru_kernel(xl_ref, gl_ref, ga_ref, fl_ref, cw_ref, cb_ref, wd_ref, ba_ref, bx_ref,
                lam_ref, fb_ref, wa_ref, e_ref,
                yag_ref, kf_ref, ft_ref, fend_ref,
                xq_sc, a_sc, b_sc, h_sc, halo_sc, hc_sc, fc_sc, *, tm):
    s = pl.program_id(1)
    seg = tm // SUBLANES
    pitch = seg + SUBLANES
    gps = seg // SUBLANES
    nc = D_MODEL // LANES
    n_halo = CONV_W - 1

    @pl.when(s == 0)
    def _():
        halo_sc[...] = jnp.zeros_like(halo_sc)
        hc_sc[...] = jnp.zeros_like(hc_sc)
        fc_sc[...] = jnp.zeros_like(fc_sc)

    for g in range(tm // SUBLANES):
        u, j0 = g // gps, (g % gps) * SUBLANES
        for c in range(nc):
            xq_sc[c, pl.ds(j0 * SUBLANES + u, SUBLANES, stride=SUBLANES), :] = (
                xl_ref[g * SUBLANES:(g + 1) * SUBLANES, c * LANES:(c + 1) * LANES])

    sub = lax.broadcasted_iota(jnp.int32, (SUBLANES, LANES), 0)
    c_all = -LRU_C * _softplus(-lam_ref[...])
    for c in range(nc):
        cols = slice(c * LANES, (c + 1) * LANES)
        x = xq_sc[c]
        wrap = []
        for q in range(n_halo):
            rows = slice(q * SUBLANES, (q + 1) * SUBLANES)
            rolled = pltpu.roll(x[tm - n_halo * SUBLANES:, :][rows], 1, axis=0)
            wrap.append(jnp.where(sub == 0, halo_sc[c, rows, :], rolled))
            halo_sc[c, rows, :] = rolled
        xa = cb_ref[:, cols] + cw_ref[CONV_W - 1:CONV_W, cols] * x
        for d in range(1, CONV_W):
            xd = jnp.concatenate(wrap[n_halo - d:] + [x[:tm - d * SUBLANES, :]], axis=0)
            xa = xa + cw_ref[CONV_W - 1 - d:CONV_W - d, cols] * xd
        ri = _dot(xa.astype(BF16), wd_ref[c])
        t_r = jnp.tanh(ri[:, :LANES] + ba_ref[:, cols])
        t_i = jnp.tanh(ri[:, LANES:] + bx_ref[:, cols])
        half_c = 0.5 * c_all[:, cols]
        a = jnp.exp(half_c * t_r + half_c)
        half_xa = 0.5 * xa
        one_m_a2 = 1.0 - a * a
        mult = one_m_a2 * lax.rsqrt(jnp.maximum(one_m_a2, 1e-30))
        a_sc[c] = a
        b_sc[c] = mult * (half_xa * t_i + half_xa)

    def scan_step(j, carry):
        r0 = pl.multiple_of(j * SUBLANES, SUBLANES)
        out = []
        for c in range(nc):
            h, p = carry[c]
            a = a_sc[c, pl.ds(r0, SUBLANES), :]
            out.append((a * h + b_sc[c, pl.ds(r0, SUBLANES), :], a * p))
        return tuple(out)

    ends = lax.fori_loop(
        0, seg, scan_step,
        tuple((jnp.zeros((SUBLANES, LANES), F32), jnp.ones((SUBLANES, LANES), F32))
              for _ in range(nc)), unroll=SUBLANES)

    carry_in = []
    for c in range(nc):
        cols = slice(c * LANES, (c + 1) * LANES)
        h_end, p_end = ends[c]
        for d in (1, 2, 4):
            keep = sub >= d
            p_sh = jnp.where(keep, pltpu.roll(p_end, d, axis=0), 1.0)
            h_sh = jnp.where(keep, pltpu.roll(h_end, d, axis=0), 0.0)
            h_end = p_end * h_sh + h_end
            p_end = p_end * p_sh
        blk_in = hc_sc[:, cols]
        seg_out = h_end + p_end * blk_in
        carry_in.append(jnp.where(sub == 0, blk_in, pltpu.roll(seg_out, 1, axis=0)))
        hc_sc[:, cols] = jnp.broadcast_to(seg_out[SUBLANES - 1:SUBLANES, :], (SUBLANES, LANES))

    def final_step(j, carry):
        r0 = pl.multiple_of(j * SUBLANES, SUBLANES)
        out = []
        for c in range(nc):
            h = a_sc[c, pl.ds(r0, SUBLANES), :] * carry[c] + b_sc[c, pl.ds(r0, SUBLANES), :]
            h_sc[c, pl.ds(j, SUBLANES, stride=pitch), :] = h
            out.append(h)
        return tuple(out)

    lax.fori_loop(0, seg, final_step, tuple(carry_in), unroll=SUBLANES)

    h = jnp.concatenate(
        [jnp.concatenate([h_sc[c, u * pitch:u * pitch + seg, :] for u in range(SUBLANES)], axis=0)
         for c in range(nc)], axis=1)
    y = (gl_ref[...] * h).astype(BF16)
    yag_ref[...] = ga_ref[...] * _dot(y, wa_ref[...])

    lf = -_softplus(-(fl_ref[...] + fb_ref[...])) * LOG2E
    rowf = lax.broadcasted_iota(jnp.int32, (tm, LANES), 0)
    d = 1
    while d < tm:
        lf = lf + jnp.where(rowf >= d, pltpu.roll(lf, d, axis=0), 0.0)
        d *= 2
    fblk = lf + fc_sc[...]
    fc_sc[...] = fblk[tm - 1:tm, :]
    ft_ref[...] = fblk.T[0:N_HEADS, :]
    fend_ref[...] = fblk[tm - 1:tm, :]
    hi = (-fblk).astype(BF16)
    r1 = -fblk - hi.astype(F32)
    mid = r1.astype(BF16)
    lo = (r1 - mid.astype(F32)).astype(BF16)
    kf_ref[...] = _dot(jnp.concatenate([hi, mid, lo], axis=1), e_ref[...]).astype(BF16)


def _lru(pf, fl, cw, cb, wd, ba, bx, lam, fb, wa, e, B, S, tm):
    T = B * S
    ns = S // tm
    row_blk = lambda c: pl.BlockSpec((tm, D_MODEL), lambda b, s, c=c: (b * ns + s, c))
    return pl.pallas_call(
        functools.partial(_lru_kernel, tm=tm),
        grid=(B, ns),
        in_specs=[
            row_blk(0), row_blk(1), row_blk(2),
            pl.BlockSpec((tm, LANES), lambda b, s: (b * ns + s, 0)),
            _const_spec((CONV_W, D_MODEL)),
            _const_spec((1, D_MODEL)),
            _const_spec((D_MODEL // LANES, LANES, 2 * LANES)),
            _const_spec((1, D_MODEL)),
            _const_spec((1, D_MODEL)),
            _const_spec((1, D_MODEL)),
            _const_spec((1, LANES)),
            _const_spec((D_MODEL, D_MODEL)),
            _const_spec((F_PARTS * LANES, LANES)),
        ],
        out_specs=[
            pl.BlockSpec((tm, D_MODEL), lambda b, s: (b * ns + s, 0)),
            pl.BlockSpec((tm, LANES), lambda b, s: (b * ns + s, 0)),
            pl.BlockSpec((None, N_HEADS, tm), lambda b, s: (b, 0, s)),
            pl.BlockSpec((None, 1, LANES), lambda b, s: (b * ns + s, 0, 0)),
        ],
        out_shape=[
            jax.ShapeDtypeStruct((T, D_MODEL), F32),
            jax.ShapeDtypeStruct((T, LANES), BF16),
            jax.ShapeDtypeStruct((B, N_HEADS, S), F32),
            jax.ShapeDtypeStruct((T // tm, 1, LANES), F32),
        ],
        scratch_shapes=[
            pltpu.VMEM((D_MODEL // LANES, tm, LANES), F32),
            pltpu.VMEM((D_MODEL // LANES, tm, LANES), F32),
            pltpu.VMEM((D_MODEL // LANES, tm, LANES), F32),
            pltpu.VMEM((D_MODEL // LANES, tm + SUBLANES * SUBLANES, LANES), F32),
            pltpu.VMEM((D_MODEL // LANES, (CONV_W - 1) * SUBLANES, LANES), F32),
            pltpu.VMEM((SUBLANES, D_MODEL), F32),
            pltpu.VMEM((1, LANES), F32),
        ],
        compiler_params=pltpu.CompilerParams(
            dimension_semantics=("arbitrary", "arbitrary"), vmem_limit_bytes=VMEM_LIMIT),
        name="lru",
    )(pf, pf, pf, fl, cw, cb, wd, ba, bx, lam, fb, wa, e)


def _col_reduce(op, z):
    rows, cols = z.shape
    return op(op(z.reshape(SUBLANES, rows // SUBLANES, cols), axis=0), axis=0, keepdims=True)


def _attn_kernel(q_ref, qn_ref, k_ref, kf_ref, k0n_ref, kf0n_ref, vt_ref, ft_ref, nrm_ref, fend_ref,
                 o_ref, z_sc, p_sc, acc_sc, m_sc, zmax_sc, skip_sm, *, tq, tk, hp, nq):
    hg = pl.program_id(1)
    qi = pl.program_id(2)
    n_blk = nrm_ref.shape[0]
    head_lane = lax.broadcasted_iota(jnp.int32, (1, LANES), 1)
    blk = lax.broadcasted_iota(jnp.int32, (n_blk, LANES), 0)

    def first_live_block(tile):
        q2 = jnp.maximum(nrm_ref[2 * tile, 0:1, :], nrm_ref[2 * tile + 1, 0:1, :])
        k2 = jnp.max(nrm_ref[:, 1, :], axis=0, keepdims=True)
        bound = (2.0 * NORM_MARGIN) * jnp.sqrt(q2 * k2) + fend_ref[jnp.maximum(2 * tile - 1, 0)]
        dead = (bound - fend_ref[:, 0, :] <= -P_UNDERFLOW_LOG2) & (blk < 2 * tile)
        first_live = jnp.min(jnp.where(dead, n_blk, blk), axis=0, keepdims=True)
        mine = (head_lane >= hg * hp) & (head_lane < (hg + 1) * hp)
        return jnp.min(jnp.where(mine, first_live, n_blk))

    lane = lax.broadcasted_iota(jnp.int32, (tq, LANES), 1)
    upper = slice(tk, tq)
    n_groups = pl.num_programs(0) * pl.num_programs(1)
    group_next = jnp.minimum(pl.program_id(0) * pl.num_programs(1) + hg + 1, n_groups - 1)
    last_tile = qi == nq - 1
    hg_next = jnp.where(last_tile, group_next % pl.num_programs(1), hg)

    def piece_ones(h):
        return jnp.where((lane >= F_PARTS * h) & (lane < F_PARTS * (h + 1)), 1.0, 0.0).astype(BF16)

    qa, qa_next, f_t = [], [], []
    for a in range(hp):
        h = hg * hp + a
        qa.append(jnp.concatenate([q_ref[a], piece_ones(h)], axis=1))
        qa_next.append(jnp.concatenate([qn_ref[a], piece_ones(hg_next * hp + a)], axis=1))
        f_t.append(ft_ref[pl.ds(h, 1), :])

    def qk(a, j, queries):
        c0 = pl.multiple_of(j * tk, tk)
        ka = jnp.concatenate([k_ref[a, pl.ds(c0, tk), :], kf_ref[pl.ds(c0, tk), :]], axis=1)
        return lax.dot_general(ka, queries, _NT_DIMS, preferred_element_type=F32)

    ones_rows = jnp.ones((BF16_SUBLANES, tk), BF16)

    def pv(a, j, p):
        c0 = pl.multiple_of(j * tk, tk)
        vt_aug = jnp.concatenate([vt_ref[a, :, pl.ds(c0, tk)], ones_rows], axis=0)
        return _dot(vt_aug, p)

    def softmax(z, m_row, zmax, f_row, masked):
        if masked:
            kr = lax.broadcasted_iota(jnp.int32, z.shape, 0)
            qc = lax.broadcasted_iota(jnp.int32, z.shape, 1)
            z = jnp.where(kr <= qc, z, -jnp.inf)
            zmax = _col_reduce(jnp.max, z)
        m_new = jnp.maximum(m_row, zmax + f_row)
        return m_new, jnp.exp2(m_row - m_new), jnp.exp2(z - (m_new - f_row)).astype(BF16)

    def trip(j, slot, masked=False):
        zmax_next = []
        for a in range(hp):
            if masked:
                z_sc[1 - slot, a, :, upper] = qk(a, j + 1, qa[a][upper])
            else:
                zn = qk(a, j + 1, qa[a])
                z_sc[1 - slot, a] = zn
                zmax_next.append(_col_reduce(jnp.max, zn))
        pvs = [pv(a, jnp.maximum(j - 1, 0), p_sc[a]) for a in range(hp)]
        for a in range(hp):
            if masked:
                z_lo = z_sc[slot, a, :, :tk]
                kr = lax.broadcasted_iota(jnp.int32, z_lo.shape, 0)
                qc = lax.broadcasted_iota(jnp.int32, z_lo.shape, 1)
                z_lo = jnp.where(kr <= qc, z_lo, -jnp.inf)
                zmax = jnp.concatenate([_col_reduce(jnp.max, z_lo), zmax_sc[a][:, upper]], axis=1)
                m_new = jnp.maximum(m_sc[a], zmax + f_t[a])
                shift = m_new - f_t[a]
                alpha = jnp.exp2(m_sc[a] - m_new)
                p_sc[a, :, :tk] = jnp.exp2(z_lo - shift[:, :tk]).astype(BF16)
                p_sc[a, :, upper] = jnp.exp2(z_sc[slot, a, :, upper] - shift[:, upper]).astype(BF16)
            else:
                m_new, alpha, p = softmax(z_sc[slot, a], m_sc[a], zmax_sc[a], f_t[a], False)
                p_sc[a] = p
                zmax_sc[a] = zmax_next[a]
            m_sc[a] = m_new
            acc_sc[a] = alpha * (acc_sc[a] + pvs[a])

    @pl.when((pl.program_id(0) == 0) & (hg == 0) & (qi == 0))
    def _():
        p_sc[...] = jnp.zeros_like(p_sc)
        skip_sm[0] = 0
        for a in range(hp):
            z0 = qk(a, 0, qa[a])
            z_sc[0, a] = z0
            zmax_sc[a] = _col_reduce(jnp.max, z0)

    for a in range(hp):
        m_sc[a] = jnp.full((1, tq), -jnp.inf, F32)
        acc_sc[a] = jnp.zeros((HEAD_DIM + BF16_SUBLANES, tq), F32)

    start = skip_sm[0]

    @pl.when((start & 1) == 1)
    def _():
        trip(start, 1)

    @pl.loop((start + 1) >> 1, qi)
    def _(i):
        trip(2 * i, 0)
        trip(2 * i + 1, 1)

    first = 2 * qi
    trip(first, 0, masked=True)
    for a in range(hp):
        acc = acc_sc[a] + pv(a, first, p_sc[a])
        _, alpha, p = softmax(z_sc[1, a, :, upper], m_sc[a][:, upper], None, f_t[a][:, upper], True)
        acc_up = alpha * acc[:, upper] + pv(a, first + 1, p)
        out = jnp.concatenate(
            [acc[:HEAD_DIM, :tk] / acc[HEAD_DIM:HEAD_DIM + 1, :tk],
             acc_up[:HEAD_DIM] / acc_up[HEAD_DIM:HEAD_DIM + 1]], axis=1)
        o_ref[a] = out.T.astype(BF16)

    start_next = jnp.where(last_tile, 0, first_live_block(jnp.minimum(qi + 1, nq - 1)))
    skip_sm[0] = start_next
    c0 = pl.multiple_of(start_next * tk, tk)
    for a in range(hp):
        ka_here = jnp.concatenate([k_ref[a, pl.ds(c0, tk), :], kf_ref[pl.ds(c0, tk), :]], axis=1)
        ka_group = jnp.concatenate([k0n_ref[a], kf0n_ref[...]], axis=1)
        zn = lax.dot_general(jnp.where(last_tile, ka_group, ka_here), qa_next[a], _NT_DIMS,
                             preferred_element_type=F32)
        z_sc[start_next & 1, a] = zn
        zmax_sc[a] = _col_reduce(jnp.max, zn)


def _attn(qk, kf, vt, ft, nrm, fend, B, S, tq, tk, hp):
    T = B * S
    nq = S // tq
    n_blk = S // tk
    n_grp = N_HEADS // hp
    assert tq == 2 * tk and nrm.shape[0] == B * n_blk and fend.shape[0] == B * n_blk

    def next_group(b, g):
        gn = jnp.minimum(b * n_grp + g + 1, B * n_grp - 1)
        return gn // n_grp, gn % n_grp

    def q_next(b, g, i):
        bn, gn = next_group(b, g)
        last = i == nq - 1
        return (0, jnp.where(last, gn, g), jnp.where(last, bn * nq, b * nq + i + 1), 0)
    return pl.pallas_call(
        functools.partial(_attn_kernel, tq=tq, tk=tk, hp=hp, nq=nq),
        grid=(B, N_HEADS // hp, nq),
        in_specs=[
            pl.BlockSpec((None, hp, tq, HEAD_DIM), lambda b, g, i: (0, g, b * nq + i, 0)),
            pl.BlockSpec((None, hp, tq, HEAD_DIM), q_next),
            pl.BlockSpec((None, hp, S, HEAD_DIM), lambda b, g, i: (1, g, b, 0)),
            pl.BlockSpec((S, LANES), lambda b, g, i: (b, 0)),
            pl.BlockSpec((None, hp, tk, HEAD_DIM),
                         lambda b, g, i: (1, next_group(b, g)[1], next_group(b, g)[0] * n_blk, 0)),
            pl.BlockSpec((tk, LANES), lambda b, g, i: (next_group(b, g)[0] * n_blk, 0)),
            pl.BlockSpec((hp, HEAD_DIM, S), lambda b, g, i: (g, 0, b)),
            pl.BlockSpec((None, N_HEADS, tq), lambda b, g, i: (b, 0, i)),
            pl.BlockSpec((n_blk, 2, LANES), lambda b, g, i: (b, 0, 0)),
            pl.BlockSpec((n_blk, 1, LANES), lambda b, g, i: (b, 0, 0)),
        ],
        out_specs=pl.BlockSpec((hp, tq, HEAD_DIM), lambda b, g, i: (g, b * nq + i, 0)),
        out_shape=jax.ShapeDtypeStruct((N_HEADS, T, HEAD_DIM), BF16),
        scratch_shapes=[
            pltpu.VMEM((2, hp, tk, tq), F32),
            pltpu.VMEM((hp, tk, tq), BF16),
            pltpu.VMEM((hp, HEAD_DIM + BF16_SUBLANES, tq), F32),
            pltpu.VMEM((hp, 1, tq), F32),
            pltpu.VMEM((hp, 1, tq), F32),
            pltpu.SMEM((1,), jnp.int32),
        ],
        compiler_params=pltpu.CompilerParams(
            dimension_semantics=("arbitrary", "arbitrary", "arbitrary"),
            vmem_limit_bytes=VMEM_LIMIT),
        name="fox_attn",
    )(qk, qk, qk, kf, qk, kf, vt, ft, nrm, fend)


def _merge_mlp_kernel(ob_ref, yag_ref, gb_ref, x_ref, wb_ref, wo_ref, gm_ref, wu_ref,
                      wdn_ref, gf_ref, o_ref):
    ob = jnp.concatenate([ob_ref[h] for h in range(N_HEADS)], axis=1)
    yb = _dot(ob, wb_ref[...])
    mix = yag_ref[...] + gb_ref[...] * yb
    x1 = x_ref[...] + _dot(mix.astype(BF16), wo_ref[...])
    m = _rms(x1, gm_ref[...]).astype(BF16)
    acc = x1
    for c in range(D_FF // D_MODEL):
        cols = slice(c * D_MODEL, (c + 1) * D_MODEL)
        hc = jnp.maximum(_dot(m, wu_ref[:, cols]), 0.0)
        acc = acc + _dot((hc * hc).astype(BF16), wdn_ref[cols, :])
    o_ref[...] = _rms(acc, gf_ref[...])


def _merge_mlp(ob, yag, pf, x2, wb, wo, gm, wu, wdn, gf, tm):
    T = x2.shape[0]
    return pl.pallas_call(
        _merge_mlp_kernel,
        grid=(T // tm,),
        in_specs=[
            pl.BlockSpec((N_HEADS, tm, HEAD_DIM), lambda i: (0, i, 0)),
            pl.BlockSpec((tm, D_MODEL), lambda i: (i, 0)),
            pl.BlockSpec((tm, D_MODEL), lambda i: (i, 3)),
            pl.BlockSpec((tm, D_MODEL), lambda i: (i, 0)),
            _const_spec((D_MODEL, D_MODEL)),
            _const_spec((D_MODEL, D_MODEL)),
            _const_spec((1, D_MODEL)),
            _const_spec((D_MODEL, D_FF)),
            _const_spec((D_FF, D_MODEL)),
            _const_spec((1, D_MODEL)),
        ],
        out_specs=pl.BlockSpec((tm, D_MODEL), lambda i: (i, 0)),
        out_shape=jax.ShapeDtypeStruct((T, D_MODEL), F32),
        compiler_params=pltpu.CompilerParams(
            dimension_semantics=("arbitrary",), vmem_limit_bytes=VMEM_LIMIT),
        name="merge_mlp",
    )(ob, yag, pf, x2, wb, wo, gm, wu, wdn, gf)


def _block_diag_pairs(wa, wx):
    def pair(w):
        w = w.reshape(LRU_BLOCKS // 2, 2, LRU_BW, LRU_BW)
        z = jnp.zeros_like(w[:, 0])
        top = jnp.concatenate([w[:, 0], z], axis=2)
        bot = jnp.concatenate([z, w[:, 1]], axis=2)
        return jnp.concatenate([top, bot], axis=1)
    return jnp.concatenate([pair(wa), pair(wx)], axis=2).astype(BF16)


def kernel(x, norm_mix_g, w_in, conv_w, conv_b, lru_wa, lru_ba, lru_wx, lru_bx, lru_lambda,
           forget_b, w_branch_a, w_branch_b, w_out, norm_mlp_g, w_up, w_down, norm_final_g):
    B, S, D = x.shape
    assert D == D_MODEL
    T = B * S
    tm = min(512, S)
    tq = min(1024, S)
    tk = tq // 2
    assert S % tm == 0 and S % tq == 0 and tm % LANES == 0

    w_main = w_in.astype(BF16)
    w_vt = w_in[:, W_IN_V_SLAB * D:(W_IN_V_SLAB + 1) * D].T.astype(BF16)
    w_f = jnp.pad(w_in[:, 7 * D:], ((0, 0), (0, LANES - N_HEADS))).astype(BF16)
    wd = _block_diag_pairs(0.5 * lru_wa, 0.5 * lru_wx)
    row = lambda v: v.reshape(1, -1).astype(F32)
    fb = jnp.pad(forget_b, (0, LANES - N_HEADS)).reshape(1, LANES)
    e_rows = jnp.arange(F_PARTS * LANES)
    e_tgt = jnp.where(e_rows % LANES < N_HEADS, (e_rows % LANES) * F_PARTS + e_rows // LANES, -1)
    e = (e_tgt[:, None] == jnp.arange(LANES)[None, :]).astype(BF16)

    x2 = x.reshape(T, D)
    pf, qk, vt, fl, nrm = _in_proj(x2, row(norm_mix_g), w_main, w_vt, w_f, tm)
    yag, kf, ft, fend = _lru(pf, fl, conv_w, row(conv_b), wd, row(0.5 * lru_ba), row(0.5 * lru_bx),
                             row(lru_lambda), fb, w_branch_a.astype(BF16), e, B, S, tm)
    assert tk == tm
    ob = _attn(qk, kf, vt, ft, nrm, fend, B, S, tq, tk, hp=2)
    out = _merge_mlp(ob, yag, pf, x2, w_branch_b.astype(BF16), w_out.astype(BF16),
                     row(norm_mlp_g), w_up.astype(BF16), w_down.astype(BF16),
                     row(norm_final_g), tm)
    return out.reshape(B, S, D)
```

```python
import functools
import math

import jax
import jax.numpy as jnp
from jax import lax
from jax.experimental import pallas as pl
from jax.experimental.pallas import tpu as pltpu

D_MODEL = 1024
N_HEADS = 8
HEAD_DIM = 128
LRU_BLOCKS = 16
LRU_BW = 64
CONV_W = 4
LRU_C = 8.0
D_FF = 4 * D_MODEL
RMS_EPS = 1e-6
LANES = 128
SUBLANES = 8
BF16_SUBLANES = 16
LOG2E = 1.4426950408889634
Q_SCALE = LOG2E / math.sqrt(HEAD_DIM)
VMEM_LIMIT = 56 * 1024 * 1024

BF16 = jnp.bfloat16
F32 = jnp.float32
_NT_DIMS = (((1,), (1,)), ((), ()))
W_IN_PF_SLABS = (0, 1, 5, 6)
W_IN_QK_SLABS = (2, 3)
W_IN_V_SLAB = 4
P_UNDERFLOW_LOG2 = 160.0
NORM_MARGIN = 1.02
F_PARTS = 3


def _dot(a, b):
    return jnp.dot(a, b, preferred_element_type=F32)


def _rms(x, g):
    return x * lax.rsqrt(jnp.mean(x * x, axis=-1, keepdims=True) + RMS_EPS) * g


def _softplus(x):
    return jnp.maximum(x, 0.0) + jnp.log1p(jnp.exp(-jnp.abs(x)))


def _sigmoid(x):
    return 0.5 * jnp.tanh(0.5 * x) + 0.5


def _gelu_tanh(x):
    c = math.sqrt(2.0 / math.pi)
    half_x = 0.5 * x
    return half_x + half_x * jnp.tanh(x * (c + (c * 0.044715) * (x * x)))


def _const_spec(shape):
    nd = len(shape)
    return pl.BlockSpec(shape, lambda *_: (0,) * nd, pipeline_mode=pl.Buffered(1))


def _in_proj_kernel(x_ref, g_ref, w_ref, wvt_ref, pf_ref, qk_ref, vt_ref, flt_ref, nrm_ref):
    u = _rms(x_ref[...], g_ref[...]).astype(BF16)
    post = (None, _gelu_tanh, _sigmoid, _sigmoid)
    for j, src in enumerate(W_IN_PF_SLABS):
        res = _dot(u, w_ref[:, src * D_MODEL:(src + 1) * D_MODEL])
        pf_ref[:, j * D_MODEL:(j + 1) * D_MODEL] = res if post[j] is None else post[j](res)
    for j, src in enumerate(W_IN_QK_SLABS):
        res = _dot(u, w_ref[:, src * D_MODEL:(src + 1) * D_MODEL])
        if j == 0:
            res = res * Q_SCALE
        lane = lax.broadcasted_iota(jnp.int32, (1, LANES), 1)
        nrm = jnp.zeros((1, LANES), F32)
        for h in range(N_HEADS):
            xb = res[:, h * HEAD_DIM:(h + 1) * HEAD_DIM].astype(BF16)
            qk_ref[j, h] = xb
            xf = xb.astype(F32)
            n2 = jnp.max(jnp.sum(xf * xf, axis=1, keepdims=True), axis=0, keepdims=True)
            nrm = jnp.where(lane == h, n2, nrm)
        nrm_ref[j:j + 1, :] = nrm
    vt = lax.dot_general(wvt_ref[...], u, _NT_DIMS, preferred_element_type=F32)
    for h in range(N_HEADS):
        vt_ref[h] = vt[h * HEAD_DIM:(h + 1) * HEAD_DIM, :].astype(BF16)
    flt_ref[...] = vt[D_MODEL:, :]


def _in_proj(x2, g, w_main, w_vt, tm):
    T = x2.shape[0]
    return pl.pallas_call(
        _in_proj_kernel,
        grid=(T // tm,),
        in_specs=[
            pl.BlockSpec((tm, D_MODEL), lambda i: (i, 0)),
            _const_spec((1, D_MODEL)),
            _const_spec(w_main.shape),
            _const_spec((D_MODEL + BF16_SUBLANES, D_MODEL)),
        ],
        out_specs=[
            pl.BlockSpec((tm, 4 * D_MODEL), lambda i: (i, 0)),
            pl.BlockSpec((2, N_HEADS, tm, HEAD_DIM), lambda i: (0, 0, i, 0)),
            pl.BlockSpec((N_HEADS, HEAD_DIM, tm), lambda i: (0, 0, i)),
            pl.BlockSpec((BF16_SUBLANES, tm), lambda i: (0, i)),
            pl.BlockSpec((None, 2, LANES), lambda i: (i, 0, 0)),
        ],
        out_shape=[
            jax.ShapeDtypeStruct((T, 4 * D_MODEL), F32),
            jax.ShapeDtypeStruct((2, N_HEADS, T, HEAD_DIM), BF16),
            jax.ShapeDtypeStruct((N_HEADS, HEAD_DIM, T), BF16),
            jax.ShapeDtypeStruct((BF16_SUBLANES, T), F32),
            jax.ShapeDtypeStruct((T // tm, 2, LANES), F32),
        ],
        compiler_params=pltpu.CompilerParams(
            dimension_semantics=("arbitrary",), vmem_limit_bytes=VMEM_LIMIT),
        name="in_proj",
    )(x2, g, w_main, w_vt)


def _lru_kernel(xl_ref, gl_ref, ga_ref, flt_ref, cw_ref, cb_ref, wd_ref, ba_ref, bx_ref,
                lam_ref, fb_ref, wa_ref, e_ref,
                yag_ref, kf_ref, ft_ref, fend_ref,
                xq_sc, a_sc, b_sc, h_sc, halo_sc, hc_sc, fc_sc, *, tm):
    s = pl.program_id(1)
    seg = tm // SUBLANES
    pitch = seg + SUBLANES
    gps = seg // SUBLANES
    nc = D_MODEL // LANES
    n_halo = CONV_W - 1

    @pl.when(s == 0)
    def _():
        halo_sc[...] = jnp.zeros_like(halo_sc)
        hc_sc[...] = jnp.zeros_like(hc_sc)
        fc_sc[...] = jnp.zeros_like(fc_sc)

    for g in range(tm // SUBLANES):
        u, j0 = g // gps, (g % gps) * SUBLANES
        for c in range(nc):
            xq_sc[c, pl.ds(j0 * SUBLANES + u, SUBLANES, stride=SUBLANES), :] = (
                xl_ref[g * SUBLANES:(g + 1) * SUBLANES, c * LANES:(c + 1) * LANES])

    sub = lax.broadcasted_iota(jnp.int32, (SUBLANES, LANES), 0)
    c_all = -LRU_C * _softplus(-lam_ref[...])
    for c in range(nc):
        cols = slice(c * LANES, (c + 1) * LANES)
        x = xq_sc[c]
        wrap = []
        for q in range(n_halo):
            rows = slice(q * SUBLANES, (q + 1) * SUBLANES)
            rolled = pltpu.roll(x[tm - n_halo * SUBLANES:, :][rows], 1, axis=0)
            wrap.append(jnp.where(sub == 0, halo_sc[c, rows, :], rolled))
            halo_sc[c, rows, :] = rolled
        xa = cb_ref[:, cols] + cw_ref[CONV_W - 1:CONV_W, cols] * x
        for d in range(1, CONV_W):
            xd = jnp.concatenate(wrap[n_halo - d:] + [x[:tm - d * SUBLANES, :]], axis=0)
            xa = xa + cw_ref[CONV_W - 1 - d:CONV_W - d, cols] * xd
        ri = _dot(xa.astype(BF16), wd_ref[c])
        t_r = jnp.tanh(ri[:, :LANES] + ba_ref[:, cols])
        t_i = jnp.tanh(ri[:, LANES:] + bx_ref[:, cols])
        half_c = 0.5 * c_all[:, cols]
        a = jnp.exp(half_c * t_r + half_c)
        half_xa = 0.5 * xa
        one_m_a2 = 1.0 - a * a
        mult = one_m_a2 * lax.rsqrt(jnp.maximum(one_m_a2, 1e-30))
        a_sc[c] = a
        b_sc[c] = mult * (half_xa * t_i + half_xa)

    def scan_step(j, carry):
        r0 = pl.multiple_of(j * SUBLANES, SUBLANES)
        out = []
        for c in range(nc):
            h, p = carry[c]
            a = a_sc[c, pl.ds(r0, SUBLANES), :]
            out.append((a * h + b_sc[c, pl.ds(r0, SUBLANES), :], a * p))
        return tuple(out)

    ends = lax.fori_loop(
        0, seg, scan_step,
        tuple((jnp.zeros((SUBLANES, LANES), F32), jnp.ones((SUBLANES, LANES), F32))
              for _ in range(nc)), unroll=SUBLANES)

    carry_in = []
    for c in range(nc):
        cols = slice(c * LANES, (c + 1) * LANES)
        h_end, p_end = ends[c]
        for d in (1, 2, 4):
            keep = sub >= d
            p_sh = jnp.where(keep, pltpu.roll(p_end, d, axis=0), 1.0)
            h_sh = jnp.where(keep, pltpu.roll(h_end, d, axis=0), 0.0)
            h_end = p_end * h_sh + h_end
            p_end = p_end * p_sh
        blk_in = hc_sc[:, cols]
        seg_out = h_end + p_end * blk_in
        carry_in.append(jnp.where(sub == 0, blk_in, pltpu.roll(seg_out, 1, axis=0)))
        hc_sc[:, cols] = jnp.broadcast_to(seg_out[SUBLANES - 1:SUBLANES, :], (SUBLANES, LANES))

    def final_step(j, carry):
        r0 = pl.multiple_of(j * SUBLANES, SUBLANES)
        out = []
        for c in range(nc):
            h = a_sc[c, pl.ds(r0, SUBLANES), :] * carry[c] + b_sc[c, pl.ds(r0, SUBLANES), :]
            h_sc[c, pl.ds(j, SUBLANES, stride=pitch), :] = h
            out.append(h)
        return tuple(out)

    lax.fori_loop(0, seg, final_step, tuple(carry_in), unroll=SUBLANES)

    h = jnp.concatenate(
        [jnp.concatenate([h_sc[c, u * pitch:u * pitch + seg, :] for u in range(SUBLANES)], axis=0)
         for c in range(nc)], axis=1)
    y = (gl_ref[...] * h).astype(BF16)
    yag_ref[...] = ga_ref[...] * _dot(y, wa_ref[...])

    fl = jnp.concatenate([flt_ref[...], jnp.zeros((LANES - BF16_SUBLANES, tm), F32)], axis=0).T
    lf = -_softplus(-(fl + fb_ref[...])) * LOG2E
    rowf = lax.broadcasted_iota(jnp.int32, (tm, LANES), 0)
    d = 1
    while d < tm:
        lf = lf + jnp.where(rowf >= d, pltpu.roll(lf, d, axis=0), 0.0)
        d *= 2
    fblk = lf + fc_sc[...]
    fc_sc[...] = fblk[tm - 1:tm, :]
    ft_ref[...] = fblk.T[0:N_HEADS, :]
    fend_ref[...] = fblk[tm - 1:tm, :]
    hi = (-fblk).astype(BF16)
    r1 = -fblk - hi.astype(F32)
    mid = r1.astype(BF16)
    lo = (r1 - mid.astype(F32)).astype(BF16)
    kf_ref[...] = _dot(jnp.concatenate([hi, mid, lo], axis=1), e_ref[...]).astype(BF16)


def _lru(pf, fl, cw, cb, wd, ba, bx, lam, fb, wa, e, B, S, tm):
    T = B * S
    ns = S // tm
    row_blk = lambda c: pl.BlockSpec((tm, D_MODEL), lambda b, s, c=c: (b * ns + s, c))
    return pl.pallas_call(
        functools.partial(_lru_kernel, tm=tm),
        grid=(B, ns),
        in_specs=[
            row_blk(0), row_blk(1), row_blk(2),
            pl.BlockSpec((BF16_SUBLANES, tm), lambda b, s: (0, b * ns + s)),
            _const_spec((CONV_W, D_MODEL)),
            _const_spec((1, D_MODEL)),
            _const_spec((D_MODEL // LANES, LANES, 2 * LANES)),
            _const_spec((1, D_MODEL)),
            _const_spec((1, D_MODEL)),
            _const_spec((1, D_MODEL)),
            _const_spec((1, LANES)),
            _const_spec((D_MODEL, D_MODEL)),
            _const_spec((F_PARTS * LANES, LANES)),
        ],
        out_specs=[
            pl.BlockSpec((tm, D_MODEL), lambda b, s: (b * ns + s, 0)),
            pl.BlockSpec((tm, LANES), lambda b, s: (b * ns + s, 0)),
            pl.BlockSpec((None, N_HEADS, tm), lambda b, s: (b, 0, s)),
            pl.BlockSpec((None, 1, LANES), lambda b, s: (b * ns + s, 0, 0)),
        ],
        out_shape=[
            jax.ShapeDtypeStruct((T, D_MODEL), F32),
            jax.ShapeDtypeStruct((T, LANES), BF16),
            jax.ShapeDtypeStruct((B, N_HEADS, S), F32),
            jax.ShapeDtypeStruct((T // tm, 1, LANES), F32),
        ],
        scratch_shapes=[
            pltpu.VMEM((D_MODEL // LANES, tm, LANES), F32),
            pltpu.VMEM((D_MODEL // LANES, tm, LANES), F32),
            pltpu.VMEM((D_MODEL // LANES, tm, LANES), F32),
            pltpu.VMEM((D_MODEL // LANES, tm + SUBLANES * SUBLANES, LANES), F32),
            pltpu.VMEM((D_MODEL // LANES, (CONV_W - 1) * SUBLANES, LANES), F32),
            pltpu.VMEM((SUBLANES, D_MODEL), F32),
            pltpu.VMEM((1, LANES), F32),
        ],
        compiler_params=pltpu.CompilerParams(
            dimension_semantics=("arbitrary", "arbitrary"), vmem_limit_bytes=VMEM_LIMIT),
        name="lru",
    )(pf, pf, pf, fl, cw, cb, wd, ba, bx, lam, fb, wa, e)


def _col_reduce(op, z):
    rows, cols = z.shape
    return op(op(z.reshape(SUBLANES, rows // SUBLANES, cols), axis=0), axis=0, keepdims=True)


def _attn_kernel(q_ref, qn_ref, k_ref, kf_ref, k0n_ref, kf0n_ref, vt_ref, ft_ref, nrm_ref, fend_ref,
                 o_ref, z_sc, p_sc, acc_sc, m_sc, zmax_sc, skip_sm, *, tq, tk, hp, nq):
    hg = pl.program_id(1)
    qi = pl.program_id(2)
    n_blk = nrm_ref.shape[0]
    head_lane = lax.broadcasted_iota(jnp.int32, (1, LANES), 1)
    blk = lax.broadcasted_iota(jnp.int32, (n_blk, LANES), 0)

    def first_live_block(tile):
        q2 = jnp.maximum(nrm_ref[2 * tile, 0:1, :], nrm_ref[2 * tile + 1, 0:1, :])
        k2 = jnp.max(nrm_ref[:, 1, :], axis=0, keepdims=True)
        bound = (2.0 * NORM_MARGIN) * jnp.sqrt(q2 * k2) + fend_ref[jnp.maximum(2 * tile - 1, 0)]
        dead = (bound - fend_ref[:, 0, :] <= -P_UNDERFLOW_LOG2) & (blk < 2 * tile)
        first_live = jnp.min(jnp.where(dead, n_blk, blk), axis=0, keepdims=True)
        mine = (head_lane >= hg * hp) & (head_lane < (hg + 1) * hp)
        return jnp.min(jnp.where(mine, first_live, n_blk))

    lane = lax.broadcasted_iota(jnp.int32, (tq, LANES), 1)
    upper = slice(tk, tq)
    n_groups = pl.num_programs(0) * pl.num_programs(1)
    group_next = jnp.minimum(pl.program_id(0) * pl.num_programs(1) + hg + 1, n_groups - 1)
    last_tile = qi == nq - 1
    hg_next = jnp.where(last_tile, group_next % pl.num_programs(1), hg)

    def piece_ones(h):
        return jnp.where((lane >= F_PARTS * h) & (lane < F_PARTS * (h + 1)), 1.0, 0.0).astype(BF16)

    qa, qa_next, f_t = [], [], []
    for a in range(hp):
        h = hg * hp + a
        qa.append(jnp.concatenate([q_ref[a], piece_ones(h)], axis=1))
        qa_next.append(jnp.concatenate([qn_ref[a], piece_ones(hg_next * hp + a)], axis=1))
        f_t.append(ft_ref[pl.ds(h, 1), :])

    def qk(a, j, queries):
        c0 = pl.multiple_of(j * tk, tk)
        ka = jnp.concatenate([k_ref[a, pl.ds(c0, tk), :], kf_ref[pl.ds(c0, tk), :]], axis=1)
        return lax.dot_general(ka, queries, _NT_DIMS, preferred_element_type=F32)

    ones_rows = jnp.ones((BF16_SUBLANES, tk), BF16)

    def pv(a, j, p):
        c0 = pl.multiple_of(j * tk, tk)
        vt_aug = jnp.concatenate([vt_ref[a, :, pl.ds(c0, tk)], ones_rows], axis=0)
        return _dot(vt_aug, p)

    def softmax(z, m_row, zmax, f_row, masked):
        if masked:
            kr = lax.broadcasted_iota(jnp.int32, z.shape, 0)
            qc = lax.broadcasted_iota(jnp.int32, z.shape, 1)
            z = jnp.where(kr <= qc, z, -jnp.inf)
            zmax = _col_reduce(jnp.max, z)
        m_new = jnp.maximum(m_row, zmax + f_row)
        return m_new, jnp.exp2(m_row - m_new), jnp.exp2(z - (m_new - f_row)).astype(BF16)

    def trip(j, slot, masked=False):
        zmax_next = []
        for a in range(hp):
            if masked:
                z_sc[1 - slot, a, :, upper] = qk(a, j + 1, qa[a][upper])
            else:
                zn = qk(a, j + 1, qa[a])
                z_sc[1 - slot, a] = zn
                zmax_next.append(_col_reduce(jnp.max, zn))
        pvs = [pv(a, jnp.maximum(j - 1, 0), p_sc[a]) for a in range(hp)]
        for a in range(hp):
            if masked:
                z_lo = z_sc[slot, a, :, :tk]
                kr = lax.broadcasted_iota(jnp.int32, z_lo.shape, 0)
                qc = lax.broadcasted_iota(jnp.int32, z_lo.shape, 1)
                z_lo = jnp.where(kr <= qc, z_lo, -jnp.inf)
                zmax = jnp.concatenate([_col_reduce(jnp.max, z_lo), zmax_sc[a][:, upper]], axis=1)
                m_new = jnp.maximum(m_sc[a], zmax + f_t[a])
                shift = m_new - f_t[a]
                alpha = jnp.exp2(m_sc[a] - m_new)
                p_sc[a, :, :tk] = jnp.exp2(z_lo - shift[:, :tk]).astype(BF16)
                p_sc[a, :, upper] = jnp.exp2(z_sc[slot, a, :, upper] - shift[:, upper]).astype(BF16)
            else:
                m_new, alpha, p = softmax(z_sc[slot, a], m_sc[a], zmax_sc[a], f_t[a], False)
                p_sc[a] = p
                zmax_sc[a] = zmax_next[a]
            m_sc[a] = m_new
            acc_sc[a] = alpha * (acc_sc[a] + pvs[a])

    @pl.when((pl.program_id(0) == 0) & (hg == 0) & (qi == 0))
    def _():
        p_sc[...] = jnp.zeros_like(p_sc)
        skip_sm[0] = 0
        for a in range(hp):
            z0 = qk(a, 0, qa[a])
            z_sc[0, a] = z0
            zmax_sc[a] = _col_reduce(jnp.max, z0)

    for a in range(hp):
        m_sc[a] = jnp.full((1, tq), -jnp.inf, F32)
        acc_sc[a] = jnp.zeros((HEAD_DIM + BF16_SUBLANES, tq), F32)

    start = skip_sm[0]

    @pl.when((start & 1) == 1)
    def _():
        trip(start, 1)

    @pl.loop((start + 1) >> 1, qi)
    def _(i):
        trip(2 * i, 0)
        trip(2 * i + 1, 1)

    first = 2 * qi
    trip(first, 0, masked=True)
    for a in range(hp):
        acc = acc_sc[a] + pv(a, first, p_sc[a])
        _, alpha, p = softmax(z_sc[1, a, :, upper], m_sc[a][:, upper], None, f_t[a][:, upper], True)
        acc_up = alpha * acc[:, upper] + pv(a, first + 1, p)
        out = jnp.concatenate(
            [acc[:HEAD_DIM, :tk] / acc[HEAD_DIM:HEAD_DIM + 1, :tk],
             acc_up[:HEAD_DIM] / acc_up[HEAD_DIM:HEAD_DIM + 1]], axis=1)
        o_ref[a] = out.T.astype(BF16)

    start_next = jnp.where(last_tile, 0, first_live_block(jnp.minimum(qi + 1, nq - 1)))
    skip_sm[0] = start_next
    c0 = pl.multiple_of(start_next * tk, tk)
    for a in range(hp):
        ka_here = jnp.concatenate([k_ref[a, pl.ds(c0, tk), :], kf_ref[pl.ds(c0, tk), :]], axis=1)
        ka_group = jnp.concatenate([k0n_ref[a], kf0n_ref[...]], axis=1)
        zn = lax.dot_general(jnp.where(last_tile, ka_group, ka_here), qa_next[a], _NT_DIMS,
                             preferred_element_type=F32)
        z_sc[start_next & 1, a] = zn
        zmax_sc[a] = _col_reduce(jnp.max, zn)


def _attn(qk, kf, vt, ft, nrm, fend, B, S, tq, tk, hp):
    T = B * S
    nq = S // tq
    n_blk = S // tk
    n_grp = N_HEADS // hp
    assert tq == 2 * tk and nrm.shape[0] == B * n_blk and fend.shape[0] == B * n_blk

    def next_group(b, g):
        gn = jnp.minimum(b * n_grp + g + 1, B * n_grp - 1)
        return gn // n_grp, gn % n_grp

    def q_next(b, g, i):
        bn, gn = next_group(b, g)
        last = i == nq - 1
        return (0, jnp.where(last, gn, g), jnp.where(last, bn * nq, b * nq + i + 1), 0)
    return pl.pallas_call(
        functools.partial(_attn_kernel, tq=tq, tk=tk, hp=hp, nq=nq),
        grid=(B, N_HEADS // hp, nq),
        in_specs=[
            pl.BlockSpec((None, hp, tq, HEAD_DIM), lambda b, g, i: (0, g, b * nq + i, 0)),
            pl.BlockSpec((None, hp, tq, HEAD_DIM), q_next),
            pl.BlockSpec((None, hp, S, HEAD_DIM), lambda b, g, i: (1, g, b, 0)),
            pl.BlockSpec((S, LANES), lambda b, g, i: (b, 0)),
            pl.BlockSpec((None, hp, tk, HEAD_DIM),
                         lambda b, g, i: (1, next_group(b, g)[1], next_group(b, g)[0] * n_blk, 0)),
            pl.BlockSpec((tk, LANES), lambda b, g, i: (next_group(b, g)[0] * n_blk, 0)),
            pl.BlockSpec((hp, HEAD_DIM, S), lambda b, g, i: (g, 0, b)),
            pl.BlockSpec((None, N_HEADS, tq), lambda b, g, i: (b, 0, i)),
            pl.BlockSpec((n_blk, 2, LANES), lambda b, g, i: (b, 0, 0)),
            pl.BlockSpec((n_blk, 1, LANES), lambda b, g, i: (b, 0, 0)),
        ],
        out_specs=pl.BlockSpec((hp, tq, HEAD_DIM), lambda b, g, i: (g, b * nq + i, 0)),
        out_shape=jax.ShapeDtypeStruct((N_HEADS, T, HEAD_DIM), BF16),
        scratch_shapes=[
            pltpu.VMEM((2, hp, tk, tq), F32),
            pltpu.VMEM((hp, tk, tq), BF16),
            pltpu.VMEM((hp, HEAD_DIM + BF16_SUBLANES, tq), F32),
            pltpu.VMEM((hp, 1, tq), F32),
            pltpu.VMEM((hp, 1, tq), F32),
            pltpu.SMEM((1,), jnp.int32),
        ],
        compiler_params=pltpu.CompilerParams(
            dimension_semantics=("arbitrary", "arbitrary", "arbitrary"),
            vmem_limit_bytes=VMEM_LIMIT),
        name="fox_attn",
    )(qk, qk, qk, kf, qk, kf, vt, ft, nrm, fend)


def _merge_mlp_kernel(ob_ref, yag_ref, gb_ref, x_ref, wb_ref, wo_ref, gm_ref, wu_ref,
                      wdn_ref, gf_ref, o_ref):
    ob = jnp.concatenate([ob_ref[h] for h in range(N_HEADS)], axis=1)
    yb = _dot(ob, wb_ref[...])
    mix = yag_ref[...] + gb_ref[...] * yb
    x1 = x_ref[...] + _dot(mix.astype(BF16), wo_ref[...])
    m = _rms(x1, gm_ref[...]).astype(BF16)
    acc = x1
    for c in range(D_FF // D_MODEL):
        cols = slice(c * D_MODEL, (c + 1) * D_MODEL)
        hc = jnp.maximum(_dot(m, wu_ref[:, cols]), 0.0)
        acc = acc + _dot((hc * hc).astype(BF16), wdn_ref[cols, :])
    o_ref[...] = _rms(acc, gf_ref[...])


def _merge_mlp(ob, yag, pf, x2, wb, wo, gm, wu, wdn, gf, tm):
    T = x2.shape[0]
    return pl.pallas_call(
        _merge_mlp_kernel,
        grid=(T // tm,),
        in_specs=[
            pl.BlockSpec((N_HEADS, tm, HEAD_DIM), lambda i: (0, i, 0)),
            pl.BlockSpec((tm, D_MODEL), lambda i: (i, 0)),
            pl.BlockSpec((tm, D_MODEL), lambda i: (i, 3)),
            pl.BlockSpec((tm, D_MODEL), lambda i: (i, 0)),
            _const_spec((D_MODEL, D_MODEL)),
            _const_spec((D_MODEL, D_MODEL)),
            _const_spec((1, D_MODEL)),
            _const_spec((D_MODEL, D_FF)),
            _const_spec((D_FF, D_MODEL)),
            _const_spec((1, D_MODEL)),
        ],
        out_specs=pl.BlockSpec((tm, D_MODEL), lambda i: (i, 0)),
        out_shape=jax.ShapeDtypeStruct((T, D_MODEL), F32),
        compiler_params=pltpu.CompilerParams(
            dimension_semantics=("arbitrary",), vmem_limit_bytes=VMEM_LIMIT),
        name="merge_mlp",
    )(ob, yag, pf, x2, wb, wo, gm, wu, wdn, gf)


def _block_diag_pairs(wa, wx):
    def pair(w):
        w = w.reshape(LRU_BLOCKS // 2, 2, LRU_BW, LRU_BW)
        z = jnp.zeros_like(w[:, 0])
        top = jnp.concatenate([w[:, 0], z], axis=2)
        bot = jnp.concatenate([z, w[:, 1]], axis=2)
        return jnp.concatenate([top, bot], axis=1)
    return jnp.concatenate([pair(wa), pair(wx)], axis=2).astype(BF16)


def kernel(x, norm_mix_g, w_in, conv_w, conv_b, lru_wa, lru_ba, lru_wx, lru_bx, lru_lambda,
           forget_b, w_branch_a, w_branch_b, w_out, norm_mlp_g, w_up, w_down, norm_final_g):
    B, S, D = x.shape
    assert D == D_MODEL
    T = B * S
    tm = min(512, S)
    tq = min(1024, S)
    tk = tq // 2
    assert S % tm == 0 and S % tq == 0 and tm % LANES == 0

    w_main = w_in.astype(BF16)
    w_vt = jnp.concatenate([w_in[:, W_IN_V_SLAB * D:(W_IN_V_SLAB + 1) * D], w_in[:, 7 * D:],
                            jnp.zeros((D, BF16_SUBLANES - N_HEADS), F32)], axis=1).T.astype(BF16)
    wd = _block_diag_pairs(0.5 * lru_wa, 0.5 * lru_wx)
    row = lambda v: v.reshape(1, -1).astype(F32)
    fb = jnp.pad(forget_b, (0, LANES - N_HEADS)).reshape(1, LANES)
    e_rows = jnp.arange(F_PARTS * LANES)
    e_tgt = jnp.where(e_rows % LANES < N_HEADS, (e_rows % LANES) * F_PARTS + e_rows // LANES, -1)
    e = (e_tgt[:, None] == jnp.arange(LANES)[None, :]).astype(BF16)

    x2 = x.reshape(T, D)
    pf, qk, vt, fl, nrm = _in_proj(x2, row(norm_mix_g), w_main, w_vt, tm)
    yag, kf, ft, fend = _lru(pf, fl, conv_w, row(conv_b), wd, row(0.5 * lru_ba), row(0.5 * lru_bx),
                             row(lru_lambda), fb, w_branch_a.astype(BF16), e, B, S, tm)
    assert tk == tm
    ob = _attn(qk, kf, vt, ft, nrm, fend, B, S, tq, tk, hp=2)
    out = _merge_mlp(ob, yag, pf, x2, w_branch_b.astype(BF16), w_out.astype(BF16),
                     row(norm_mlp_g), w_up.astype(BF16), w_down.astype(BF16),
                     row(norm_final_g), tm)
    return out.reshape(B, S, D)
```

```python
import functools
import math

import jax
import jax.numpy as jnp
from jax import lax
from jax.experimental import pallas as pl
from jax.experimental.pallas import tpu as pltpu

D_MODEL = 1024
N_HEADS = 8
HEAD_DIM = 128
LRU_BLOCKS = 16
LRU_BW = 64
CONV_W = 4
LRU_C = 8.0
D_FF = 4 * D_MODEL
RMS_EPS = 1e-6
LANES = 128
SUBLANES = 8
BF16_SUBLANES = 16
LOG2E = 1.4426950408889634
Q_SCALE = LOG2E / math.sqrt(HEAD_DIM)
VMEM_LIMIT = 56 * 1024 * 1024

BF16 = jnp.bfloat16
F32 = jnp.float32
_NT_DIMS = (((1,), (1,)), ((), ()))
W_IN_PF_SLABS = (0, 1, 5, 6)
W_IN_QK_SLABS = (2, 3)
W_IN_V_SLAB = 4
P_UNDERFLOW_LOG2 = 160.0
NORM_MARGIN = 1.02
F_PARTS = 3


def _dot(a, b):
    return jnp.dot(a, b, preferred_element_type=F32)


def _rms(x, g):
    return x * lax.rsqrt(jnp.mean(x * x, axis=-1, keepdims=True) + RMS_EPS) * g


def _softplus(x):
    return jnp.maximum(x, 0.0) + jnp.log1p(jnp.exp(-jnp.abs(x)))


def _sigmoid(x):
    return 0.5 * jnp.tanh(0.5 * x) + 0.5


def _gelu_tanh(x):
    c = math.sqrt(2.0 / math.pi)
    half_x = 0.5 * x
    return half_x + half_x * jnp.tanh(x * (c + (c * 0.044715) * (x * x)))


def _const_spec(shape):
    nd = len(shape)
    return pl.BlockSpec(shape, lambda *_: (0,) * nd, pipeline_mode=pl.Buffered(1))


def _in_proj_kernel(x_ref, g_ref, w_ref, wvt_ref, wf_ref, pf_ref, qk_ref, vt_ref, fl_ref,
                    nrm_ref):
    u = _rms(x_ref[...], g_ref[...]).astype(BF16)
    fl_ref[...] = _dot(u, wf_ref[...])
    post = (None, _gelu_tanh, _sigmoid, _sigmoid)
    for j, src in enumerate(W_IN_PF_SLABS):
        res = _dot(u, w_ref[:, src * D_MODEL:(src + 1) * D_MODEL])
        pf_ref[:, j * D_MODEL:(j + 1) * D_MODEL] = res if post[j] is None else post[j](res)
    for j, src in enumerate(W_IN_QK_SLABS):
        res = _dot(u, w_ref[:, src * D_MODEL:(src + 1) * D_MODEL])
        if j == 0:
            res = res * Q_SCALE
        lane = lax.broadcasted_iota(jnp.int32, (1, LANES), 1)
        nrm = jnp.zeros((1, LANES), F32)
        for h in range(N_HEADS):
            xb = res[:, h * HEAD_DIM:(h + 1) * HEAD_DIM].astype(BF16)
            qk_ref[j, h] = xb
            xf = xb.astype(F32)
            n2 = jnp.max(jnp.sum(xf * xf, axis=1, keepdims=True), axis=0, keepdims=True)
            nrm = jnp.where(lane == h, n2, nrm)
        nrm_ref[j:j + 1, :] = nrm
    vt = lax.dot_general(wvt_ref[...], u, _NT_DIMS, preferred_element_type=F32)
    for h in range(N_HEADS):
        vt_ref[h] = vt[h * HEAD_DIM:(h + 1) * HEAD_DIM, :].astype(BF16)


def _in_proj(x2, g, w_main, w_vt, w_f, tm):
    T = x2.shape[0]
    return pl.pallas_call(
        _in_proj_kernel,
        grid=(T // tm,),
        in_specs=[
            pl.BlockSpec((tm, D_MODEL), lambda i: (i, 0)),
            _const_spec((1, D_MODEL)),
            _const_spec(w_main.shape),
            _const_spec((D_MODEL, D_MODEL)),
            _const_spec((D_MODEL, LANES)),
        ],
        out_specs=[
            pl.BlockSpec((tm, 4 * D_MODEL), lambda i: (i, 0)),
            pl.BlockSpec((2, N_HEADS, tm, HEAD_DIM), lambda i: (0, 0, i, 0)),
            pl.BlockSpec((N_HEADS, HEAD_DIM, tm), lambda i: (0, 0, i)),
            pl.BlockSpec((tm, LANES), lambda i: (i, 0)),
            pl.BlockSpec((None, 2, LANES), lambda i: (i, 0, 0)),
        ],
        out_shape=[
            jax.ShapeDtypeStruct((T, 4 * D_MODEL), F32),
            jax.ShapeDtypeStruct((2, N_HEADS, T, HEAD_DIM), BF16),
            jax.ShapeDtypeStruct((N_HEADS, HEAD_DIM, T), BF16),
            jax.ShapeDtypeStruct((T, LANES), F32),
            jax.ShapeDtypeStruct((T // tm, 2, LANES), F32),
        ],
        compiler_params=pltpu.CompilerParams(
            dimension_semantics=("arbitrary",), vmem_limit_bytes=VMEM_LIMIT),
        name="in_proj",
    )(x2, g, w_main, w_vt, w_f)


def _lru_kernel(pf_hbm, gl_ref, ga_ref, fl_ref, cw_ref, cb_ref, wd_ref, ba_ref, bx_ref,
                lam_ref, fb_ref, wa_ref, e_ref,
                yag_ref, kf_ref, ft_ref, fend_ref,
                xq_sc, a_sc, b_sc, h_sc, halo_sc, hc_sc, fc_sc, x_sem, *, tm):
    s = pl.program_id(1)
    seg = tm // SUBLANES
    pitch = seg + SUBLANES
    nc = D_MODEL // LANES
    n_halo = CONV_W - 1
    tile = pl.program_id(0) * pl.num_programs(1) + s
    n_tiles = pl.num_programs(0) * pl.num_programs(1)
    slot = tile % 2

    def x_copies(t, buf):
        return [pltpu.make_async_copy(
            pf_hbm.at[pl.ds(t * tm + u * seg, seg), pl.ds(0, D_MODEL)],
            xq_sc.at[buf, :, u, :],
            x_sem.at[buf]) for u in range(SUBLANES)]

    @pl.when(tile == 0)
    def _():
        for cp in x_copies(tile, slot):
            cp.start()

    @pl.when(tile + 1 < n_tiles)
    def _():
        for cp in x_copies(tile + 1, 1 - slot):
            cp.start()

    for cp in x_copies(tile, slot):
        cp.wait()

    @pl.when(s == 0)
    def _():
        halo_sc[...] = jnp.zeros_like(halo_sc)
        hc_sc[...] = jnp.zeros_like(hc_sc)
        fc_sc[...] = jnp.zeros_like(fc_sc)

    sub = lax.broadcasted_iota(jnp.int32, (SUBLANES, LANES), 0)
    c_all = -LRU_C * _softplus(-lam_ref[...])
    for c in range(nc):
        cols = slice(c * LANES, (c + 1) * LANES)
        x = xq_sc[slot, :, :, cols].reshape(tm, LANES)
        wrap = []
        for q in range(n_halo):
            rows = slice(q * SUBLANES, (q + 1) * SUBLANES)
            rolled = pltpu.roll(x[tm - n_halo * SUBLANES:, :][rows], 1, axis=0)
            wrap.append(jnp.where(sub == 0, halo_sc[c, rows, :], rolled))
            halo_sc[c, rows, :] = rolled
        xa = cb_ref[:, cols] + cw_ref[CONV_W - 1:CONV_W, cols] * x
        for d in range(1, CONV_W):
            xd = jnp.concatenate(wrap[n_halo - d:] + [x[:tm - d * SUBLANES, :]], axis=0)
            xa = xa + cw_ref[CONV_W - 1 - d:CONV_W - d, cols] * xd
        ri = _dot(xa.astype(BF16), wd_ref[c])
        t_r = jnp.tanh(ri[:, :LANES] + ba_ref[:, cols])
        t_i = jnp.tanh(ri[:, LANES:] + bx_ref[:, cols])
        half_c = 0.5 * c_all[:, cols]
        a = jnp.exp(half_c * t_r + half_c)
        half_xa = 0.5 * xa
        one_m_a2 = 1.0 - a * a
        mult = one_m_a2 * lax.rsqrt(jnp.maximum(one_m_a2, 1e-30))
        a_sc[c] = a
        b_sc[c] = mult * (half_xa * t_i + half_xa)

    def scan_step(j, carry):
        r0 = pl.multiple_of(j * SUBLANES, SUBLANES)
        out = []
        for c in range(nc):
            h, p = carry[c]
            a = a_sc[c, pl.ds(r0, SUBLANES), :]
            out.append((a * h + b_sc[c, pl.ds(r0, SUBLANES), :], a * p))
        return tuple(out)

    ends = lax.fori_loop(
        0, seg, scan_step,
        tuple((jnp.zeros((SUBLANES, LANES), F32), jnp.ones((SUBLANES, LANES), F32))
              for _ in range(nc)), unroll=SUBLANES)

    carry_in = []
    for c in range(nc):
        cols = slice(c * LANES, (c + 1) * LANES)
        h_end, p_end = ends[c]
        for d in (1, 2, 4):
            keep = sub >= d
            p_sh = jnp.where(keep, pltpu.roll(p_end, d, axis=0), 1.0)
            h_sh = jnp.where(keep, pltpu.roll(h_end, d, axis=0), 0.0)
            h_end = p_end * h_sh + h_end
            p_end = p_end * p_sh
        blk_in = hc_sc[:, cols]
        seg_out = h_end + p_end * blk_in
        carry_in.append(jnp.where(sub == 0, blk_in, pltpu.roll(seg_out, 1, axis=0)))
        hc_sc[:, cols] = jnp.broadcast_to(seg_out[SUBLANES - 1:SUBLANES, :], (SUBLANES, LANES))

    def final_step(j, carry):
        r0 = pl.multiple_of(j * SUBLANES, SUBLANES)
        out = []
        for c in range(nc):
            h = a_sc[c, pl.ds(r0, SUBLANES), :] * carry[c] + b_sc[c, pl.ds(r0, SUBLANES), :]
            h_sc[c, pl.ds(j, SUBLANES, stride=pitch), :] = h
            out.append(h)
        return tuple(out)

    lax.fori_loop(0, seg, final_step, tuple(carry_in), unroll=SUBLANES)

    h = jnp.concatenate(
        [jnp.concatenate([h_sc[c, u * pitch:u * pitch + seg, :] for u in range(SUBLANES)], axis=0)
         for c in range(nc)], axis=1)
    y = (gl_ref[...] * h).astype(BF16)
    yag_ref[...] = ga_ref[...] * _dot(y, wa_ref[...])

    lf = -_softplus(-(fl_ref[...] + fb_ref[...])) * LOG2E
    rowf = lax.broadcasted_iota(jnp.int32, (tm, LANES), 0)
    d = 1
    while d < tm:
        lf = lf + jnp.where(rowf >= d, pltpu.roll(lf, d, axis=0), 0.0)
        d *= 2
    fblk = lf + fc_sc[...]
    fc_sc[...] = fblk[tm - 1:tm, :]
    ft_ref[...] = fblk.T[0:N_HEADS, :]
    fend_ref[...] = fblk[tm - 1:tm, :]
    hi = (-fblk).astype(BF16)
    r1 = -fblk - hi.astype(F32)
    mid = r1.astype(BF16)
    lo = (r1 - mid.astype(F32)).astype(BF16)
    kf_ref[...] = _dot(jnp.concatenate([hi, mid, lo], axis=1), e_ref[...]).astype(BF16)


def _lru(pf, fl, cw, cb, wd, ba, bx, lam, fb, wa, e, B, S, tm):
    T = B * S
    ns = S // tm
    row_blk = lambda c: pl.BlockSpec((tm, D_MODEL), lambda b, s, c=c: (b * ns + s, c))
    return pl.pallas_call(
        functools.partial(_lru_kernel, tm=tm),
        grid=(B, ns),
        in_specs=[
            pl.BlockSpec(memory_space=pl.ANY),
            row_blk(1), row_blk(2),
            pl.BlockSpec((tm, LANES), lambda b, s: (b * ns + s, 0)),
            _const_spec((CONV_W, D_MODEL)),
            _const_spec((1, D_MODEL)),
            _const_spec((D_MODEL // LANES, LANES, 2 * LANES)),
            _const_spec((1, D_MODEL)),
            _const_spec((1, D_MODEL)),
            _const_spec((1, D_MODEL)),
            _const_spec((1, LANES)),
            _const_spec((D_MODEL, D_MODEL)),
            _const_spec((F_PARTS * LANES, LANES)),
        ],
        out_specs=[
            pl.BlockSpec((tm, D_MODEL), lambda b, s: (b * ns + s, 0)),
            pl.BlockSpec((tm, LANES), lambda b, s: (b * ns + s, 0)),
            pl.BlockSpec((None, N_HEADS, tm), lambda b, s: (b, 0, s)),
            pl.BlockSpec((None, 1, LANES), lambda b, s: (b * ns + s, 0, 0)),
        ],
        out_shape=[
            jax.ShapeDtypeStruct((T, D_MODEL), F32),
            jax.ShapeDtypeStruct((T, LANES), BF16),
            jax.ShapeDtypeStruct((B, N_HEADS, S), F32),
            jax.ShapeDtypeStruct((T // tm, 1, LANES), F32),
        ],
        scratch_shapes=[
            pltpu.VMEM((2, tm // SUBLANES, SUBLANES, D_MODEL), F32),
            pltpu.VMEM((D_MODEL // LANES, tm, LANES), F32),
            pltpu.VMEM((D_MODEL // LANES, tm, LANES), F32),
            pltpu.VMEM((D_MODEL // LANES, tm + SUBLANES * SUBLANES, LANES), F32),
            pltpu.VMEM((D_MODEL // LANES, (CONV_W - 1) * SUBLANES, LANES), F32),
            pltpu.VMEM((SUBLANES, D_MODEL), F32),
            pltpu.VMEM((1, LANES), F32),
            pltpu.SemaphoreType.DMA((2,)),
        ],
        compiler_params=pltpu.CompilerParams(
            dimension_semantics=("arbitrary", "arbitrary"), vmem_limit_bytes=VMEM_LIMIT),
        name="lru",
    )(pf, pf, pf, fl, cw, cb, wd, ba, bx, lam, fb, wa, e)


def _col_reduce(op, z):
    rows, cols = z.shape
    return op(op(z.reshape(SUBLANES, rows // SUBLANES, cols), axis=0), axis=0, keepdims=True)


def _attn_kernel(q_ref, qn_ref, k_ref, kf_ref, k0n_ref, kf0n_ref, vt_ref, ft_ref, nrm_ref, fend_ref,
                 o_ref, z_sc, p_sc, acc_sc, m_sc, zmax_sc, skip_sm, *, tq, tk, hp, nq):
    hg = pl.program_id(1)
    qi = pl.program_id(2)
    n_blk = nrm_ref.shape[0]
    head_lane = lax.broadcasted_iota(jnp.int32, (1, LANES), 1)
    blk = lax.broadcasted_iota(jnp.int32, (n_blk, LANES), 0)

    def first_live_block(tile):
        q2 = jnp.maximum(nrm_ref[2 * tile, 0:1, :], nrm_ref[2 * tile + 1, 0:1, :])
        k2 = jnp.max(nrm_ref[:, 1, :], axis=0, keepdims=True)
        bound = (2.0 * NORM_MARGIN) * jnp.sqrt(q2 * k2) + fend_ref[jnp.maximum(2 * tile - 1, 0)]
        dead = (bound - fend_ref[:, 0, :] <= -P_UNDERFLOW_LOG2) & (blk < 2 * tile)
        first_live = jnp.min(jnp.where(dead, n_blk, blk), axis=0, keepdims=True)
        mine = (head_lane >= hg * hp) & (head_lane < (hg + 1) * hp)
        return jnp.min(jnp.where(mine, first_live, n_blk))

    lane = lax.broadcasted_iota(jnp.int32, (tq, LANES), 1)
    upper = slice(tk, tq)
    n_groups = pl.num_programs(0) * pl.num_programs(1)
    group_next = jnp.minimum(pl.program_id(0) * pl.num_programs(1) + hg + 1, n_groups - 1)
    last_tile = qi == nq - 1
    hg_next = jnp.where(last_tile, group_next % pl.num_programs(1), hg)

    def piece_ones(h):
        return jnp.where((lane >= F_PARTS * h) & (lane < F_PARTS * (h + 1)), 1.0, 0.0).astype(BF16)

    qa, qa_next, f_t = [], [], []
    for a in range(hp):
        h = hg * hp + a
        qa.append(jnp.concatenate([q_ref[a], piece_ones(h)], axis=1))
        qa_next.append(jnp.concatenate([qn_ref[a], piece_ones(hg_next * hp + a)], axis=1))
        f_t.append(ft_ref[pl.ds(h, 1), :])

    def qk(a, j, queries):
        c0 = pl.multiple_of(j * tk, tk)
        ka = jnp.concatenate([k_ref[a, pl.ds(c0, tk), :], kf_ref[pl.ds(c0, tk), :]], axis=1)
        return lax.dot_general(ka, queries, _NT_DIMS, preferred_element_type=F32)

    ones_rows = jnp.ones((BF16_SUBLANES, tk), BF16)

    def pv(a, j, p):
        c0 = pl.multiple_of(j * tk, tk)
        vt_aug = jnp.concatenate([vt_ref[a, :, pl.ds(c0, tk)], ones_rows], axis=0)
        return _dot(vt_aug, p)

    def softmax(z, m_row, zmax, f_row, masked):
        if masked:
            kr = lax.broadcasted_iota(jnp.int32, z.shape, 0)
            qc = lax.broadcasted_iota(jnp.int32, z.shape, 1)
            z = jnp.where(kr <= qc, z, -jnp.inf)
            zmax = _col_reduce(jnp.max, z)
        m_new = jnp.maximum(m_row, zmax + f_row)
        return m_new, jnp.exp2(m_row - m_new), jnp.exp2(z - (m_new - f_row)).astype(BF16)

    def trip(j, slot, masked=False):
        zmax_next = []
        for a in range(hp):
            if masked:
                z_sc[1 - slot, a, :, upper] = qk(a, j + 1, qa[a][upper])
            else:
                zn = qk(a, j + 1, qa[a])
                z_sc[1 - slot, a] = zn
                zmax_next.append(_col_reduce(jnp.max, zn))
        pvs = [pv(a, jnp.maximum(j - 1, 0), p_sc[a]) for a in range(hp)]
        for a in range(hp):
            if masked:
                z_lo = z_sc[slot, a, :, :tk]
                kr = lax.broadcasted_iota(jnp.int32, z_lo.shape, 0)
                qc = lax.broadcasted_iota(jnp.int32, z_lo.shape, 1)
                z_lo = jnp.where(kr <= qc, z_lo, -jnp.inf)
                zmax = jnp.concatenate([_col_reduce(jnp.max, z_lo), zmax_sc[a][:, upper]], axis=1)
                m_new = jnp.maximum(m_sc[a], zmax + f_t[a])
                shift = m_new - f_t[a]
                alpha = jnp.exp2(m_sc[a] - m_new)
                p_sc[a, :, :tk] = jnp.exp2(z_lo - shift[:, :tk]).astype(BF16)
                p_sc[a, :, upper] = jnp.exp2(z_sc[slot, a, :, upper] - shift[:, upper]).astype(BF16)
            else:
                m_new, alpha, p = softmax(z_sc[slot, a], m_sc[a], zmax_sc[a], f_t[a], False)
                p_sc[a] = p
                zmax_sc[a] = zmax_next[a]
            m_sc[a] = m_new
            acc_sc[a] = alpha * (acc_sc[a] + pvs[a])

    @pl.when((pl.program_id(0) == 0) & (hg == 0) & (qi == 0))
    def _():
        p_sc[...] = jnp.zeros_like(p_sc)
        skip_sm[0] = 0
        for a in range(hp):
            z0 = qk(a, 0, qa[a])
            z_sc[0, a] = z0
            zmax_sc[a] = _col_reduce(jnp.max, z0)

    for a in range(hp):
        m_sc[a] = jnp.full((1, tq), -jnp.inf, F32)
        acc_sc[a] = jnp.zeros((HEAD_DIM + BF16_SUBLANES, tq), F32)

    start = skip_sm[0]

    @pl.when((start & 1) == 1)
    def _():
        trip(start, 1)

    @pl.loop((start + 1) >> 1, qi)
    def _(i):
        trip(2 * i, 0)
        trip(2 * i + 1, 1)

    first = 2 * qi
    trip(first, 0, masked=True)
    for a in range(hp):
        acc = acc_sc[a] + pv(a, first, p_sc[a])
        _, alpha, p = softmax(z_sc[1, a, :, upper], m_sc[a][:, upper], None, f_t[a][:, upper], True)
        acc_up = alpha * acc[:, upper] + pv(a, first + 1, p)
        out = jnp.concatenate(
            [acc[:HEAD_DIM, :tk] / acc[HEAD_DIM:HEAD_DIM + 1, :tk],
             acc_up[:HEAD_DIM] / acc_up[HEAD_DIM:HEAD_DIM + 1]], axis=1)
        o_ref[a] = out.T.astype(BF16)

    start_next = jnp.where(last_tile, 0, first_live_block(jnp.minimum(qi + 1, nq - 1)))
    skip_sm[0] = start_next
    c0 = pl.multiple_of(start_next * tk, tk)
    for a in range(hp):
        ka_here = jnp.concatenate([k_ref[a, pl.ds(c0, tk), :], kf_ref[pl.ds(c0, tk), :]], axis=1)
        ka_group = jnp.concatenate([k0n_ref[a], kf0n_ref[...]], axis=1)
        zn = lax.dot_general(jnp.where(last_tile, ka_group, ka_here), qa_next[a], _NT_DIMS,
                             preferred_element_type=F32)
        z_sc[start_next & 1, a] = zn
        zmax_sc[a] = _col_reduce(jnp.max, zn)


def _attn(qk, kf, vt, ft, nrm, fend, B, S, tq, tk, hp):
    T = B * S
    nq = S // tq
    n_blk = S // tk
    n_grp = N_HEADS // hp
    assert tq == 2 * tk and nrm.shape[0] == B * n_blk and fend.shape[0] == B * n_blk

    def next_group(b, g):
        gn = jnp.minimum(b * n_grp + g + 1, B * n_grp - 1)
        return gn // n_grp, gn % n_grp

    def q_next(b, g, i):
        bn, gn = next_group(b, g)
        last = i == nq - 1
        return (0, jnp.where(last, gn, g), jnp.where(last, bn * nq, b * nq + i + 1), 0)
    return pl.pallas_call(
        functools.partial(_attn_kernel, tq=tq, tk=tk, hp=hp, nq=nq),
        grid=(B, N_HEADS // hp, nq),
        in_specs=[
            pl.BlockSpec((None, hp, tq, HEAD_DIM), lambda b, g, i: (0, g, b * nq + i, 0)),
            pl.BlockSpec((None, hp, tq, HEAD_DIM), q_next),
            pl.BlockSpec((None, hp, S, HEAD_DIM), lambda b, g, i: (1, g, b, 0)),
            pl.BlockSpec((S, LANES), lambda b, g, i: (b, 0)),
            pl.BlockSpec((None, hp, tk, HEAD_DIM),
                         lambda b, g, i: (1, next_group(b, g)[1], next_group(b, g)[0] * n_blk, 0)),
            pl.BlockSpec((tk, LANES), lambda b, g, i: (next_group(b, g)[0] * n_blk, 0)),
            pl.BlockSpec((hp, HEAD_DIM, S), lambda b, g, i: (g, 0, b)),
            pl.BlockSpec((None, N_HEADS, tq), lambda b, g, i: (b, 0, i)),
            pl.BlockSpec((n_blk, 2, LANES), lambda b, g, i: (b, 0, 0)),
            pl.BlockSpec((n_blk, 1, LANES), lambda b, g, i: (b, 0, 0)),
        ],
        out_specs=pl.BlockSpec((hp, tq, HEAD_DIM), lambda b, g, i: (g, b * nq + i, 0)),
        out_shape=jax.ShapeDtypeStruct((N_HEADS, T, HEAD_DIM), BF16),
        scratch_shapes=[
            pltpu.VMEM((2, hp, tk, tq), F32),
            pltpu.VMEM((hp, tk, tq), BF16),
            pltpu.VMEM((hp, HEAD_DIM + BF16_SUBLANES, tq), F32),
            pltpu.VMEM((hp, 1, tq), F32),
            pltpu.VMEM((hp, 1, tq), F32),
            pltpu.SMEM((1,), jnp.int32),
        ],
        compiler_params=pltpu.CompilerParams(
            dimension_semantics=("arbitrary", "arbitrary", "arbitrary"),
            vmem_limit_bytes=VMEM_LIMIT),
        name="fox_attn",
    )(qk, qk, qk, kf, qk, kf, vt, ft, nrm, fend)


def _merge_mlp_kernel(ob_ref, yag_ref, gb_ref, x_ref, wb_ref, wo_ref, gm_ref, wu_ref,
                      wdn_ref, gf_ref, o_ref):
    ob = jnp.concatenate([ob_ref[h] for h in range(N_HEADS)], axis=1)
    yb = _dot(ob, wb_ref[...])
    mix = yag_ref[...] + gb_ref[...] * yb
    x1 = x_ref[...] + _dot(mix.astype(BF16), wo_ref[...])
    m = _rms(x1, gm_ref[...]).astype(BF16)
    acc = x1
    for c in range(D_FF // D_MODEL):
        cols = slice(c * D_MODEL, (c + 1) * D_MODEL)
        hc = jnp.maximum(_dot(m, wu_ref[:, cols]), 0.0)
        acc = acc + _dot((hc * hc).astype(BF16), wdn_ref[cols, :])
    o_ref[...] = _rms(acc, gf_ref[...])


def _merge_mlp(ob, yag, pf, x2, wb, wo, gm, wu, wdn, gf, tm):
    T = x2.shape[0]
    return pl.pallas_call(
        _merge_mlp_kernel,
        grid=(T // tm,),
        in_specs=[
            pl.BlockSpec((N_HEADS, tm, HEAD_DIM), lambda i: (0, i, 0)),
            pl.BlockSpec((tm, D_MODEL), lambda i: (i, 0)),
            pl.BlockSpec((tm, D_MODEL), lambda i: (i, 3)),
            pl.BlockSpec((tm, D_MODEL), lambda i: (i, 0)),
            _const_spec((D_MODEL, D_MODEL)),
            _const_spec((D_MODEL, D_MODEL)),
            _const_spec((1, D_MODEL)),
            _const_spec((D_MODEL, D_FF)),
            _const_spec((D_FF, D_MODEL)),
            _const_spec((1, D_MODEL)),
        ],
        out_specs=pl.BlockSpec((tm, D_MODEL), lambda i: (i, 0)),
        out_shape=jax.ShapeDtypeStruct((T, D_MODEL), F32),
        compiler_params=pltpu.CompilerParams(
            dimension_semantics=("arbitrary",), vmem_limit_bytes=VMEM_LIMIT),
        name="merge_mlp",
    )(ob, yag, pf, x2, wb, wo, gm, wu, wdn, gf)


def _block_diag_pairs(wa, wx):
    def pair(w):
        w = w.reshape(LRU_BLOCKS // 2, 2, LRU_BW, LRU_BW)
        z = jnp.zeros_like(w[:, 0])
        top = jnp.concatenate([w[:, 0], z], axis=2)
        bot = jnp.concatenate([z, w[:, 1]], axis=2)
        return jnp.concatenate([top, bot], axis=1)
    return jnp.concatenate([pair(wa), pair(wx)], axis=2).astype(BF16)


def kernel(x, norm_mix_g, w_in, conv_w, conv_b, lru_wa, lru_ba, lru_wx, lru_bx, lru_lambda,
           forget_b, w_branch_a, w_branch_b, w_out, norm_mlp_g, w_up, w_down, norm_final_g):
    B, S, D = x.shape
    assert D == D_MODEL
    T = B * S
    tm = min(512, S)
    tq = min(1024, S)
    tk = tq // 2
    assert S % tm == 0 and S % tq == 0 and tm % LANES == 0

    w_main = w_in.astype(BF16)
    w_vt = w_in[:, W_IN_V_SLAB * D:(W_IN_V_SLAB + 1) * D].T.astype(BF16)
    w_f = jnp.pad(w_in[:, 7 * D:], ((0, 0), (0, LANES - N_HEADS))).astype(BF16)
    wd = _block_diag_pairs(0.5 * lru_wa, 0.5 * lru_wx)
    row = lambda v: v.reshape(1, -1).astype(F32)
    fb = jnp.pad(forget_b, (0, LANES - N_HEADS)).reshape(1, LANES)
    e_rows = jnp.arange(F_PARTS * LANES)
    e_tgt = jnp.where(e_rows % LANES < N_HEADS, (e_rows % LANES) * F_PARTS + e_rows // LANES, -1)
    e = (e_tgt[:, None] == jnp.arange(LANES)[None, :]).astype(BF16)

    x2 = x.reshape(T, D)
    pf, qk, vt, fl, nrm = _in_proj(x2, row(norm_mix_g), w_main, w_vt, w_f, tm)
    yag, kf, ft, fend = _lru(pf, fl, conv_w, row(conv_b), wd, row(0.5 * lru_ba), row(0.5 * lru_bx),
                             row(lru_lambda), fb, w_branch_a.astype(BF16), e, B, S, tm)
    assert tk == tm
    ob = _attn(qk, kf, vt, ft, nrm, fend, B, S, tq, tk, hp=2)
    out = _merge_mlp(ob, yag, pf, x2, w_branch_b.astype(BF16), w_out.astype(BF16),
                     row(norm_mlp_g), w_up.astype(BF16), w_down.astype(BF16),
                     row(norm_final_g), tm)
    return out.reshape(B, S, D)
```

```python
import functools
import math

import jax
import jax.numpy as jnp
from jax import lax
from jax.experimental import pallas as pl
from jax.experimental.pallas import tpu as pltpu

D_MODEL = 1024
N_HEADS = 8
HEAD_DIM = 128
LRU_BLOCKS = 16
LRU_BW = 64
CONV_W = 4
LRU_C = 8.0
D_FF = 4 * D_MODEL
RMS_EPS = 1e-6
LANES = 128
SUBLANES = 8
BF16_SUBLANES = 16
LOG2E = 1.4426950408889634
Q_SCALE = LOG2E / math.sqrt(HEAD_DIM)
VMEM_LIMIT = 56 * 1024 * 1024

BF16 = jnp.bfloat16
F32 = jnp.float32
_NT_DIMS = (((1,), (1,)), ((), ()))
W_IN_PF_SLABS = (0, 1, 5, 6)
W_IN_QK_SLABS = (2, 3)
W_IN_V_SLAB = 4
P_UNDERFLOW_LOG2 = 160.0
NORM_MARGIN = 1.02
F_PARTS = 3


def _dot(a, b):
    return jnp.dot(a, b, preferred_element_type=F32)


def _rms(x, g):
    return x * lax.rsqrt(jnp.mean(x * x, axis=-1, keepdims=True) + RMS_EPS) * g


def _softplus(x):
    return jnp.maximum(x, 0.0) + jnp.log1p(jnp.exp(-jnp.abs(x)))


def _sigmoid(x):
    return 0.5 * jnp.tanh(0.5 * x) + 0.5


def _gelu_tanh(x):
    c = math.sqrt(2.0 / math.pi)
    half_x = 0.5 * x
    return half_x + half_x * jnp.tanh(x * (c + (c * 0.044715) * (x * x)))


def _const_spec(shape):
    nd = len(shape)
    return pl.BlockSpec(shape, lambda *_: (0,) * nd, pipeline_mode=pl.Buffered(1))


def _in_proj_kernel(x_ref, g_ref, w_ref, wvt_ref, wf_ref, xl_ref, g3_ref, qk_ref, vt_ref, fl_ref,
                    nrm_ref):
    u = _rms(x_ref[...], g_ref[...]).astype(BF16)
    fl_ref[...] = _dot(u, wf_ref[...])
    post = (None, _gelu_tanh, _sigmoid, _sigmoid)
    for j, src in enumerate(W_IN_PF_SLABS):
        res = _dot(u, w_ref[:, src * D_MODEL:(src + 1) * D_MODEL])
        if post[j] is None:
            xl_ref[...] = res
        else:
            g3_ref[:, (j - 1) * D_MODEL:j * D_MODEL] = post[j](res).astype(BF16)
    for j, src in enumerate(W_IN_QK_SLABS):
        res = _dot(u, w_ref[:, src * D_MODEL:(src + 1) * D_MODEL])
        if j == 0:
            res = res * Q_SCALE
        lane = lax.broadcasted_iota(jnp.int32, (1, LANES), 1)
        nrm = jnp.zeros((1, LANES), F32)
        for h in range(N_HEADS):
            xb = res[:, h * HEAD_DIM:(h + 1) * HEAD_DIM].astype(BF16)
            qk_ref[j, h] = xb
            xf = xb.astype(F32)
            n2 = jnp.max(jnp.sum(xf * xf, axis=1, keepdims=True), axis=0, keepdims=True)
            nrm = jnp.where(lane == h, n2, nrm)
        nrm_ref[j:j + 1, :] = nrm
    vt = lax.dot_general(wvt_ref[...], u, _NT_DIMS, preferred_element_type=F32)
    for h in range(N_HEADS):
        vt_ref[h] = vt[h * HEAD_DIM:(h + 1) * HEAD_DIM, :].astype(BF16)


def _in_proj(x2, g, w_main, w_vt, w_f, tm):
    T = x2.shape[0]
    return pl.pallas_call(
        _in_proj_kernel,
        grid=(T // tm,),
        in_specs=[
            pl.BlockSpec((tm, D_MODEL), lambda i: (i, 0)),
            _const_spec((1, D_MODEL)),
            _const_spec(w_main.shape),
            _const_spec((D_MODEL, D_MODEL)),
            _const_spec((D_MODEL, LANES)),
        ],
        out_specs=[
            pl.BlockSpec((tm, D_MODEL), lambda i: (i, 0)),
            pl.BlockSpec((tm, 3 * D_MODEL), lambda i: (i, 0)),
            pl.BlockSpec((2, N_HEADS, tm, HEAD_DIM), lambda i: (0, 0, i, 0)),
            pl.BlockSpec((N_HEADS, HEAD_DIM, tm), lambda i: (0, 0, i)),
            pl.BlockSpec((tm, LANES), lambda i: (i, 0)),
            pl.BlockSpec((None, 2, LANES), lambda i: (i, 0, 0)),
        ],
        out_shape=[
            jax.ShapeDtypeStruct((T, D_MODEL), F32),
            jax.ShapeDtypeStruct((T, 3 * D_MODEL), BF16),
            jax.ShapeDtypeStruct((2, N_HEADS, T, HEAD_DIM), BF16),
            jax.ShapeDtypeStruct((N_HEADS, HEAD_DIM, T), BF16),
            jax.ShapeDtypeStruct((T, LANES), F32),
            jax.ShapeDtypeStruct((T // tm, 2, LANES), F32),
        ],
        compiler_params=pltpu.CompilerParams(
            dimension_semantics=("arbitrary",), vmem_limit_bytes=VMEM_LIMIT),
        name="in_proj",
    )(x2, g, w_main, w_vt, w_f)


def _lru_kernel(xl_ref, gl_ref, ga_ref, fl_ref, cw_ref, cb_ref, wd_ref, ba_ref, bx_ref,
                lam_ref, fb_ref, wa_ref, e_ref,
                yag_ref, kf_ref, ft_ref, fend_ref,
                xq_sc, a_sc, b_sc, h_sc, halo_sc, hc_sc, fc_sc, *, tm):
    s = pl.program_id(1)
    seg = tm // SUBLANES
    pitch = seg + SUBLANES
    gps = seg // SUBLANES
    nc = D_MODEL // LANES
    n_halo = CONV_W - 1

    @pl.when(s == 0)
    def _():
        halo_sc[...] = jnp.zeros_like(halo_sc)
        hc_sc[...] = jnp.zeros_like(hc_sc)
        fc_sc[...] = jnp.zeros_like(fc_sc)

    for g in range(tm // SUBLANES):
        u, j0 = g // gps, (g % gps) * SUBLANES
        for c in range(nc):
            xq_sc[c, pl.ds(j0 * SUBLANES + u, SUBLANES, stride=SUBLANES), :] = (
                xl_ref[g * SUBLANES:(g + 1) * SUBLANES, c * LANES:(c + 1) * LANES])

    sub = lax.broadcasted_iota(jnp.int32, (SUBLANES, LANES), 0)
    c_all = -LRU_C * _softplus(-lam_ref[...])
    for c in range(nc):
        cols = slice(c * LANES, (c + 1) * LANES)
        x = xq_sc[c]
        wrap = []
        for q in range(n_halo):
            rows = slice(q * SUBLANES, (q + 1) * SUBLANES)
            rolled = pltpu.roll(x[tm - n_halo * SUBLANES:, :][rows], 1, axis=0)
            wrap.append(jnp.where(sub == 0, halo_sc[c, rows, :], rolled))
            halo_sc[c, rows, :] = rolled
        xa = cb_ref[:, cols] + cw_ref[CONV_W - 1:CONV_W, cols] * x
        for d in range(1, CONV_W):
            xd = jnp.concatenate(wrap[n_halo - d:] + [x[:tm - d * SUBLANES, :]], axis=0)
            xa = xa + cw_ref[CONV_W - 1 - d:CONV_W - d, cols] * xd
        ri = _dot(xa.astype(BF16), wd_ref[c])
        t_r = jnp.tanh(ri[:, :LANES] + ba_ref[:, cols])
        t_i = jnp.tanh(ri[:, LANES:] + bx_ref[:, cols])
        half_c = 0.5 * c_all[:, cols]
        a = jnp.exp(half_c * t_r + half_c)
        half_xa = 0.5 * xa
        one_m_a2 = 1.0 - a * a
        mult = one_m_a2 * lax.rsqrt(jnp.maximum(one_m_a2, 1e-30))
        a_sc[c] = a
        b_sc[c] = mult * (half_xa * t_i + half_xa)

    def scan_step(j, carry):
        r0 = pl.multiple_of(j * SUBLANES, SUBLANES)
        out = []
        for c in range(nc):
            h, p = carry[c]
            a = a_sc[c, pl.ds(r0, SUBLANES), :]
            out.append((a * h + b_sc[c, pl.ds(r0, SUBLANES), :], a * p))
        return tuple(out)

    ends = lax.fori_loop(
        0, seg, scan_step,
        tuple((jnp.zeros((SUBLANES, LANES), F32), jnp.ones((SUBLANES, LANES), F32))
              for _ in range(nc)), unroll=SUBLANES)

    carry_in = []
    for c in range(nc):
        cols = slice(c * LANES, (c + 1) * LANES)
        h_end, p_end = ends[c]
        for d in (1, 2, 4):
            keep = sub >= d
            p_sh = jnp.where(keep, pltpu.roll(p_end, d, axis=0), 1.0)
            h_sh = jnp.where(keep, pltpu.roll(h_end, d, axis=0), 0.0)
            h_end = p_end * h_sh + h_end
            p_end = p_end * p_sh
        blk_in = hc_sc[:, cols]
        seg_out = h_end + p_end * blk_in
        carry_in.append(jnp.where(sub == 0, blk_in, pltpu.roll(seg_out, 1, axis=0)))
        hc_sc[:, cols] = jnp.broadcast_to(seg_out[SUBLANES - 1:SUBLANES, :], (SUBLANES, LANES))

    def final_step(j, carry):
        r0 = pl.multiple_of(j * SUBLANES, SUBLANES)
        out = []
        for c in range(nc):
            h = a_sc[c, pl.ds(r0, SUBLANES), :] * carry[c] + b_sc[c, pl.ds(r0, SUBLANES), :]
            h_sc[c, pl.ds(j, SUBLANES, stride=pitch), :] = h
            out.append(h)
        return tuple(out)

    lax.fori_loop(0, seg, final_step, tuple(carry_in), unroll=SUBLANES)

    h = jnp.concatenate(
        [jnp.concatenate([h_sc[c, u * pitch:u * pitch + seg, :] for u in range(SUBLANES)], axis=0)
         for c in range(nc)], axis=1)
    y = (gl_ref[...] * h).astype(BF16)
    yag_ref[...] = ga_ref[...] * _dot(y, wa_ref[...])

    lf = -_softplus(-(fl_ref[...] + fb_ref[...])) * LOG2E
    rowf = lax.broadcasted_iota(jnp.int32, (tm, LANES), 0)
    d = 1
    while d < tm:
        lf = lf + jnp.where(rowf >= d, pltpu.roll(lf, d, axis=0), 0.0)
        d *= 2
    fblk = lf + fc_sc[...]
    fc_sc[...] = fblk[tm - 1:tm, :]
    ft_ref[...] = fblk.T[0:N_HEADS, :]
    fend_ref[...] = fblk[tm - 1:tm, :]
    hi = (-fblk).astype(BF16)
    r1 = -fblk - hi.astype(F32)
    mid = r1.astype(BF16)
    lo = (r1 - mid.astype(F32)).astype(BF16)
    kf_ref[...] = _dot(jnp.concatenate([hi, mid, lo], axis=1), e_ref[...]).astype(BF16)


def _lru(xl, g3, fl, cw, cb, wd, ba, bx, lam, fb, wa, e, B, S, tm):
    T = B * S
    ns = S // tm
    row_blk = lambda c: pl.BlockSpec((tm, D_MODEL), lambda b, s, c=c: (b * ns + s, c))
    return pl.pallas_call(
        functools.partial(_lru_kernel, tm=tm),
        grid=(B, ns),
        in_specs=[
            row_blk(0), row_blk(0), row_blk(1),
            pl.BlockSpec((tm, LANES), lambda b, s: (b * ns + s, 0)),
            _const_spec((CONV_W, D_MODEL)),
            _const_spec((1, D_MODEL)),
            _const_spec((D_MODEL // LANES, LANES, 2 * LANES)),
            _const_spec((1, D_MODEL)),
            _const_spec((1, D_MODEL)),
            _const_spec((1, D_MODEL)),
            _const_spec((1, LANES)),
            _const_spec((D_MODEL, D_MODEL)),
            _const_spec((F_PARTS * LANES, LANES)),
        ],
        out_specs=[
            pl.BlockSpec((tm, D_MODEL), lambda b, s: (b * ns + s, 0)),
            pl.BlockSpec((tm, LANES), lambda b, s: (b * ns + s, 0)),
            pl.BlockSpec((None, N_HEADS, tm), lambda b, s: (b, 0, s)),
            pl.BlockSpec((None, 1, LANES), lambda b, s: (b * ns + s, 0, 0)),
        ],
        out_shape=[
            jax.ShapeDtypeStruct((T, D_MODEL), F32),
            jax.ShapeDtypeStruct((T, LANES), BF16),
            jax.ShapeDtypeStruct((B, N_HEADS, S), F32),
            jax.ShapeDtypeStruct((T // tm, 1, LANES), F32),
        ],
        scratch_shapes=[
            pltpu.VMEM((D_MODEL // LANES, tm, LANES), F32),
            pltpu.VMEM((D_MODEL // LANES, tm, LANES), F32),
            pltpu.VMEM((D_MODEL // LANES, tm, LANES), F32),
            pltpu.VMEM((D_MODEL // LANES, tm + SUBLANES * SUBLANES, LANES), F32),
            pltpu.VMEM((D_MODEL // LANES, (CONV_W - 1) * SUBLANES, LANES), F32),
            pltpu.VMEM((SUBLANES, D_MODEL), F32),
            pltpu.VMEM((1, LANES), F32),
        ],
        compiler_params=pltpu.CompilerParams(
            dimension_semantics=("arbitrary", "arbitrary"), vmem_limit_bytes=VMEM_LIMIT),
        name="lru",
    )(xl, g3, g3, fl, cw, cb, wd, ba, bx, lam, fb, wa, e)


def _col_reduce(op, z):
    rows, cols = z.shape
    return op(op(z.reshape(SUBLANES, rows // SUBLANES, cols), axis=0), axis=0, keepdims=True)


def _attn_kernel(q_ref, qn_ref, k_ref, kf_ref, k0n_ref, kf0n_ref, vt_ref, ft_ref, nrm_ref, fend_ref,
                 o_ref, z_sc, p_sc, acc_sc, m_sc, zmax_sc, skip_sm, *, tq, tk, hp, nq):
    hg = pl.program_id(1)
    qi = pl.program_id(2)
    n_blk = nrm_ref.shape[0]
    head_lane = lax.broadcasted_iota(jnp.int32, (1, LANES), 1)
    blk = lax.broadcasted_iota(jnp.int32, (n_blk, LANES), 0)

    def first_live_block(tile):
        q2 = jnp.maximum(nrm_ref[2 * tile, 0:1, :], nrm_ref[2 * tile + 1, 0:1, :])
        k2 = jnp.max(nrm_ref[:, 1, :], axis=0, keepdims=True)
        bound = (2.0 * NORM_MARGIN) * jnp.sqrt(q2 * k2) + fend_ref[jnp.maximum(2 * tile - 1, 0)]
        dead = (bound - fend_ref[:, 0, :] <= -P_UNDERFLOW_LOG2) & (blk < 2 * tile)
        first_live = jnp.min(jnp.where(dead, n_blk, blk), axis=0, keepdims=True)
        mine = (head_lane >= hg * hp) & (head_lane < (hg + 1) * hp)
        return jnp.min(jnp.where(mine, first_live, n_blk))

    lane = lax.broadcasted_iota(jnp.int32, (tq, LANES), 1)
    upper = slice(tk, tq)
    n_groups = pl.num_programs(0) * pl.num_programs(1)
    group_next = jnp.minimum(pl.program_id(0) * pl.num_programs(1) + hg + 1, n_groups - 1)
    last_tile = qi == nq - 1
    hg_next = jnp.where(last_tile, group_next % pl.num_programs(1), hg)

    def piece_ones(h):
        return jnp.where((lane >= F_PARTS * h) & (lane < F_PARTS * (h + 1)), 1.0, 0.0).astype(BF16)

    qa, qa_next, f_t = [], [], []
    for a in range(hp):
        h = hg * hp + a
        qa.append(jnp.concatenate([q_ref[a], piece_ones(h)], axis=1))
        qa_next.append(jnp.concatenate([qn_ref[a], piece_ones(hg_next * hp + a)], axis=1))
        f_t.append(ft_ref[pl.ds(h, 1), :])

    def qk(a, j, queries):
        c0 = pl.multiple_of(j * tk, tk)
        ka = jnp.concatenate([k_ref[a, pl.ds(c0, tk), :], kf_ref[pl.ds(c0, tk), :]], axis=1)
        return lax.dot_general(ka, queries, _NT_DIMS, preferred_element_type=F32)

    ones_rows = jnp.ones((BF16_SUBLANES, tk), BF16)

    def pv(a, j, p):
        c0 = pl.multiple_of(j * tk, tk)
        vt_aug = jnp.concatenate([vt_ref[a, :, pl.ds(c0, tk)], ones_rows], axis=0)
        return _dot(vt_aug, p)

    def softmax(z, m_row, zmax, f_row, masked):
        if masked:
            kr = lax.broadcasted_iota(jnp.int32, z.shape, 0)
            qc = lax.broadcasted_iota(jnp.int32, z.shape, 1)
            z = jnp.where(kr <= qc, z, -jnp.inf)
            zmax = _col_reduce(jnp.max, z)
        m_new = jnp.maximum(m_row, zmax + f_row)
        return m_new, jnp.exp2(m_row - m_new), jnp.exp2(z - (m_new - f_row)).astype(BF16)

    def trip(j, slot, masked=False):
        zmax_next = []
        for a in range(hp):
            if masked:
                z_sc[1 - slot, a, :, upper] = qk(a, j + 1, qa[a][upper])
            else:
                zn = qk(a, j + 1, qa[a])
                z_sc[1 - slot, a] = zn
                zmax_next.append(_col_reduce(jnp.max, zn))
        pvs = [pv(a, jnp.maximum(j - 1, 0), p_sc[a]) for a in range(hp)]
        for a in range(hp):
            if masked:
                z_lo = z_sc[slot, a, :, :tk]
                kr = lax.broadcasted_iota(jnp.int32, z_lo.shape, 0)
                qc = lax.broadcasted_iota(jnp.int32, z_lo.shape, 1)
                z_lo = jnp.where(kr <= qc, z_lo, -jnp.inf)
                zmax = jnp.concatenate([_col_reduce(jnp.max, z_lo), zmax_sc[a][:, upper]], axis=1)
                m_new = jnp.maximum(m_sc[a], zmax + f_t[a])
                shift = m_new - f_t[a]
                alpha = jnp.exp2(m_sc[a] - m_new)
                p_sc[a, :, :tk] = jnp.exp2(z_lo - shift[:, :tk]).astype(BF16)
                p_sc[a, :, upper] = jnp.exp2(z_sc[slot, a, :, upper] - shift[:, upper]).astype(BF16)
            else:
                m_new, alpha, p = softmax(z_sc[slot, a], m_sc[a], zmax_sc[a], f_t[a], False)
                p_sc[a] = p
                zmax_sc[a] = zmax_next[a]
            m_sc[a] = m_new
            acc_sc[a] = alpha * (acc_sc[a] + pvs[a])

    @pl.when((pl.program_id(0) == 0) & (hg == 0) & (qi == 0))
    def _():
        p_sc[...] = jnp.zeros_like(p_sc)
        skip_sm[0] = 0
        for a in range(hp):
            z0 = qk(a, 0, qa[a])
            z_sc[0, a] = z0
            zmax_sc[a] = _col_reduce(jnp.max, z0)

    for a in range(hp):
        m_sc[a] = jnp.full((1, tq), -jnp.inf, F32)
        acc_sc[a] = jnp.zeros((HEAD_DIM + BF16_SUBLANES, tq), F32)

    start = skip_sm[0]

    @pl.when((start & 1) == 1)
    def _():
        trip(start, 1)

    @pl.loop((start + 1) >> 1, qi)
    def _(i):
        trip(2 * i, 0)
        trip(2 * i + 1, 1)

    first = 2 * qi
    trip(first, 0, masked=True)
    for a in range(hp):
        acc = acc_sc[a] + pv(a, first, p_sc[a])
        _, alpha, p = softmax(z_sc[1, a, :, upper], m_sc[a][:, upper], None, f_t[a][:, upper], True)
        acc_up = alpha * acc[:, upper] + pv(a, first + 1, p)
        out = jnp.concatenate(
            [acc[:HEAD_DIM, :tk] / acc[HEAD_DIM:HEAD_DIM + 1, :tk],
             acc_up[:HEAD_DIM] / acc_up[HEAD_DIM:HEAD_DIM + 1]], axis=1)
        o_ref[a] = out.T.astype(BF16)

    start_next = jnp.where(last_tile, 0, first_live_block(jnp.minimum(qi + 1, nq - 1)))
    skip_sm[0] = start_next
    c0 = pl.multiple_of(start_next * tk, tk)
    for a in range(hp):
        ka_here = jnp.concatenate([k_ref[a, pl.ds(c0, tk), :], kf_ref[pl.ds(c0, tk), :]], axis=1)
        ka_group = jnp.concatenate([k0n_ref[a], kf0n_ref[...]], axis=1)
        zn = lax.dot_general(jnp.where(last_tile, ka_group, ka_here), qa_next[a], _NT_DIMS,
                             preferred_element_type=F32)
        z_sc[start_next & 1, a] = zn
        zmax_sc[a] = _col_reduce(jnp.max, zn)


def _attn(qk, kf, vt, ft, nrm, fend, B, S, tq, tk, hp):
    T = B * S
    nq = S // tq
    n_blk = S // tk
    n_grp = N_HEADS // hp
    assert tq == 2 * tk and nrm.shape[0] == B * n_blk and fend.shape[0] == B * n_blk

    def next_group(b, g):
        gn = jnp.minimum(b * n_grp + g + 1, B * n_grp - 1)
        return gn // n_grp, gn % n_grp

    def q_next(b, g, i):
        bn, gn = next_group(b, g)
        last = i == nq - 1
        return (0, jnp.where(last, gn, g), jnp.where(last, bn * nq, b * nq + i + 1), 0)
    return pl.pallas_call(
        functools.partial(_attn_kernel, tq=tq, tk=tk, hp=hp, nq=nq),
        grid=(B, N_HEADS // hp, nq),
        in_specs=[
            pl.BlockSpec((None, hp, tq, HEAD_DIM), lambda b, g, i: (0, g, b * nq + i, 0)),
            pl.BlockSpec((None, hp, tq, HEAD_DIM), q_next),
            pl.BlockSpec((None, hp, S, HEAD_DIM), lambda b, g, i: (1, g, b, 0)),
            pl.BlockSpec((S, LANES), lambda b, g, i: (b, 0)),
            pl.BlockSpec((None, hp, tk, HEAD_DIM),
                         lambda b, g, i: (1, next_group(b, g)[1], next_group(b, g)[0] * n_blk, 0)),
            pl.BlockSpec((tk, LANES), lambda b, g, i: (next_group(b, g)[0] * n_blk, 0)),
            pl.BlockSpec((hp, HEAD_DIM, S), lambda b, g, i: (g, 0, b)),
            pl.BlockSpec((None, N_HEADS, tq), lambda b, g, i: (b, 0, i)),
            pl.BlockSpec((n_blk, 2, LANES), lambda b, g, i: (b, 0, 0)),
            pl.BlockSpec((n_blk, 1, LANES), lambda b, g, i: (b, 0, 0)),
        ],
        out_specs=pl.BlockSpec((hp, tq, HEAD_DIM), lambda b, g, i: (g, b * nq + i, 0)),
        out_shape=jax.ShapeDtypeStruct((N_HEADS, T, HEAD_DIM), BF16),
        scratch_shapes=[
            pltpu.VMEM((2, hp, tk, tq), F32),
            pltpu.VMEM((hp, tk, tq), BF16),
            pltpu.VMEM((hp, HEAD_DIM + BF16_SUBLANES, tq), F32),
            pltpu.VMEM((hp, 1, tq), F32),
            pltpu.VMEM((hp, 1, tq), F32),
            pltpu.SMEM((1,), jnp.int32),
        ],
        compiler_params=pltpu.CompilerParams(
            dimension_semantics=("arbitrary", "arbitrary", "arbitrary"),
            vmem_limit_bytes=VMEM_LIMIT),
        name="fox_attn",
    )(qk, qk, qk, kf, qk, kf, vt, ft, nrm, fend)


def _merge_mlp_kernel(ob_ref, yag_ref, gb_ref, x_ref, wb_ref, wo_ref, gm_ref, wu_ref,
                      wdn_ref, gf_ref, o_ref):
    ob = jnp.concatenate([ob_ref[h] for h in range(N_HEADS)], axis=1)
    yb = _dot(ob, wb_ref[...])
    mix = yag_ref[...] + gb_ref[...] * yb
    x1 = x_ref[...] + _dot(mix.astype(BF16), wo_ref[...])
    m = _rms(x1, gm_ref[...]).astype(BF16)
    acc = x1
    for c in range(D_FF // D_MODEL):
        cols = slice(c * D_MODEL, (c + 1) * D_MODEL)
        hc = jnp.maximum(_dot(m, wu_ref[:, cols]), 0.0)
        acc = acc + _dot((hc * hc).astype(BF16), wdn_ref[cols, :])
    o_ref[...] = _rms(acc, gf_ref[...])


def _merge_mlp(ob, yag, g3, x2, wb, wo, gm, wu, wdn, gf, tm):
    T = x2.shape[0]
    return pl.pallas_call(
        _merge_mlp_kernel,
        grid=(T // tm,),
        in_specs=[
            pl.BlockSpec((N_HEADS, tm, HEAD_DIM), lambda i: (0, i, 0)),
            pl.BlockSpec((tm, D_MODEL), lambda i: (i, 0)),
            pl.BlockSpec((tm, D_MODEL), lambda i: (i, 2)),
            pl.BlockSpec((tm, D_MODEL), lambda i: (i, 0)),
            _const_spec((D_MODEL, D_MODEL)),
            _const_spec((D_MODEL, D_MODEL)),
            _const_spec((1, D_MODEL)),
            _const_spec((D_MODEL, D_FF)),
            _const_spec((D_FF, D_MODEL)),
            _const_spec((1, D_MODEL)),
        ],
        out_specs=pl.BlockSpec((tm, D_MODEL), lambda i: (i, 0)),
        out_shape=jax.ShapeDtypeStruct((T, D_MODEL), F32),
        compiler_params=pltpu.CompilerParams(
            dimension_semantics=("arbitrary",), vmem_limit_bytes=VMEM_LIMIT),
        name="merge_mlp",
    )(ob, yag, g3, x2, wb, wo, gm, wu, wdn, gf)


def _block_diag_pairs(wa, wx):
    def pair(w):
        w = w.reshape(LRU_BLOCKS // 2, 2, LRU_BW, LRU_BW)
        z = jnp.zeros_like(w[:, 0])
        top = jnp.concatenate([w[:, 0], z], axis=2)
        bot = jnp.concatenate([z, w[:, 1]], axis=2)
        return jnp.concatenate([top, bot], axis=1)
    return jnp.concatenate([pair(wa), pair(wx)], axis=2).astype(BF16)


def kernel(x, norm_mix_g, w_in, conv_w, conv_b, lru_wa, lru_ba, lru_wx, lru_bx, lru_lambda,
           forget_b, w_branch_a, w_branch_b, w_out, norm_mlp_g, w_up, w_down, norm_final_g):
    B, S, D = x.shape
    assert D == D_MODEL
    T = B * S
    tm = min(512, S)
    tq = min(1024, S)
    tk = tq // 2
    assert S % tm == 0 and S % tq == 0 and tm % LANES == 0

    w_main = w_in.astype(BF16)
    w_vt = w_in[:, W_IN_V_SLAB * D:(W_IN_V_SLAB + 1) * D].T.astype(BF16)
    w_f = jnp.pad(w_in[:, 7 * D:], ((0, 0), (0, LANES - N_HEADS))).astype(BF16)
    wd = _block_diag_pairs(0.5 * lru_wa, 0.5 * lru_wx)
    row = lambda v: v.reshape(1, -1).astype(F32)
    fb = jnp.pad(forget_b, (0, LANES - N_HEADS)).reshape(1, LANES)
    e_rows = jnp.arange(F_PARTS * LANES)
    e_tgt = jnp.where(e_rows % LANES < N_HEADS, (e_rows % LANES) * F_PARTS + e_rows // LANES, -1)
    e = (e_tgt[:, None] == jnp.arange(LANES)[None, :]).astype(BF16)

    x2 = x.reshape(T, D)
    xl, g3, qk, vt, fl, nrm = _in_proj(x2, row(norm_mix_g), w_main, w_vt, w_f, tm)
    yag, kf, ft, fend = _lru(xl, g3, fl, conv_w, row(conv_b), wd, row(0.5 * lru_ba), row(0.5 * lru_bx),
                             row(lru_lambda), fb, w_branch_a.astype(BF16), e, B, S, tm)
    assert tk == tm
    ob = _attn(qk, kf, vt, ft, nrm, fend, B, S, tq, tk, hp=2)
    out = _merge_mlp(ob, yag, g3, x2, w_branch_b.astype(BF16), w_out.astype(BF16),
                     row(norm_mlp_g), w_up.astype(BF16), w_down.astype(BF16),
                     row(norm_final_g), tm)
    return out.reshape(B, S, D)
```

```python
import functools
import math

import jax
import jax.numpy as jnp
from jax import lax
from jax.experimental import pallas as pl
from jax.experimental.pallas import tpu as pltpu

D_MODEL = 1024
N_HEADS = 8
HEAD_DIM = 128
LRU_BLOCKS = 16
LRU_BW = 64
CONV_W = 4
LRU_C = 8.0
D_FF = 4 * D_MODEL
RMS_EPS = 1e-6
LANES = 128
SUBLANES = 8
BF16_SUBLANES = 16
LOG2E = 1.4426950408889634
Q_SCALE = LOG2E / math.sqrt(HEAD_DIM)
VMEM_LIMIT = 56 * 1024 * 1024

BF16 = jnp.bfloat16
F32 = jnp.float32
_NT_DIMS = (((1,), (1,)), ((), ()))
W_IN_PF_SLABS = (0, 1, 5, 6)
W_IN_QK_SLABS = (2, 3)
W_IN_V_SLAB = 4
P_UNDERFLOW_LOG2 = 160.0
NORM_MARGIN = 1.02
F_PARTS = 3


def _dot(a, b):
    return jnp.dot(a, b, preferred_element_type=F32)


def _rms(x, g):
    return x * lax.rsqrt(jnp.mean(x * x, axis=-1, keepdims=True) + RMS_EPS) * g


def _softplus(x):
    return jnp.maximum(x, 0.0) + jnp.log1p(jnp.exp(-jnp.abs(x)))


def _sigmoid(x):
    return 0.5 * jnp.tanh(0.5 * x) + 0.5


def _gelu_tanh(x):
    c = math.sqrt(2.0 / math.pi)
    half_x = 0.5 * x
    return half_x + half_x * jnp.tanh(x * (c + (c * 0.044715) * (x * x)))


def _const_spec(shape):
    nd = len(shape)
    return pl.BlockSpec(shape, lambda *_: (0,) * nd, pipeline_mode=pl.Buffered(1))


def _in_proj_kernel(x_ref, g_ref, w_ref, wvt_ref, wf_ref, pf_ref, qk_ref, vt_ref, fl_ref,
                    nrm_ref):
    u = _rms(x_ref[...], g_ref[...]).astype(BF16)
    fl_ref[...] = _dot(u, wf_ref[...])
    post = (None, _gelu_tanh, _sigmoid, _sigmoid)
    for j, src in enumerate(W_IN_PF_SLABS):
        res = _dot(u, w_ref[:, src * D_MODEL:(src + 1) * D_MODEL])
        pf_ref[:, j * D_MODEL:(j + 1) * D_MODEL] = res if post[j] is None else post[j](res)
    for j, src in enumerate(W_IN_QK_SLABS):
        res = _dot(u, w_ref[:, src * D_MODEL:(src + 1) * D_MODEL])
        if j == 0:
            res = res * Q_SCALE
        lane = lax.broadcasted_iota(jnp.int32, (1, LANES), 1)
        nrm = jnp.zeros((1, LANES), F32)
        for h in range(N_HEADS):
            xb = res[:, h * HEAD_DIM:(h + 1) * HEAD_DIM].astype(BF16)
            qk_ref[j, h] = xb
            xf = xb.astype(F32)
            n2 = jnp.max(jnp.sum(xf * xf, axis=1, keepdims=True), axis=0, keepdims=True)
            nrm = jnp.where(lane == h, n2, nrm)
        nrm_ref[j:j + 1, :] = nrm
    vt = lax.dot_general(wvt_ref[...], u, _NT_DIMS, preferred_element_type=F32)
    for h in range(N_HEADS):
        vt_ref[h] = vt[h * HEAD_DIM:(h + 1) * HEAD_DIM, :].astype(BF16)


def _in_proj(x2, g, w_main, w_vt, w_f, tm):
    T = x2.shape[0]
    return pl.pallas_call(
        _in_proj_kernel,
        grid=(T // tm,),
        in_specs=[
            pl.BlockSpec((tm, D_MODEL), lambda i: (i, 0)),
            _const_spec((1, D_MODEL)),
            _const_spec(w_main.shape),
            _const_spec((D_MODEL, D_MODEL)),
            _const_spec((D_MODEL, LANES)),
        ],
        out_specs=[
            pl.BlockSpec((tm, 4 * D_MODEL), lambda i: (i, 0)),
            pl.BlockSpec((2, N_HEADS, tm, HEAD_DIM), lambda i: (0, 0, i, 0)),
            pl.BlockSpec((N_HEADS, HEAD_DIM, tm), lambda i: (0, 0, i)),
            pl.BlockSpec((tm, LANES), lambda i: (i, 0)),
            pl.BlockSpec((None, 2, LANES), lambda i: (i, 0, 0)),
        ],
        out_shape=[
            jax.ShapeDtypeStruct((T, 4 * D_MODEL), F32),
            jax.ShapeDtypeStruct((2, N_HEADS, T, HEAD_DIM), BF16),
            jax.ShapeDtypeStruct((N_HEADS, HEAD_DIM, T), BF16),
            jax.ShapeDtypeStruct((T, LANES), F32),
            jax.ShapeDtypeStruct((T // tm, 2, LANES), F32),
        ],
        compiler_params=pltpu.CompilerParams(
            dimension_semantics=("arbitrary",), vmem_limit_bytes=VMEM_LIMIT),
        name="in_proj",
    )(x2, g, w_main, w_vt, w_f)


def _lru_kernel(xl_ref, gl_ref, ga_ref, fl_ref, cw_ref, cb_ref, wd_ref, ba_ref, bx_ref,
                lam_ref, fb_ref, wa_ref, e_ref,
                yag_ref, kf_ref, ft_ref, fend_ref,
                xq_sc, a_sc, b_sc, h_sc, halo_sc, hc_sc, fc_sc, *, tm):
    s = pl.program_id(1)
    seg = tm // SUBLANES
    pitch = seg + SUBLANES
    gps = seg // SUBLANES
    nc = D_MODEL // LANES
    n_halo = CONV_W - 1

    @pl.when(s == 0)
    def _():
        halo_sc[...] = jnp.zeros_like(halo_sc)
        hc_sc[...] = jnp.zeros_like(hc_sc)
        fc_sc[...] = jnp.zeros_like(fc_sc)

    for g in range(tm // SUBLANES):
        u, j0 = g // gps, (g % gps) * SUBLANES
        for c in range(nc):
            xq_sc[c, pl.ds(j0 * SUBLANES + u, SUBLANES, stride=SUBLANES), :] = (
                xl_ref[g * SUBLANES:(g + 1) * SUBLANES, c * LANES:(c + 1) * LANES])

    sub = lax.broadcasted_iota(jnp.int32, (SUBLANES, LANES), 0)
    c_all = -LRU_C * _softplus(-lam_ref[...])
    for c in range(nc):
        cols = slice(c * LANES, (c + 1) * LANES)
        x = xq_sc[c]
        wrap = []
        for q in range(n_halo):
            rows = slice(q * SUBLANES, (q + 1) * SUBLANES)
            rolled = pltpu.roll(x[tm - n_halo * SUBLANES:, :][rows], 1, axis=0)
            wrap.append(jnp.where(sub == 0, halo_sc[c, rows, :], rolled))
            halo_sc[c, rows, :] = rolled
        xa = cb_ref[:, cols] + cw_ref[CONV_W - 1:CONV_W, cols] * x
        for d in range(1, CONV_W):
            xd = jnp.concatenate(wrap[n_halo - d:] + [x[:tm - d * SUBLANES, :]], axis=0)
            xa = xa + cw_ref[CONV_W - 1 - d:CONV_W - d, cols] * xd
        ri = _dot(xa.astype(BF16), wd_ref[c])
        t_r = jnp.tanh(ri[:, :LANES] + ba_ref[:, cols])
        t_i = jnp.tanh(ri[:, LANES:] + bx_ref[:, cols])
        half_c = 0.5 * c_all[:, cols]
        a = jnp.exp(half_c * t_r + half_c)
        half_xa = 0.5 * xa
        one_m_a2 = 1.0 - a * a
        mult = one_m_a2 * lax.rsqrt(jnp.maximum(one_m_a2, 1e-30))
        a_sc[c] = a
        b_sc[c] = mult * (half_xa * t_i + half_xa)

    def scan_step(j, carry):
        r0 = pl.multiple_of(j * SUBLANES, SUBLANES)
        out = []
        for c in range(nc):
            h, p = carry[c]
            a = a_sc[c, pl.ds(r0, SUBLANES), :]
            out.append((a * h + b_sc[c, pl.ds(r0, SUBLANES), :], a * p))
        return tuple(out)

    ends = lax.fori_loop(
        0, seg, scan_step,
        tuple((jnp.zeros((SUBLANES, LANES), F32), jnp.ones((SUBLANES, LANES), F32))
              for _ in range(nc)), unroll=SUBLANES)

    carry_in = []
    for c in range(nc):
        cols = slice(c * LANES, (c + 1) * LANES)
        h_end, p_end = ends[c]
        for d in (1, 2, 4):
            keep = sub >= d
            p_sh = jnp.where(keep, pltpu.roll(p_end, d, axis=0), 1.0)
            h_sh = jnp.where(keep, pltpu.roll(h_end, d, axis=0), 0.0)
            h_end = p_end * h_sh + h_end
            p_end = p_end * p_sh
        blk_in = hc_sc[:, cols]
        seg_out = h_end + p_end * blk_in
        carry_in.append(jnp.where(sub == 0, blk_in, pltpu.roll(seg_out, 1, axis=0)))
        hc_sc[:, cols] = jnp.broadcast_to(seg_out[SUBLANES - 1:SUBLANES, :], (SUBLANES, LANES))

    def final_step(j, carry):
        r0 = pl.multiple_of(j * SUBLANES, SUBLANES)
        out = []
        for c in range(nc):
            h = a_sc[c, pl.ds(r0, SUBLANES), :] * carry[c] + b_sc[c, pl.ds(r0, SUBLANES), :]
            h_sc[c, pl.ds(j, SUBLANES, stride=pitch), :] = h
            out.append(h)
        return tuple(out)

    lax.fori_loop(0, seg, final_step, tuple(carry_in), unroll=SUBLANES)

    h = jnp.concatenate(
        [jnp.concatenate([h_sc[c, u * pitch:u * pitch + seg, :] for u in range(SUBLANES)], axis=0)
         for c in range(nc)], axis=1)
    y = (gl_ref[...] * h).astype(BF16)
    yag_ref[...] = ga_ref[...] * _dot(y, wa_ref[...])

    lf = -_softplus(-(fl_ref[...] + fb_ref[...])) * LOG2E
    rowf = lax.broadcasted_iota(jnp.int32, (tm, LANES), 0)
    d = 1
    while d < tm:
        lf = lf + jnp.where(rowf >= d, pltpu.roll(lf, d, axis=0), 0.0)
        d *= 2
    fblk = lf + fc_sc[...]
    fc_sc[...] = fblk[tm - 1:tm, :]
    ft_ref[...] = fblk.T[0:N_HEADS, :]
    fend_ref[...] = fblk[tm - 1:tm, :]
    hi = (-fblk).astype(BF16)
    r1 = -fblk - hi.astype(F32)
    mid = r1.astype(BF16)
    lo = (r1 - mid.astype(F32)).astype(BF16)
    kf_ref[...] = _dot(jnp.concatenate([hi, mid, lo], axis=1), e_ref[...]).astype(BF16)


def _lru(pf, fl, cw, cb, wd, ba, bx, lam, fb, wa, e, B, S, tm):
    T = B * S
    ns = S // tm
    row_blk = lambda c: pl.BlockSpec((tm, D_MODEL), lambda b, s, c=c: (b * ns + s, c))
    return pl.pallas_call(
        functools.partial(_lru_kernel, tm=tm),
        grid=(B, ns),
        in_specs=[
            row_blk(0), row_blk(1), row_blk(2),
            pl.BlockSpec((tm, LANES), lambda b, s: (b * ns + s, 0)),
            _const_spec((CONV_W, D_MODEL)),
            _const_spec((1, D_MODEL)),
            _const_spec((D_MODEL // LANES, LANES, 2 * LANES)),
            _const_spec((1, D_MODEL)),
            _const_spec((1, D_MODEL)),
            _const_spec((1, D_MODEL)),
            _const_spec((1, LANES)),
            _const_spec((D_MODEL, D_MODEL)),
            _const_spec((F_PARTS * LANES, LANES)),
        ],
        out_specs=[
            pl.BlockSpec((tm, D_MODEL), lambda b, s: (b * ns + s, 0)),
            pl.BlockSpec((tm, LANES), lambda b, s: (b * ns + s, 0)),
            pl.BlockSpec((None, N_HEADS, tm), lambda b, s: (b, 0, s)),
            pl.BlockSpec((None, 1, LANES), lambda b, s: (b * ns + s, 0, 0)),
        ],
        out_shape=[
            jax.ShapeDtypeStruct((T, D_MODEL), F32),
            jax.ShapeDtypeStruct((T, LANES), BF16),
            jax.ShapeDtypeStruct((B, N_HEADS, S), F32),
            jax.ShapeDtypeStruct((T // tm, 1, LANES), F32),
        ],
        scratch_shapes=[
            pltpu.VMEM((D_MODEL // LANES, tm, LANES), F32),
            pltpu.VMEM((D_MODEL // LANES, tm, LANES), F32),
            pltpu.VMEM((D_MODEL // LANES, tm, LANES), F32),
            pltpu.VMEM((D_MODEL // LANES, tm + SUBLANES * SUBLANES, LANES), F32),
            pltpu.VMEM((D_MODEL // LANES, (CONV_W - 1) * SUBLANES, LANES), F32),
            pltpu.VMEM((SUBLANES, D_MODEL), F32),
            pltpu.VMEM((1, LANES), F32),
        ],
        compiler_params=pltpu.CompilerParams(
            dimension_semantics=("arbitrary", "arbitrary"), vmem_limit_bytes=VMEM_LIMIT),
        name="lru",
    )(pf, pf, pf, fl, cw, cb, wd, ba, bx, lam, fb, wa, e)


def _col_reduce(op, z):
    rows, cols = z.shape
    return op(op(z.reshape(SUBLANES, rows // SUBLANES, cols), axis=0), axis=0, keepdims=True)


def _attn_kernel(q_ref, qn_ref, k_ref, kf_ref, k0n_ref, kf0n_ref, vt_ref, ft_ref, nrm_ref, fend_ref,
                 o_ref, z_sc, p_sc, acc_sc, m_sc, zmax_sc, skip_sm, *, tq, tk, hp, nq):
    hg = pl.program_id(1)
    qi = pl.program_id(2)
    n_blk = nrm_ref.shape[0]
    head_lane = lax.broadcasted_iota(jnp.int32, (1, LANES), 1)
    blk = lax.broadcasted_iota(jnp.int32, (n_blk, LANES), 0)

    def first_live_block(tile):
        q2 = jnp.maximum(nrm_ref[2 * tile, 0:1, :], nrm_ref[2 * tile + 1, 0:1, :])
        k2 = jnp.max(nrm_ref[:, 1, :], axis=0, keepdims=True)
        bound = (2.0 * NORM_MARGIN) * jnp.sqrt(q2 * k2) + fend_ref[jnp.maximum(2 * tile - 1, 0)]
        dead = (bound - fend_ref[:, 0, :] <= -P_UNDERFLOW_LOG2) & (blk < 2 * tile)
        first_live = jnp.min(jnp.where(dead, n_blk, blk), axis=0, keepdims=True)
        mine = (head_lane >= hg * hp) & (head_lane < (hg + 1) * hp)
        return jnp.min(jnp.where(mine, first_live, n_blk))

    lane = lax.broadcasted_iota(jnp.int32, (tq, LANES), 1)
    upper = slice(tk, tq)
    n_groups = pl.num_programs(0) * pl.num_programs(1)
    group_next = jnp.minimum(pl.program_id(0) * pl.num_programs(1) + hg + 1, n_groups - 1)
    last_tile = qi == nq - 1
    hg_next = jnp.where(last_tile, group_next % pl.num_programs(1), hg)

    def piece_ones(h):
        return jnp.where((lane >= F_PARTS * h) & (lane < F_PARTS * (h + 1)), 1.0, 0.0).astype(BF16)

    qa, qa_next, f_t = [], [], []
    for a in range(hp):
        h = hg * hp + a
        qa.append(jnp.concatenate([q_ref[a], piece_ones(h)], axis=1))
        qa_next.append(jnp.concatenate([qn_ref[a], piece_ones(hg_next * hp + a)], axis=1))
        f_t.append(ft_ref[pl.ds(h, 1), :])

    def qk(a, j, queries):
        c0 = pl.multiple_of(j * tk, tk)
        ka = jnp.concatenate([k_ref[a, pl.ds(c0, tk), :], kf_ref[pl.ds(c0, tk), :]], axis=1)
        return lax.dot_general(ka, queries, _NT_DIMS, preferred_element_type=F32)

    ones_rows = jnp.ones((BF16_SUBLANES, tk), BF16)

    def pv(a, j, p):
        c0 = pl.multiple_of(j * tk, tk)
        vt_aug = jnp.concatenate([vt_ref[a, :, pl.ds(c0, tk)], ones_rows], axis=0)
        return _dot(vt_aug, p)

    def softmax(z, m_row, zmax, f_row, masked):
        if masked:
            kr = lax.broadcasted_iota(jnp.int32, z.shape, 0)
            qc = lax.broadcasted_iota(jnp.int32, z.shape, 1)
            z = jnp.where(kr <= qc, z, -jnp.inf)
            zmax = _col_reduce(jnp.max, z)
        m_new = jnp.maximum(m_row, zmax + f_row)
        return m_new, jnp.exp2(m_row - m_new), jnp.exp2(z - (m_new - f_row)).astype(BF16)

    def trip(j, slot, masked=False):
        zmax_next, pvs = [], []
        for a in range(hp):
            if masked:
                z_sc[1 - slot, a, :, upper] = qk(a, j + 1, qa[a][upper])
            else:
                zn = qk(a, j + 1, qa[a])
                z_sc[1 - slot, a] = zn
                zmax_next.append(_col_reduce(jnp.max, zn))
            pvs.append(pv(a, jnp.maximum(j - 1, 0), p_sc[a]))
        for a in range(hp):
            if masked:
                z_lo = z_sc[slot, a, :, :tk]
                kr = lax.broadcasted_iota(jnp.int32, z_lo.shape, 0)
                qc = lax.broadcasted_iota(jnp.int32, z_lo.shape, 1)
                z_lo = jnp.where(kr <= qc, z_lo, -jnp.inf)
                zmax = jnp.concatenate([_col_reduce(jnp.max, z_lo), zmax_sc[a][:, upper]], axis=1)
                m_new = jnp.maximum(m_sc[a], zmax + f_t[a])
                shift = m_new - f_t[a]
                alpha = jnp.exp2(m_sc[a] - m_new)
                p_sc[a, :, :tk] = jnp.exp2(z_lo - shift[:, :tk]).astype(BF16)
                p_sc[a, :, upper] = jnp.exp2(z_sc[slot, a, :, upper] - shift[:, upper]).astype(BF16)
            else:
                m_new, alpha, p = softmax(z_sc[slot, a], m_sc[a], zmax_sc[a], f_t[a], False)
                p_sc[a] = p
                zmax_sc[a] = zmax_next[a]
            m_sc[a] = m_new
            acc_sc[a] = alpha * (acc_sc[a] + pvs[a])

    @pl.when((pl.program_id(0) == 0) & (hg == 0) & (qi == 0))
    def _():
        p_sc[...] = jnp.zeros_like(p_sc)
        skip_sm[0] = 0
        for a in range(hp):
            z0 = qk(a, 0, qa[a])
            z_sc[0, a] = z0
            zmax_sc[a] = _col_reduce(jnp.max, z0)

    for a in range(hp):
        m_sc[a] = jnp.full((1, tq), -jnp.inf, F32)
        acc_sc[a] = jnp.zeros((HEAD_DIM + BF16_SUBLANES, tq), F32)

    start = skip_sm[0]

    @pl.when((start & 1) == 1)
    def _():
        trip(start, 1)

    @pl.loop((start + 1) >> 1, qi)
    def _(i):
        trip(2 * i, 0)
        trip(2 * i + 1, 1)

    first = 2 * qi
    trip(first, 0, masked=True)
    for a in range(hp):
        acc = acc_sc[a] + pv(a, first, p_sc[a])
        _, alpha, p = softmax(z_sc[1, a, :, upper], m_sc[a][:, upper], None, f_t[a][:, upper], True)
        acc_up = alpha * acc[:, upper] + pv(a, first + 1, p)
        out = jnp.concatenate(
            [acc[:HEAD_DIM, :tk] / acc[HEAD_DIM:HEAD_DIM + 1, :tk],
             acc_up[:HEAD_DIM] / acc_up[HEAD_DIM:HEAD_DIM + 1]], axis=1)
        o_ref[a] = out.T.astype(BF16)

    start_next = jnp.where(last_tile, 0, first_live_block(jnp.minimum(qi + 1, nq - 1)))
    skip_sm[0] = start_next
    c0 = pl.multiple_of(start_next * tk, tk)
    for a in range(hp):
        ka_here = jnp.concatenate([k_ref[a, pl.ds(c0, tk), :], kf_ref[pl.ds(c0, tk), :]], axis=1)
        ka_group = jnp.concatenate([k0n_ref[a], kf0n_ref[...]], axis=1)
        zn = lax.dot_general(jnp.where(last_tile, ka_group, ka_here), qa_next[a], _NT_DIMS,
                             preferred_element_type=F32)
        z_sc[start_next & 1, a] = zn
        zmax_sc[a] = _col_reduce(jnp.max, zn)


def _attn(qk, kf, vt, ft, nrm, fend, B, S, tq, tk, hp):
    T = B * S
    nq = S // tq
    n_blk = S // tk
    n_grp = N_HEADS // hp
    assert tq == 2 * tk and nrm.shape[0] == B * n_blk and fend.shape[0] == B * n_blk

    def next_group(b, g):
        gn = jnp.minimum(b * n_grp + g + 1, B * n_grp - 1)
        return gn // n_grp, gn % n_grp

    def q_next(b, g, i):
        bn, gn = next_group(b, g)
        last = i == nq - 1
        return (0, jnp.where(last, gn, g), jnp.where(last, bn * nq, b * nq + i + 1), 0)
    return pl.pallas_call(
        functools.partial(_attn_kernel, tq=tq, tk=tk, hp=hp, nq=nq),
        grid=(B, N_HEADS // hp, nq),
        in_specs=[
            pl.BlockSpec((None, hp, tq, HEAD_DIM), lambda b, g, i: (0, g, b * nq + i, 0)),
            pl.BlockSpec((None, hp, tq, HEAD_DIM), q_next),
            pl.BlockSpec((None, hp, S, HEAD_DIM), lambda b, g, i: (1, g, b, 0)),
            pl.BlockSpec((S, LANES), lambda b, g, i: (b, 0)),
            pl.BlockSpec((None, hp, tk, HEAD_DIM),
                         lambda b, g, i: (1, next_group(b, g)[1], next_group(b, g)[0] * n_blk, 0)),
            pl.BlockSpec((tk, LANES), lambda b, g, i: (next_group(b, g)[0] * n_blk, 0)),
            pl.BlockSpec((hp, HEAD_DIM, S), lambda b, g, i: (g, 0, b)),
            pl.BlockSpec((None, N_HEADS, tq), lambda b, g, i: (b, 0, i)),
            pl.BlockSpec((n_blk, 2, LANES), lambda b, g, i: (b, 0, 0)),
            pl.BlockSpec((n_blk, 1, LANES), lambda b, g, i: (b, 0, 0)),
        ],
        out_specs=pl.BlockSpec((hp, tq, HEAD_DIM), lambda b, g, i: (g, b * nq + i, 0)),
        out_shape=jax.ShapeDtypeStruct((N_HEADS, T, HEAD_DIM), BF16),
        scratch_shapes=[
            pltpu.VMEM((2, hp, tk, tq), F32),
            pltpu.VMEM((hp, tk, tq), BF16),
            pltpu.VMEM((hp, HEAD_DIM + BF16_SUBLANES, tq), F32),
            pltpu.VMEM((hp, 1, tq), F32),
            pltpu.VMEM((hp, 1, tq), F32),
            pltpu.SMEM((1,), jnp.int32),
        ],
        compiler_params=pltpu.CompilerParams(
            dimension_semantics=("arbitrary", "arbitrary", "arbitrary"),
            vmem_limit_bytes=VMEM_LIMIT),
        name="fox_attn",
    )(qk, qk, qk, kf, qk, kf, vt, ft, nrm, fend)


def _merge_mlp_kernel(ob_ref, yag_ref, gb_ref, x_ref, wb_ref, wo_ref, gm_ref, wu_ref,
                      wdn_ref, gf_ref, o_ref):
    ob = jnp.concatenate([ob_ref[h] for h in range(N_HEADS)], axis=1)
    yb = _dot(ob, wb_ref[...])
    mix = yag_ref[...] + gb_ref[...] * yb
    x1 = x_ref[...] + _dot(mix.astype(BF16), wo_ref[...])
    m = _rms(x1, gm_ref[...]).astype(BF16)
    acc = x1
    for c in range(D_FF // D_MODEL):
        cols = slice(c * D_MODEL, (c + 1) * D_MODEL)
        hc = jnp.maximum(_dot(m, wu_ref[:, cols]), 0.0)
        acc = acc + _dot((hc * hc).astype(BF16), wdn_ref[cols, :])
    o_ref[...] = _rms(acc, gf_ref[...])


def _merge_mlp(ob, yag, pf, x2, wb, wo, gm, wu, wdn, gf, tm):
    T = x2.shape[0]
    return pl.pallas_call(
        _merge_mlp_kernel,
        grid=(T // tm,),
        in_specs=[
            pl.BlockSpec((N_HEADS, tm, HEAD_DIM), lambda i: (0, i, 0)),
            pl.BlockSpec((tm, D_MODEL), lambda i: (i, 0)),
            pl.BlockSpec((tm, D_MODEL), lambda i: (i, 3)),
            pl.BlockSpec((tm, D_MODEL), lambda i: (i, 0)),
            _const_spec((D_MODEL, D_MODEL)),
            _const_spec((D_MODEL, D_MODEL)),
            _const_spec((1, D_MODEL)),
            _const_spec((D_MODEL, D_FF)),
            _const_spec((D_FF, D_MODEL)),
            _const_spec((1, D_MODEL)),
        ],
        out_specs=pl.BlockSpec((tm, D_MODEL), lambda i: (i, 0)),
        out_shape=jax.ShapeDtypeStruct((T, D_MODEL), F32),
        compiler_params=pltpu.CompilerParams(
            dimension_semantics=("arbitrary",), vmem_limit_bytes=VMEM_LIMIT),
        name="merge_mlp",
    )(ob, yag, pf, x2, wb, wo, gm, wu, wdn, gf)


def _block_diag_pairs(wa, wx):
    def pair(w):
        w = w.reshape(LRU_BLOCKS // 2, 2, LRU_BW, LRU_BW)
        z = jnp.zeros_like(w[:, 0])
        top = jnp.concatenate([w[:, 0], z], axis=2)
        bot = jnp.concatenate([z, w[:, 1]], axis=2)
        return jnp.concatenate([top, bot], axis=1)
    return jnp.concatenate([pair(wa), pair(wx)], axis=2).astype(BF16)


def kernel(x, norm_mix_g, w_in, conv_w, conv_b, lru_wa, lru_ba, lru_wx, lru_bx, lru_lambda,
           forget_b, w_branch_a, w_branch_b, w_out, norm_mlp_g, w_up, w_down, norm_final_g):
    B, S, D = x.shape
    assert D == D_MODEL
    T = B * S
    tm = min(512, S)
    tq = min(1024, S)
    tk = tq // 2
    assert S % tm == 0 and S % tq == 0 and tm % LANES == 0

    w_main = w_in.astype(BF16)
    w_vt = w_in[:, W_IN_V_SLAB * D:(W_IN_V_SLAB + 1) * D].T.astype(BF16)
    w_f = jnp.pad(w_in[:, 7 * D:], ((0, 0), (0, LANES - N_HEADS))).astype(BF16)
    wd = _block_diag_pairs(0.5 * lru_wa, 0.5 * lru_wx)
    row = lambda v: v.reshape(1, -1).astype(F32)
    fb = jnp.pad(forget_b, (0, LANES - N_HEADS)).reshape(1, LANES)
    e_rows = jnp.arange(F_PARTS * LANES)
    e_tgt = jnp.where(e_rows % LANES < N_HEADS, (e_rows % LANES) * F_PARTS + e_rows // LANES, -1)
    e = (e_tgt[:, None] == jnp.arange(LANES)[None, :]).astype(BF16)

    x2 = x.reshape(T, D)
    pf, qk, vt, fl, nrm = _in_proj(x2, row(norm_mix_g), w_main, w_vt, w_f, tm)
    yag, kf, ft, fend = _lru(pf, fl, conv_w, row(conv_b), wd, row(0.5 * lru_ba), row(0.5 * lru_bx),
                             row(lru_lambda), fb, w_branch_a.astype(BF16), e, B, S, tm)
    assert tk == tm
    ob = _attn(qk, kf, vt, ft, nrm, fend, B, S, tq, tk, hp=2)
    out = _merge_mlp(ob, yag, pf, x2, w_branch_b.astype(BF16), w_out.astype(BF16),
                     row(norm_mlp_g), w_up.astype(BF16), w_down.astype(BF16),
                     row(norm_final_g), tm)
    return out.reshape(B, S, D)
```
